```python
import jax, jax.numpy as jnp
from jax import lax
import numpy as np

D_MODEL = 1024
BATCH = 8
SEQ = 2048
DEPTH = 4

N_MIXERS = 3
N_FOX_LAYERS = (DEPTH + 2) // 3
N_HGRN_LAYERS = (DEPTH + 1) // 3
N_CONV_LAYERS = DEPTH // 3

FOX_HEADS = 16
FOX_HEAD_DIM = D_MODEL // FOX_HEADS
FOX_BLOCK = 128
FOX_IN_WIDTH = 3 * D_MODEL + FOX_HEADS
FORGET_BIAS_INIT = 1.0

HGRN_EXPAND = 128
HGRN_HEADS = D_MODEL // HGRN_EXPAND
HGRN_KEY_DIM = HGRN_EXPAND
HGRN_VAL_DIM = D_MODEL // HGRN_HEADS
HGRN_KEY_WIDTH = HGRN_HEADS * HGRN_KEY_DIM
HGRN_VAL_WIDTH = HGRN_HEADS * HGRN_VAL_DIM
HGRN_IN_WIDTH = 2 * HGRN_KEY_WIDTH + 2 * HGRN_VAL_WIDTH
HGRN_CHUNK = 64

CONV_WIDTH = 3

N_GROUPS = 4
EXPERTS_PER_GROUP = 8
N_EXPERTS = N_GROUPS * EXPERTS_PER_GROUP
TOP_K = 2
EXPERT_DIM = 256

DEEPNORM_ALPHA = (2 * DEPTH) ** 0.25
DEEPNORM_BETA = (8 * DEPTH) ** -0.25
LN_EPS = 1e-5
RMS_EPS = 1e-6

kernel_name = 'hybrid_fox_hgrn2_shortconv_hmoe_deepnorm'


def layer_norm(x, g, b):
    xf = x.astype(jnp.float32)
    mu = jnp.mean(xf, axis=-1, keepdims=True)
    var = jnp.mean(jnp.square(xf - mu), axis=-1, keepdims=True)
    return ((xf - mu) * lax.rsqrt(var + LN_EPS) * g + b).astype(x.dtype)


def forgetting_attention(x, w_in, b_f, w_out):
    bsz, t_len, _ = x.shape
    proj = x @ w_in
    q, k, v, f_logit = jnp.split(proj, [D_MODEL, 2 * D_MODEL, 3 * D_MODEL], axis=-1)

    def heads(t):
        return t.reshape(bsz, t_len, FOX_HEADS, FOX_HEAD_DIM).transpose(0, 2, 1, 3).astype(jnp.float32)

    q = heads(q) * (FOX_HEAD_DIM ** -0.5)
    k = heads(k)
    v = heads(v)
    log_f = jax.nn.log_sigmoid((f_logit + b_f).astype(jnp.float32))
    c = jnp.cumsum(log_f, axis=1).transpose(0, 2, 1)

    n_blocks = t_len // FOX_BLOCK
    q_blocks = q.reshape(bsz, FOX_HEADS, n_blocks, FOX_BLOCK, FOX_HEAD_DIM).transpose(2, 0, 1, 3, 4)
    c_blocks = c.reshape(bsz, FOX_HEADS, n_blocks, FOX_BLOCK).transpose(2, 0, 1, 3)
    qpos_blocks = jnp.arange(t_len).reshape(n_blocks, FOX_BLOCK)
    kpos = jnp.arange(t_len)

    def query_block(args):
        qb, cb, qp = args
        s = jnp.einsum('bhqd,bhkd->bhqk', qb, k) + (cb[..., :, None] - c[:, :, None, :])
        s = jnp.where(kpos[None, :] <= qp[:, None], s, -jnp.inf)
        p = jax.nn.softmax(s, axis=-1)
        return jnp.einsum('bhqk,bhkd->bhqd', p, v)

    o = lax.map(query_block, (q_blocks, c_blocks, qpos_blocks))
    o = o.transpose(1, 0, 3, 2, 4).reshape(bsz, t_len, D_MODEL).astype(x.dtype)
    return o @ w_out


def hgrn2(x, w_in, lower_bound, norm_g, w_out):
    bsz, t_len, _ = x.shape
    q, f_logit, i, g = jnp.split(
        x @ w_in,
        [HGRN_KEY_WIDTH, 2 * HGRN_KEY_WIDTH, 2 * HGRN_KEY_WIDTH + HGRN_VAL_WIDTH], axis=-1)
    f = lower_bound + (1.0 - lower_bound) * jax.nn.sigmoid(f_logit.astype(jnp.float32))
    log_f = jnp.log(f)
    k = 1.0 - f

    n_chunks = t_len // HGRN_CHUNK

    def chunks(t, d):
        return t.astype(jnp.float32).reshape(bsz, n_chunks, HGRN_CHUNK, HGRN_HEADS, d).transpose(1, 0, 3, 2, 4)

    qc = chunks(q, HGRN_KEY_DIM)
    kc = chunks(k, HGRN_KEY_DIM)
    gc = chunks(log_f, HGRN_KEY_DIM)
    ic = chunks(i, HGRN_VAL_DIM)
    causal = jnp.tril(jnp.ones((HGRN_CHUNK, HGRN_CHUNK), dtype=bool))

    def step(state, inp):
        qb, kb, gb, ib = inp
        b = jnp.cumsum(gb, axis=-2)
        diff = b[:, :, :, None, :] - b[:, :, None, :, :]
        decay = jnp.exp(jnp.where(causal[:, :, None], diff, -jnp.inf))
        scores = jnp.einsum('bhtk,bhsk,bhtsk->bhts', qb, kb, decay)
        o = (jnp.einsum('bhts,bhsv->bhtv', scores, ib)
             + jnp.einsum('bhtk,bhkv->bhtv', qb * jnp.exp(b), state))
        b_last = b[:, :, -1:, :]
        state = (jnp.exp(b_last[:, :, 0, :])[..., None] * state
                 + jnp.einsum('bhsk,bhsv->bhkv', kb * jnp.exp(b_last - b), ib))
        return state, o

    s0 = jnp.zeros((bsz, HGRN_HEADS, HGRN_KEY_DIM, HGRN_VAL_DIM), jnp.float32)
    _, o = lax.scan(step, s0, (qc, kc, gc, ic))
    o = o.transpose(1, 0, 3, 2, 4).reshape(bsz, t_len, HGRN_HEADS, HGRN_VAL_DIM)
    o = o * lax.rsqrt(jnp.mean(jnp.square(o), axis=-1, keepdims=True) + RMS_EPS) * norm_g
    o = (o.reshape(bsz, t_len, HGRN_VAL_WIDTH) * jax.nn.silu(g.astype(jnp.float32))).astype(x.dtype)
    return o @ w_out


def short_conv(x, w_in, conv_w, w_out):
    b_gate, c_gate, h = jnp.split(x @ w_in, 3, axis=-1)
    z = c_gate * h
    y = lax.conv_general_dilated(
        z, conv_w[:, None, :], window_strides=(1,), padding=((CONV_WIDTH - 1, 0),),
        dimension_numbers=('NWC', 'WIO', 'NWC'), feature_group_count=D_MODEL)
    return (b_gate * y) @ w_out


def hierarchical_moe(x, w_group, b_group, w_expert, b_expert, w_gate, w_up, w_down):
    bsz, t_len, _ = x.shape
    grp_logits = (x @ w_group).astype(jnp.float32) + b_group
    grp_prob = jax.nn.softmax(grp_logits, axis=-1)
    g_sel = jnp.argmax(grp_logits, axis=-1)
    grp_w = jnp.max(grp_prob, axis=-1, keepdims=True)
    exp_logits = ((x @ w_expert).astype(jnp.float32) + b_expert).reshape(
        bsz, t_len, N_GROUPS, EXPERTS_PER_GROUP)
    in_grp = jnp.einsum('btg,btge->bte', jax.nn.one_hot(g_sel, N_GROUPS, dtype=jnp.float32), exp_logits)
    p_in = jax.nn.softmax(in_grp, axis=-1)
    top_p, top_i = lax.top_k(p_in, TOP_K)
    weights = top_p / jnp.sum(top_p, axis=-1, keepdims=True) * grp_w
    expert_ids = g_sel[..., None] * EXPERTS_PER_GROUP + top_i
    gates = jnp.sum(jax.nn.one_hot(expert_ids, N_EXPERTS, dtype=jnp.float32) * weights[..., None], axis=-2)
    hg = jnp.einsum('btd,edf->btef', x, w_gate)
    hu = jnp.einsum('btd,edf->btef', x, w_up)
    h = jax.nn.silu(hg) * hu * gates[..., None].astype(x.dtype)
    return jnp.einsum('btef,efd->btd', h, w_down)


def _normal(key, shape, scale):
    return jax.random.normal(key, shape, jnp.float32) * scale


def setup_inputs(seed: int = 0) -> dict:
    key = jax.random.key(seed)
    ks = jax.random.split(key, 22)
    d = D_MODEL
    return {
        'x': _normal(ks[0], (BATCH, SEQ, d), 1.0),
        'ln_mix_g': 1.0 + _normal(ks[1], (DEPTH, d), 0.02),
        'ln_mix_b': _normal(ks[2], (DEPTH, d), 0.02),
        'ln_ffn_g': 1.0 + _normal(ks[3], (DEPTH, d), 0.02),
        'ln_ffn_b': _normal(ks[4], (DEPTH, d), 0.02),
        'fox_w_in': _normal(ks[5], (N_FOX_LAYERS, d, FOX_IN_WIDTH), d ** -0.5),
        'fox_b_f': FORGET_BIAS_INIT + _normal(ks[6], (N_FOX_LAYERS, FOX_HEADS), 0.1),
        'fox_w_out': _normal(ks[7], (N_FOX_LAYERS, d, d), d ** -0.5 * DEEPNORM_BETA),
        'hgrn_w_in': _normal(ks[8], (N_HGRN_LAYERS, d, HGRN_IN_WIDTH), d ** -0.5),
        'hgrn_lb_logits': _normal(ks[9], (DEPTH, HGRN_KEY_WIDTH), 0.1),
        'hgrn_norm_g': 1.0 + _normal(ks[10], (N_HGRN_LAYERS, HGRN_VAL_DIM), 0.02),
        'hgrn_w_out': _normal(ks[11], (N_HGRN_LAYERS, HGRN_VAL_WIDTH, d), HGRN_VAL_WIDTH ** -0.5 * DEEPNORM_BETA),
        'conv_w_in': _normal(ks[12], (N_CONV_LAYERS, d, 3 * d), d ** -0.5),
        'conv_w': _normal(ks[13], (N_CONV_LAYERS, CONV_WIDTH, d), CONV_WIDTH ** -0.5),
        'conv_w_out': _normal(ks[14], (N_CONV_LAYERS, d, d), d ** -0.5 * DEEPNORM_BETA),
        'moe_w_group': _normal(ks[15], (DEPTH, d, N_GROUPS), d ** -0.5),
        'moe_b_group': _normal(ks[16], (DEPTH, N_GROUPS), 0.01),
        'moe_w_expert': _normal(ks[17], (DEPTH, d, N_EXPERTS), d ** -0.5),
        'moe_b_expert': _normal(ks[18], (DEPTH, N_EXPERTS), 0.01),
        'moe_w_gate': _normal(ks[19], (DEPTH, N_EXPERTS, d, EXPERT_DIM), d ** -0.5),
        'moe_w_up': _normal(ks[20], (DEPTH, N_EXPERTS, d, EXPERT_DIM), d ** -0.5),
        'moe_w_down': _normal(ks[21], (DEPTH, N_EXPERTS, EXPERT_DIM, d), EXPERT_DIM ** -0.5 * DEEPNORM_BETA),
    }


def reference(x, ln_mix_g, ln_mix_b, ln_ffn_g, ln_ffn_b,
              fox_w_in, fox_b_f, fox_w_out,
              hgrn_w_in, hgrn_lb_logits, hgrn_norm_g, hgrn_w_out,
              conv_w_in, conv_w, conv_w_out,
              moe_w_group, moe_b_group, moe_w_expert, moe_b_expert,
              moe_w_gate, moe_w_up, moe_w_down):
    lb_prob = jax.nn.softmax(hgrn_lb_logits.astype(jnp.float32), axis=0)
    lower_bounds = jnp.cumsum(lb_prob, axis=0) - lb_prob[0]

    for layer in range(DEPTH):
        kind = layer % N_MIXERS
        j = layer // N_MIXERS
        if kind == 0:
            mixed = forgetting_attention(x, fox_w_in[j], fox_b_f[j], fox_w_out[j])
        elif kind == 1:
            mixed = hgrn2(x, hgrn_w_in[j], lower_bounds[layer], hgrn_norm_g[j], hgrn_w_out[j])
        else:
            mixed = short_conv(x, conv_w_in[j], conv_w[j], conv_w_out[j])
        x = layer_norm(DEEPNORM_ALPHA * x + mixed, ln_mix_g[layer], ln_mix_b[layer])
        ffn = hierarchical_moe(x, moe_w_group[layer], moe_b_group[layer], moe_w_expert[layer],
                               moe_b_expert[layer], moe_w_gate[layer], moe_w_up[layer], moe_w_down[layer])
        x = layer_norm(DEEPNORM_ALPHA * x + ffn, ln_ffn_g[layer], ln_ffn_b[layer])
    return x
```

```python
import functools

import numpy as np
import jax
import jax.numpy as jnp
from jax import lax
from jax.experimental import pallas as pl
from jax.experimental.pallas import tpu as pltpu

F32 = jnp.float32
BF16 = jnp.bfloat16

FOX_HEADS = 16
FOX_HEAD_DIM = 64
HGRN_HEADS = 8
HGRN_DIM = 128
HGRN_CHUNK = 128
N_GROUPS = 4
EXPERTS_PER_GROUP = 8
N_EXPERTS = N_GROUPS * EXPERTS_PER_GROUP
LN_EPS = 1e-5
RMS_EPS = 1e-6
LANES = 128
ROUTER_GROUP_LANE0 = N_EXPERTS

_NT = (((1,), (1,)), ((), ()))
_TN = (((0,), (0,)), ((), ()))


def _params(semantics, vmem_mb):
    return pltpu.CompilerParams(dimension_semantics=semantics,
                                vmem_limit_bytes=vmem_mb * 1024 * 1024)


def _layer_norm(v, g, b):
    mu = jnp.mean(v, axis=-1, keepdims=True)
    d = v - mu
    var = jnp.mean(d * d, axis=-1, keepdims=True)
    return d * lax.rsqrt(var + LN_EPS) * g + b


def _sigmoid(v):
    return 1.0 / (1.0 + jnp.exp(-v))


def _proj_kernel(x_ref, *refs, n_out):
    xb = x_ref[...].astype(BF16)
    for w_ref, o_ref in zip(refs[:n_out], refs[n_out:]):
        o_ref[...] = jnp.dot(xb, w_ref[...], preferred_element_type=F32).astype(o_ref.dtype)


def _proj(x, ws, dtypes, name, tm=512):
    n, k = x.shape
    return pl.pallas_call(
        functools.partial(_proj_kernel, n_out=len(ws)),
        grid=(n // tm,),
        in_specs=[pl.BlockSpec((tm, k), lambda i: (i, 0))]
        + [pl.BlockSpec(w.shape, lambda i: (0, 0)) for w in ws],
        out_specs=[pl.BlockSpec((tm, w.shape[1]), lambda i: (i, 0)) for w in ws],
        out_shape=[jax.ShapeDtypeStruct((n, w.shape[1]), dt) for w, dt in zip(ws, dtypes)],
        compiler_params=_params(("arbitrary",), 48),
        name=name,
    )(x, *ws)


def _outproj_ln_kernel(o_ref, w_ref, x_ref, g_ref, b_ref, y_ref, *, alpha):
    mixed = jnp.dot(o_ref[...], w_ref[...], preferred_element_type=F32)
    y_ref[...] = _layer_norm(alpha * x_ref[...] + mixed, g_ref[...], b_ref[...])


def _outproj_ln(o, w, x, g, b, alpha, name, tm=512):
    n, d = x.shape
    k = o.shape[1]
    row = lambda i: (i, 0)
    fixed = lambda i: (0, 0)
    return pl.pallas_call(
        functools.partial(_outproj_ln_kernel, alpha=alpha),
        grid=(n // tm,),
        in_specs=[pl.BlockSpec((tm, k), row), pl.BlockSpec((k, d), fixed),
                  pl.BlockSpec((tm, d), row), pl.BlockSpec((1, d), fixed), pl.BlockSpec((1, d), fixed)],
        out_specs=pl.BlockSpec((tm, d), row),
        out_shape=jax.ShapeDtypeStruct((n, d), F32),
        compiler_params=_params(("arbitrary",), 32),
        name=name,
    )(o, w, x, g, b)


def _fox_gate_kernel(x_ref, wft_ref, bf_ref, u_ref, c_ref, carry_ref):
    @pl.when(pl.program_id(1) == 0)
    def _():
        carry_ref[...] = jnp.zeros_like(carry_ref)

    z = lax.dot_general(wft_ref[...], x_ref[...].astype(BF16), _NT,
                        preferred_element_type=F32) + bf_ref[...]
    logf = jnp.minimum(z, 0.0) - jnp.log(1.0 + jnp.exp(-jnp.abs(z)))
    hi = logf.astype(BF16)
    r1 = logf - hi.astype(F32)
    mid = r1.astype(BF16)
    lo = (r1 - mid.astype(F32)).astype(BF16)
    u = u_ref[...]
    c = (jnp.dot(hi, u, preferred_element_type=F32) + jnp.dot(mid, u, preferred_element_type=F32)
         + jnp.dot(lo, u, preferred_element_type=F32)) + carry_ref[...]
    c_ref[...] = c
    carry_ref[...] = c[:, c.shape[1] - 1:]


def _fox_gate(x, wft, bf, bsz, t_len, tg=512):
    n, d = x.shape
    nt = t_len // tg
    u = jnp.asarray(np.triu(np.ones((tg, tg), np.float32)), BF16)
    return pl.pallas_call(
        _fox_gate_kernel,
        grid=(bsz, nt),
        in_specs=[pl.BlockSpec((tg, d), lambda b, i: (b * nt + i, 0)),
                  pl.BlockSpec((FOX_HEADS, d), lambda b, i: (0, 0)),
                  pl.BlockSpec((FOX_HEADS, 1), lambda b, i: (0, 0)),
                  pl.BlockSpec((tg, tg), lambda b, i: (0, 0))],
        out_specs=pl.BlockSpec((FOX_HEADS, tg), lambda b, i: (0, b * nt + i)),
        out_shape=jax.ShapeDtypeStruct((FOX_HEADS, n), F32),
        scratch_shapes=[pltpu.VMEM((FOX_HEADS, 1), F32)],
        compiler_params=_params(("arbitrary", "arbitrary"), 32),
        name="fox_gate",
    )(x, wft, bf, u)


def _fox_attn_kernel(q_ref, k_ref, v_ref, c_ref, o_ref, *, tq):
    hp = pl.program_id(1)
    qi = pl.program_id(2)
    lane = lax.broadcasted_iota(jnp.int32, (1, LANES), 1)
    is_a = lane < FOX_HEAD_DIM
    q2 = q_ref[...]
    zero = jnp.zeros_like(q2)
    q_heads = (jnp.where(is_a, q2, zero), jnp.where(is_a, zero, q2))
    row = lax.broadcasted_iota(jnp.int32, (tq, tq), 0)
    col = lax.broadcasted_iota(jnp.int32, (tq, tq), 1)

    def block(j, carry, masked):
        ms, ls, acc = carry
        start = pl.multiple_of(j * tq, tq)
        k2 = k_ref[pl.ds(start, tq), :]
        v2 = v_ref[pl.ds(start, tq), :]
        new_ms, new_ls, scales, pvs = [], [], [], []
        for h in range(2):
            c_row = c_ref[pl.ds(2 * hp + h, 1), pl.ds(start, tq)]
            s = lax.dot_general(q_heads[h], k2, _NT, preferred_element_type=F32) - c_row
            if masked:
                s = jnp.where(col <= row, s, -jnp.inf)
            m_new = jnp.maximum(ms[h], jnp.max(s, axis=-1, keepdims=True))
            p = jnp.exp(s - m_new)
            a = jnp.exp(ms[h] - m_new)
            new_ms.append(m_new)
            new_ls.append(a * ls[h] + jnp.sum(p, axis=-1, keepdims=True))
            scales.append(a)
            pvs.append(jnp.dot(p.astype(BF16), v2, preferred_element_type=F32))
        acc = acc * jnp.where(is_a, scales[0], scales[1]) + jnp.where(is_a, pvs[0], pvs[1])
        return tuple(new_ms), tuple(new_ls), acc

    m0 = jnp.full((tq, 1), -jnp.inf, F32)
    l0 = jnp.zeros((tq, 1), F32)
    carry = ((m0, m0), (l0, l0), jnp.zeros((tq, LANES), F32))
    carry = lax.fori_loop(0, qi, lambda j, c: block(j, c, False), carry)
    _, ls, acc = block(qi, carry, True)
    o_ref[...] = (acc / jnp.where(is_a, ls[0], ls[1])).astype(o_ref.dtype)


def _fox_attn(qkv, c, bsz, t_len, tq=256):
    n = qkv.shape[0]
    d = FOX_HEADS * FOX_HEAD_DIM
    n_pairs = d // LANES
    nq = t_len // tq
    return pl.pallas_call(
        functools.partial(_fox_attn_kernel, tq=tq),
        grid=(bsz, n_pairs, nq),
        in_specs=[pl.BlockSpec((tq, LANES), lambda b, p, i: (b * nq + i, p)),
                  pl.BlockSpec((t_len, LANES), lambda b, p, i: (b, n_pairs + p)),
                  pl.BlockSpec((t_len, LANES), lambda b, p, i: (b, 2 * n_pairs + p)),
                  pl.BlockSpec((FOX_HEADS, t_len), lambda b, p, i: (0, b))],
        out_specs=pl.BlockSpec((tq, LANES), lambda b, p, i: (b * nq + i, p)),
        out_shape=jax.ShapeDtypeStruct((n, d), BF16),
        compiler_params=_params(("arbitrary", "arbitrary", "arbitrary"), 32),
        name="fox_attn",
    )(qkv, qkv, qkv, c)


def _fox_layer(x, w_in, b_f, w_out, g, b, alpha, bsz, t_len):
    d = x.shape[1]
    w_qkv = jnp.concatenate([w_in[:, :d] * (FOX_HEAD_DIM ** -0.5), w_in[:, d:3 * d]], axis=1).astype(BF16)
    wft = w_in[:, 3 * d:].T.astype(BF16)
    (qkv,) = _proj(x, [w_qkv], [BF16], "fox_proj")
    c = _fox_gate(x, wft, b_f.reshape(FOX_HEADS, 1), bsz, t_len)
    o = _fox_attn(qkv, c, bsz, t_len)
    return _outproj_ln(o, w_out.astype(BF16), x, g, b, alpha, "fox_out_ln")


def _hgrn_constants():
    c = HGRN_CHUNK
    r = np.arange(c)[:, None]
    j = np.arange(c)[None, :]
    blocks = [j <= r, j > r]
    masks = []
    levels = []
    half = c // 2
    while half >= 1:
        ref = (r // (2 * half)) * (2 * half) + half - 1
        upper = (r % (2 * half)) >= half
        blocks.append(np.where(upper, (j > ref) & (j <= r), (j > r) & (j <= ref)))
        masks.append(((r // (2 * half)) == (j // (2 * half))) & upper & ((j % (2 * half)) < half))
        levels.append(half)
        half //= 2
    masks.append(r == j)
    wall = np.concatenate(blocks, axis=0).astype(np.float32)
    wall2 = np.concatenate([wall, wall], axis=1)
    return wall2, np.stack(masks).astype(np.float32), tuple(levels)


def _hgrn_kernel(q_ref, fl_ref, i_ref, g_ref, lb_ref, ng_ref, wall_ref, mask_ref, o_ref, s_ref, *, levels):
    c = HGRN_CHUNK

    @pl.when(pl.program_id(1) == 0)
    def _():
        s_ref[...] = jnp.zeros_like(s_ref)

    wall = wall_ref[...]
    rowi = lax.broadcasted_iota(jnp.int32, (c, HGRN_DIM), 0)
    for h in range(HGRN_HEADS):
        sl = slice(h * HGRN_DIM, (h + 1) * HGRN_DIM)
        q = q_ref[:, sl].astype(F32)
        i_b = i_ref[:, sl]
        lb = lb_ref[:, sl]
        f = lb + (1.0 - lb) * _sigmoid(fl_ref[:, sl])
        logf = jnp.log(f)
        k = 1.0 - f
        hi = logf.astype(BF16)
        mid = (logf - hi.astype(F32)).astype(BF16)
        x_all = jnp.exp(jnp.dot(wall, jnp.concatenate([hi, mid], axis=0), preferred_element_type=F32))
        x_pre = x_all[0:c]
        x_suf = x_all[c:2 * c]
        st = s_ref[h]
        o = lax.dot_general((q * x_pre).astype(BF16), st.astype(BF16), _NT, preferred_element_type=F32)
        upd = lax.dot_general(i_b, (k * x_suf).astype(BF16), _TN, preferred_element_type=F32)
        s_ref[h] = st * x_pre[c - 1:c, :] + upd
        a = jnp.zeros((c, c), F32)
        for l, half in enumerate(levels):
            upper = (rowi & half) != 0
            z = (jnp.where(upper, q, k) * x_all[(2 + l) * c:(3 + l) * c]).astype(BF16)
            a = a + lax.dot_general(z, z, _NT, preferred_element_type=F32) * mask_ref[l]
        a = a + lax.dot_general(q.astype(BF16), k.astype(BF16), _NT,
                                preferred_element_type=F32) * mask_ref[len(levels)]
        o = o + jnp.dot(a.astype(BF16), i_b, preferred_element_type=F32)
        o = o * lax.rsqrt(jnp.mean(o * o, axis=-1, keepdims=True) + RMS_EPS) * ng_ref[...]
        gate = g_ref[:, sl].astype(F32)
        o_ref[:, sl] = (o * (gate * _sigmoid(gate))).astype(o_ref.dtype)


def _hgrn_core(q, fl, i, g, lb, ng, bsz, t_len):
    n, d = q.shape
    c = HGRN_CHUNK
    nc = t_len // c
    wall2, masks, levels = _hgrn_constants()
    row = lambda b, j: (b * nc + j, 0)
    fixed2 = lambda b, j: (0, 0)
    return pl.pallas_call(
        functools.partial(_hgrn_kernel, levels=levels),
        grid=(bsz, nc),
        in_specs=[pl.BlockSpec((c, d), row), pl.BlockSpec((c, d), row), pl.BlockSpec((c, d), row),
                  pl.BlockSpec((c, d), row), pl.BlockSpec((1, d), fixed2), pl.BlockSpec((1, HGRN_DIM), fixed2),
                  pl.BlockSpec(wall2.shape, fixed2), pl.BlockSpec(masks.shape, lambda b, j: (0, 0, 0))],
        out_specs=pl.BlockSpec((c, d), row),
        out_shape=jax.ShapeDtypeStruct((n, d), BF16),
        scratch_shapes=[pltpu.VMEM((HGRN_HEADS, HGRN_DIM, HGRN_DIM), F32)],
        compiler_params=_params(("arbitrary", "arbitrary"), 32),
        name="hgrn_core",
    )(q, fl, i, g, lb, ng, jnp.asarray(wall2, BF16), jnp.asarray(masks, F32))


def _hgrn_layer(x, w_in, lower_bound, norm_g, w_out, g, b, alpha, bsz, t_len):
    d = x.shape[1]
    wb = w_in.astype(BF16)
    q, fl, i, gate = _proj(x, [wb[:, :d], wb[:, d:2 * d], wb[:, 2 * d:3 * d], wb[:, 3 * d:]],
                           [BF16, F32, BF16, BF16], "hgrn_proj")
    o = _hgrn_core(q, fl, i, gate, lower_bound.reshape(1, d), norm_g.reshape(1, HGRN_DIM), bsz, t_len)
    return _outproj_ln(o, w_out.astype(BF16), x, g, b, alpha, "hgrn_out_ln")


def _conv_kernel(x_ref, win_ref, cw_ref, wout_ref, g_ref, b_ref, y_ref, zbuf, *, alpha, tm):
    d = x_ref.shape[1]

    @pl.when(pl.program_id(1) == 0)
    def _():
        zbuf[0:8, :] = jnp.zeros((8, d), F32)

    x = x_ref[...]
    p = jnp.dot(x.astype(BF16), win_ref[...], preferred_element_type=F32)
    z = p[:, d:2 * d] * p[:, 2 * d:]
    zbuf[8:8 + tm, :] = z
    y = cw_ref[2:3, :] * z + cw_ref[1:2, :] * zbuf[7:7 + tm, :] + cw_ref[0:1, :] * zbuf[6:6 + tm, :]
    zbuf[0:8, :] = z[tm - 8:, :]
    mixed = jnp.dot((p[:, :d] * y).astype(BF16), wout_ref[...], preferred_element_type=F32)
    y_ref[...] = _layer_norm(alpha * x + mixed, g_ref[...], b_ref[...])


def _conv_layer(x, w_in, conv_w, w_out, g, b, alpha, bsz, t_len, tm=512):
    n, d = x.shape
    nt = t_len // tm
    row = lambda bb, i: (bb * nt + i, 0)
    fixed = lambda bb, i: (0, 0)
    return pl.pallas_call(
        functools.partial(_conv_kernel, alpha=alpha, tm=tm),
        grid=(bsz, nt),
        in_specs=[pl.BlockSpec((tm, d), row), pl.BlockSpec((d, 3 * d), fixed), pl.BlockSpec(conv_w.shape, fixed),
                  pl.BlockSpec((d, d), fixed), pl.BlockSpec((1, d), fixed), pl.BlockSpec((1, d), fixed)],
        out_specs=pl.BlockSpec((tm, d), row),
        out_shape=jax.ShapeDtypeStruct((n, d), F32),
        scratch_shapes=[pltpu.VMEM((tm + 8, d), F32)],
        compiler_params=_params(("arbitrary", "arbitrary"), 48),
        name="conv_layer",
    )(x, w_in.astype(BF16), conv_w, w_out.astype(BF16), g, b)


def _router_kernel(x_ref, wh_ref, wl_ref, b_ref, gates_ref):
    x = x_ref[...]
    xh = x.astype(BF16)
    xl = (x - xh.astype(F32)).astype(BF16)
    wh = wh_ref[...]
    logits = (jnp.dot(xh, wh, preferred_element_type=F32) + jnp.dot(xh, wl_ref[...], preferred_element_type=F32)
              + jnp.dot(xl, wh, preferred_element_type=F32)) + b_ref[...]
    lane = lax.broadcasted_iota(jnp.int32, logits.shape, 1)
    lane_f = lane.astype(F32)
    neg = -jnp.inf
    far = float(LANES)
    is_g = (lane >= ROUTER_GROUP_LANE0) & (lane < ROUTER_GROUP_LANE0 + N_GROUPS)
    gl = jnp.where(is_g, logits, neg)
    gmax = jnp.max(gl, axis=-1, keepdims=True)
    g_sel = jnp.min(jnp.where(gl == gmax, lane_f, far), axis=-1, keepdims=True) - float(ROUTER_GROUP_LANE0)
    grp_w = 1.0 / jnp.sum(jnp.where(is_g, jnp.exp(logits - gmax), 0.0), axis=-1, keepdims=True)
    in_g = (lane < N_EXPERTS) & ((lane // EXPERTS_PER_GROUP).astype(F32) == g_sel)
    el = jnp.where(in_g, logits, neg)
    t1 = jnp.max(el, axis=-1, keepdims=True)
    i1 = jnp.min(jnp.where(el == t1, lane_f, far), axis=-1, keepdims=True)
    el2 = jnp.where(lane_f == i1, neg, el)
    t2 = jnp.max(el2, axis=-1, keepdims=True)
    i2 = jnp.min(jnp.where(el2 == t2, lane_f, far), axis=-1, keepdims=True)
    e2 = jnp.exp(t2 - t1)
    w1 = grp_w / (1.0 + e2)
    gates_ref[...] = jnp.where(lane_f == i1, w1, 0.0) + jnp.where(lane_f == i2, w1 * e2, 0.0)


def _router(x, w_group, b_group, w_expert, b_expert, tm=512):
    n, d = x.shape
    w = jnp.zeros((d, LANES), F32).at[:, :N_EXPERTS].set(w_expert)
    w = w.at[:, ROUTER_GROUP_LANE0:ROUTER_GROUP_LANE0 + N_GROUPS].set(w_group)
    bias = jnp.zeros((1, LANES), F32).at[0, :N_EXPERTS].set(b_expert)
    bias = bias.at[0, ROUTER_GROUP_LANE0:ROUTER_GROUP_LANE0 + N_GROUPS].set(b_group)
    wh = w.astype(BF16)
    wl = (w - wh.astype(F32)).astype(BF16)
    row = lambda i: (i, 0)
    fixed = lambda i: (0, 0)
    return pl.pallas_call(
        _router_kernel,
        grid=(n // tm,),
        in_specs=[pl.BlockSpec((tm, d), row), pl.BlockSpec((d, LANES), fixed), pl.BlockSpec((d, LANES), fixed),
                  pl.BlockSpec((1, LANES), fixed)],
        out_specs=pl.BlockSpec((tm, LANES), row),
        out_shape=jax.ShapeDtypeStruct((n, LANES), F32),
        compiler_params=_params(("arbitrary",), 32),
        name="moe_router",
    )(x, wh, wl, bias)


def _moe_kernel(x_ref, gates_ref, wg_ref, wu_ref, wd_ref, g_ref, b_ref, y_ref, xb_ref, acc_ref, *, alpha):
    e = pl.program_id(1)

    @pl.when(e == 0)
    def _():
        xb_ref[...] = x_ref[...].astype(BF16)
        acc_ref[...] = jnp.zeros_like(acc_ref)

    gates = gates_ref[...]
    lane = lax.broadcasted_iota(jnp.int32, gates.shape, 1)
    gate = jnp.sum(jnp.where(lane == e, gates, 0.0), axis=-1, keepdims=True)
    xb = xb_ref[...]
    hg = jnp.dot(xb, wg_ref[0, 0].astype(BF16), preferred_element_type=F32)
    hu = jnp.dot(xb, wu_ref[0, 0].astype(BF16), preferred_element_type=F32)
    h = hg * _sigmoid(hg) * hu * gate
    acc_ref[...] += jnp.dot(h.astype(BF16), wd_ref[0, 0].astype(BF16), preferred_element_type=F32)

    @pl.when(e == pl.num_programs(1) - 1)
    def _():
        y_ref[...] = _layer_norm(alpha * x_ref[...] + acc_ref[...], g_ref[...], b_ref[...])


def _moe_ln(x, gates, w_gate, w_up, w_down, layer, g, b, alpha, tm=1024):
    n, d = x.shape
    f = w_gate.shape[-1]
    row = lambda i, e: (i, 0)
    fixed = lambda i, e: (0, 0)
    return pl.pallas_call(
        functools.partial(_moe_kernel, alpha=alpha),
        grid=(n // tm, N_EXPERTS),
        in_specs=[pl.BlockSpec((tm, d), row), pl.BlockSpec((tm, LANES), row),
                  pl.BlockSpec((1, 1, d, f), lambda i, e: (layer, e, 0, 0)),
                  pl.BlockSpec((1, 1, d, f), lambda i, e: (layer, e, 0, 0)),
                  pl.BlockSpec((1, 1, f, d), lambda i, e: (layer, e, 0, 0)),
                  pl.BlockSpec((1, d), fixed), pl.BlockSpec((1, d), fixed)],
        out_specs=pl.BlockSpec((tm, d), row),
        out_shape=jax.ShapeDtypeStruct((n, d), F32),
        scratch_shapes=[pltpu.VMEM((tm, d), BF16), pltpu.VMEM((tm, d), F32)],
        compiler_params=_params(("arbitrary", "arbitrary"), 48),
        name="moe_experts_ln",
    )(x, gates, w_gate, w_up, w_down, g, b)


def kernel(x, ln_mix_g, ln_mix_b, ln_ffn_g, ln_ffn_b, fox_w_in, fox_b_f, fox_w_out, hgrn_w_in, hgrn_lb_logits, hgrn_norm_g, hgrn_w_out, conv_w_in, conv_w, conv_w_out, moe_w_group, moe_b_group, moe_w_expert, moe_b_expert, moe_w_gate, moe_w_up, moe_w_down):
    bsz, t_len, d = x.shape
    depth = ln_mix_g.shape[0]
    alpha = float((2 * depth) ** 0.25)
    assert d == FOX_HEADS * FOX_HEAD_DIM == HGRN_HEADS * HGRN_DIM
    assert t_len % 512 == 0 and (bsz * t_len) % 1024 == 0

    lb_prob = jax.nn.softmax(hgrn_lb_logits.astype(F32), axis=0)
    lower_bounds = jnp.cumsum(lb_prob, axis=0) - lb_prob[0]

    h = x.reshape(bsz * t_len, d)
    for layer in range(depth):
        kind, j = layer % 3, layer // 3
        g_mix, b_mix = ln_mix_g[layer].reshape(1, d), ln_mix_b[layer].reshape(1, d)
        if kind == 0:
            h = _fox_layer(h, fox_w_in[j], fox_b_f[j], fox_w_out[j], g_mix, b_mix, alpha, bsz, t_len)
        elif kind == 1:
            h = _hgrn_layer(h, hgrn_w_in[j], lower_bounds[layer], hgrn_norm_g[j], hgrn_w_out[j],
                            g_mix, b_mix, alpha, bsz, t_len)
        else:
            h = _conv_layer(h, conv_w_in[j], conv_w[j], conv_w_out[j], g_mix, b_mix, alpha, bsz, t_len)
        gates = _router(h, moe_w_group[layer], moe_b_group[layer], moe_w_expert[layer], moe_b_expert[layer])
        h = _moe_ln(h, gates, moe_w_gate, moe_w_up, moe_w_down, layer,
                    ln_ffn_g[layer].reshape(1, d), ln_ffn_b[layer].reshape(1, d), alpha)
    return h.reshape(bsz, t_len, d)
```

```python
import functools

import numpy as np
import jax
import jax.numpy as jnp
from jax import lax
from jax.experimental import pallas as pl
from jax.experimental.pallas import tpu as pltpu

F32 = jnp.float32
BF16 = jnp.bfloat16

FOX_HEADS = 16
FOX_HEAD_DIM = 64
HGRN_HEADS = 8
HGRN_DIM = 128
HGRN_CHUNK = 128
N_GROUPS = 4
EXPERTS_PER_GROUP = 8
N_EXPERTS = N_GROUPS * EXPERTS_PER_GROUP
LN_EPS = 1e-5
RMS_EPS = 1e-6
LOG2E = 1.4426950408889634
LANES = 128
ROUTER_GROUP_LANE0 = N_EXPERTS

_NT = (((1,), (1,)), ((), ()))
_TN = (((0,), (0,)), ((), ()))


def _params(semantics, vmem_mb):
    return pltpu.CompilerParams(dimension_semantics=semantics,
                                vmem_limit_bytes=vmem_mb * 1024 * 1024)


def _layer_norm(v, g, b):
    mu = jnp.mean(v, axis=-1, keepdims=True)
    d = v - mu
    var = jnp.mean(d * d, axis=-1, keepdims=True)
    return d * lax.rsqrt(var + LN_EPS) * g + b


def _sigmoid(v):
    return 1.0 / (1.0 + jnp.exp(-v))


def _proj_kernel(x_ref, *refs, n_out):
    xb = x_ref[...].astype(BF16)
    for w_ref, o_ref in zip(refs[:n_out], refs[n_out:]):
        o_ref[...] = jnp.dot(xb, w_ref[...], preferred_element_type=F32).astype(o_ref.dtype)


def _proj(x, ws, dtypes, name, tm=512):
    n, k = x.shape
    return pl.pallas_call(
        functools.partial(_proj_kernel, n_out=len(ws)),
        grid=(n // tm,),
        in_specs=[pl.BlockSpec((tm, k), lambda i: (i, 0))]
        + [pl.BlockSpec(w.shape, lambda i: (0, 0)) for w in ws],
        out_specs=[pl.BlockSpec((tm, w.shape[1]), lambda i: (i, 0)) for w in ws],
        out_shape=[jax.ShapeDtypeStruct((n, w.shape[1]), dt) for w, dt in zip(ws, dtypes)],
        compiler_params=_params(("arbitrary",), 48),
        name=name,
    )(x, *ws)


def _outproj_ln_kernel(o_ref, w_ref, x_ref, g_ref, b_ref, y_ref, *, alpha, transposed):
    dims = _TN if transposed else (((1,), (0,)), ((), ()))
    mixed = lax.dot_general(o_ref[...], w_ref[...], dims, preferred_element_type=F32)
    y_ref[...] = _layer_norm(alpha * x_ref[...] + mixed, g_ref[...], b_ref[...])


def _outproj_ln(o, w, x, g, b, alpha, name, tm=512, transposed=False):
    n, d = x.shape
    k = w.shape[0]
    row = lambda i: (i, 0)
    fixed = lambda i: (0, 0)
    o_spec = pl.BlockSpec((k, tm), lambda i: (0, i)) if transposed else pl.BlockSpec((tm, k), row)
    return pl.pallas_call(
        functools.partial(_outproj_ln_kernel, alpha=alpha, transposed=transposed),
        grid=(n // tm,),
        in_specs=[o_spec, pl.BlockSpec((k, d), fixed),
                  pl.BlockSpec((tm, d), row), pl.BlockSpec((1, d), fixed), pl.BlockSpec((1, d), fixed)],
        out_specs=pl.BlockSpec((tm, d), row),
        out_shape=jax.ShapeDtypeStruct((n, d), F32),
        compiler_params=_params(("arbitrary",), 32),
        name=name,
    )(o, w, x, g, b)


def _split3(v):
    hi = v.astype(BF16)
    r1 = v - hi.astype(F32)
    mid = r1.astype(BF16)
    return hi, mid, (r1 - mid.astype(F32)).astype(BF16)


def _fox_proj_kernel(x_ref, wqt_ref, wk_ref, wvt_ref, qt_ref, k_ref, vt_ref):
    xb = x_ref[...].astype(BF16)
    qt_ref[...] = lax.dot_general(wqt_ref[...], xb, _NT, preferred_element_type=F32).astype(BF16)
    k_ref[...] = jnp.dot(xb, wk_ref[...], preferred_element_type=F32).astype(BF16)
    vt_ref[...] = lax.dot_general(wvt_ref[...], xb, _NT, preferred_element_type=F32).astype(BF16)


def _fox_proj(x, wqt, wk, wvt, tm=512):
    n, d = x.shape
    fixed = lambda i: (0, 0)
    return pl.pallas_call(
        _fox_proj_kernel,
        grid=(n // tm,),
        in_specs=[pl.BlockSpec((tm, d), lambda i: (i, 0)), pl.BlockSpec((d, d), fixed),
                  pl.BlockSpec((d, d), fixed), pl.BlockSpec((d, d), fixed)],
        out_specs=[pl.BlockSpec((d, tm), lambda i: (0, i)), pl.BlockSpec((tm, d), lambda i: (i, 0)),
                   pl.BlockSpec((d, tm), lambda i: (0, i))],
        out_shape=[jax.ShapeDtypeStruct((d, n), BF16), jax.ShapeDtypeStruct((n, d), BF16),
                   jax.ShapeDtypeStruct((d, n), BF16)],
        compiler_params=_params(("arbitrary",), 48),
        name="fox_proj",
    )(x, wqt, wk, wvt)


def _fox_gate_kernel(x_ref, wf_ref, bf_ref, tri_ref, sel_ref, cb_ref, carry_ref):
    @pl.when(pl.program_id(1) == 0)
    def _():
        carry_ref[...] = jnp.zeros_like(carry_ref)

    z = jnp.dot(x_ref[...].astype(BF16), wf_ref[...], preferred_element_type=F32) + bf_ref[...]
    logf = jnp.minimum(z, 0.0) - jnp.log(1.0 + jnp.exp(-jnp.abs(z)))
    tri = tri_ref[...]
    c = carry_ref[...]
    for part in _split3(logf):
        c = c + jnp.dot(tri, part, preferred_element_type=F32)
    carry_ref[...] = c[c.shape[0] - 1:, :]
    cb = None
    for j, part in enumerate(_split3(c * (-LOG2E))):
        term = jnp.dot(part, sel_ref[j], preferred_element_type=F32)
        cb = term if cb is None else cb + term
    cb_ref[...] = cb.astype(BF16)


def _fox_gate(x, wf, bf, bsz, t_len, tg=512):
    n, d = x.shape
    nt = t_len // tg
    tri = jnp.asarray(np.tril(np.ones((tg, tg), np.float32)), BF16)
    sel = np.zeros((3, LANES, d), np.float32)
    for h in range(FOX_HEADS):
        for j in range(3):
            sel[j, h, (h // 2) * LANES + 3 * (h % 2) + j] = 1.0
    return pl.pallas_call(
        _fox_gate_kernel,
        grid=(bsz, nt),
        in_specs=[pl.BlockSpec((tg, d), lambda b, i: (b * nt + i, 0)),
                  pl.BlockSpec((d, LANES), lambda b, i: (0, 0)),
                  pl.BlockSpec((1, LANES), lambda b, i: (0, 0)),
                  pl.BlockSpec((tg, tg), lambda b, i: (0, 0)),
                  pl.BlockSpec((3, LANES, d), lambda b, i: (0, 0, 0))],
        out_specs=pl.BlockSpec((tg, d), lambda b, i: (b * nt + i, 0)),
        out_shape=jax.ShapeDtypeStruct((n, d), BF16),
        scratch_shapes=[pltpu.VMEM((1, LANES), F32)],
        compiler_params=_params(("arbitrary", "arbitrary"), 32),
        name="fox_gate",
    )(x, wf, bf, tri, jnp.asarray(sel, BF16))


def _fox_attn_kernel(qt_ref, k_ref, cb_ref, vt_ref, ot_ref, s_ref, *, tq):
    qi = pl.program_id(2)
    qt = qt_ref[...].astype(F32)
    feat = lax.broadcasted_iota(jnp.int32, qt.shape, 0)
    rhs = []
    for h in range(2):
        own = (feat >= h * FOX_HEAD_DIM) & (feat < (h + 1) * FOX_HEAD_DIM)
        bias_rows = (feat >= 3 * h) & (feat < 3 * h + 3)
        rhs.append(jnp.concatenate([jnp.where(own, qt, 0.0).astype(BF16),
                                    jnp.where(bias_rows, 1.0, 0.0).astype(BF16)], axis=0))
    key_i = lax.broadcasted_iota(jnp.int32, (tq, tq), 0)
    qry_i = lax.broadcasted_iota(jnp.int32, (tq, tq), 1)

    def scores(j, slot):
        start = pl.multiple_of(j * tq, tq)
        kext = jnp.concatenate([k_ref[pl.ds(start, tq), :], cb_ref[pl.ds(start, tq), :]], axis=1)
        for h in range(2):
            s_ref[slot, h] = jnp.dot(kext, rhs[h], preferred_element_type=F32)

    def block(j, carry, slot, masked):
        start = pl.multiple_of(j * tq, tq)
        if not masked:
            scores(j + 1, 1 - slot)
        out = []
        for h in range(2):
            m, l, acc = carry[h]
            st = s_ref[slot, h]
            if masked:
                st = jnp.where(key_i <= qry_i, st, -jnp.inf)
            m_new = jnp.maximum(m, jnp.max(st, axis=0, keepdims=True))
            p = jnp.exp2(st - m_new)
            a = jnp.exp2(m - m_new)
            vt = vt_ref[h * FOX_HEAD_DIM:(h + 1) * FOX_HEAD_DIM, pl.ds(start, tq)]
            pv = jnp.dot(vt, p.astype(BF16), preferred_element_type=F32)
            out.append((m_new, a * l + jnp.sum(p, axis=0, keepdims=True), acc * a + pv))
        return tuple(out)

    init = (jnp.full((1, tq), -jnp.inf, F32), jnp.zeros((1, tq), F32), jnp.zeros((FOX_HEAD_DIM, tq), F32))
    scores(0, 0)
    carry = lax.fori_loop(
        0, qi // 2, lambda i, c: block(2 * i + 1, block(2 * i, c, 0, False), 1, False), (init, init))
    carry = lax.cond(
        qi % 2 == 0,
        lambda c: block(qi, c, 0, True),
        lambda c: block(qi, block(qi - 1, c, 0, False), 1, True),
        carry)
    for h in range(2):
        _, l, acc = carry[h]
        ot_ref[h * FOX_HEAD_DIM:(h + 1) * FOX_HEAD_DIM, :] = (acc / l).astype(ot_ref.dtype)


def _fox_attn(qt, k, cb, vt, bsz, t_len, tq=256):
    d, n = qt.shape
    n_pairs = d // LANES
    nq = t_len // tq
    return pl.pallas_call(
        functools.partial(_fox_attn_kernel, tq=tq),
        grid=(bsz, n_pairs, nq),
        in_specs=[pl.BlockSpec((LANES, tq), lambda b, p, i: (p, b * nq + i)),
                  pl.BlockSpec((t_len, LANES), lambda b, p, i: (b, p)),
                  pl.BlockSpec((t_len, LANES), lambda b, p, i: (b, p)),
                  pl.BlockSpec((LANES, t_len), lambda b, p, i: (p, b))],
        out_specs=pl.BlockSpec((LANES, tq), lambda b, p, i: (p, b * nq + i)),
        out_shape=jax.ShapeDtypeStruct((d, n), BF16),
        scratch_shapes=[pltpu.VMEM((2, 2, tq, tq), F32)],
        compiler_params=_params(("arbitrary", "arbitrary", "arbitrary"), 32),
        name="fox_attn",
    )(qt, k, cb, vt)


def _fox_layer(x, w_in, b_f, w_out, g, b, alpha, bsz, t_len):
    d = x.shape[1]
    wqt = (w_in[:, :d] * (FOX_HEAD_DIM ** -0.5 * LOG2E)).T.astype(BF16)
    wk = w_in[:, d:2 * d].astype(BF16)
    wvt = w_in[:, 2 * d:3 * d].T.astype(BF16)
    wf = jnp.zeros((d, LANES), F32).at[:, :FOX_HEADS].set(w_in[:, 3 * d:]).astype(BF16)
    bf = jnp.zeros((1, LANES), F32).at[0, :FOX_HEADS].set(b_f)
    qt, k, vt = _fox_proj(x, wqt, wk, wvt)
    cb = _fox_gate(x, wf, bf, bsz, t_len)
    ot = _fox_attn(qt, k, cb, vt, bsz, t_len)
    return _outproj_ln(ot, w_out.astype(BF16), x, g, b, alpha, "fox_out_ln", transposed=True)


def _hgrn_constants():
    c = HGRN_CHUNK
    r = np.arange(c)[:, None]
    j = np.arange(c)[None, :]
    blocks = [j <= r, j > r]
    masks = []
    levels = []
    half = c // 2
    while half >= 1:
        ref = (r // (2 * half)) * (2 * half) + half - 1
        upper = (r % (2 * half)) >= half
        blocks.append(np.where(upper, (j > ref) & (j <= r), (j > r) & (j <= ref)))
        masks.append(((r // (2 * half)) == (j // (2 * half))) & upper & ((j % (2 * half)) < half))
        levels.append(half)
        half //= 2
    masks.append(r == j)
    wall = np.concatenate(blocks, axis=0).astype(np.float32)
    wall2 = np.concatenate([wall, wall], axis=1)
    return wall2, np.stack(masks).astype(np.float32), tuple(levels)


def _hgrn_kernel(q_ref, fl_ref, i_ref, g_ref, lb_ref, ng_ref, wall_ref, mask_ref, o_ref, s_ref, *, levels):
    c = HGRN_CHUNK

    @pl.when(pl.program_id(1) == 0)
    def _():
        s_ref[...] = jnp.zeros_like(s_ref)

    wall = wall_ref[...]
    rowi = lax.broadcasted_iota(jnp.int32, (c, HGRN_DIM), 0)
    for h in range(HGRN_HEADS):
        sl = slice(h * HGRN_DIM, (h + 1) * HGRN_DIM)
        q = q_ref[:, sl].astype(F32)
        i_b = i_ref[:, sl]
        lb = lb_ref[:, sl]
        f = lb + (1.0 - lb) * _sigmoid(fl_ref[:, sl])
        logf = jnp.log(f)
        k = 1.0 - f
        hi = logf.astype(BF16)
        mid = (logf - hi.astype(F32)).astype(BF16)
        x_all = jnp.exp(jnp.dot(wall, jnp.concatenate([hi, mid], axis=0), preferred_element_type=F32))
        x_pre = x_all[0:c]
        x_suf = x_all[c:2 * c]
        st = s_ref[h]
        o = lax.dot_general((q * x_pre).astype(BF16), st.astype(BF16), _NT, preferred_element_type=F32)
        upd = lax.dot_general(i_b, (k * x_suf).astype(BF16), _TN, preferred_element_type=F32)
        s_ref[h] = st * x_pre[c - 1:c, :] + upd
        a = jnp.zeros((c, c), F32)
        for l, half in enumerate(levels):
            upper = (rowi & half) != 0
            z = (jnp.where(upper, q, k) * x_all[(2 + l) * c:(3 + l) * c]).astype(BF16)
            a = a + lax.dot_general(z, z, _NT, preferred_element_type=F32) * mask_ref[l]
        a = a + lax.dot_general(q.astype(BF16), k.astype(BF16), _NT,
                                preferred_element_type=F32) * mask_ref[len(levels)]
        o = o + jnp.dot(a.astype(BF16), i_b, preferred_element_type=F32)
        o = o * lax.rsqrt(jnp.mean(o * o, axis=-1, keepdims=True) + RMS_EPS) * ng_ref[...]
        gate = g_ref[:, sl].astype(F32)
        o_ref[:, sl] = (o * (gate * _sigmoid(gate))).astype(o_ref.dtype)


def _hgrn_core(q, fl, i, g, lb, ng, bsz, t_len):
    n, d = q.shape
    c = HGRN_CHUNK
    nc = t_len // c
    wall2, masks, levels = _hgrn_constants()
    row = lambda b, j: (b * nc + j, 0)
    fixed2 = lambda b, j: (0, 0)
    return pl.pallas_call(
        functools.partial(_hgrn_kernel, levels=levels),
        grid=(bsz, nc),
        in_specs=[pl.BlockSpec((c, d), row), pl.BlockSpec((c, d), row), pl.BlockSpec((c, d), row),
                  pl.BlockSpec((c, d), row), pl.BlockSpec((1, d), fixed2), pl.BlockSpec((1, HGRN_DIM), fixed2),
                  pl.BlockSpec(wall2.shape, fixed2), pl.BlockSpec(masks.shape, lambda b, j: (0, 0, 0))],
        out_specs=pl.BlockSpec((c, d), row),
        out_shape=jax.ShapeDtypeStruct((n, d), BF16),
        scratch_shapes=[pltpu.VMEM((HGRN_HEADS, HGRN_DIM, HGRN_DIM), F32)],
        compiler_params=_params(("arbitrary", "arbitrary"), 32),
        name="hgrn_core",
    )(q, fl, i, g, lb, ng, jnp.asarray(wall2, BF16), jnp.asarray(masks, F32))


def _hgrn_layer(x, w_in, lower_bound, norm_g, w_out, g, b, alpha, bsz, t_len):
    d = x.shape[1]
    wb = w_in.astype(BF16)
    q, fl, i, gate = _proj(x, [wb[:, :d], wb[:, d:2 * d], wb[:, 2 * d:3 * d], wb[:, 3 * d:]],
                           [BF16, F32, BF16, BF16], "hgrn_proj")
    o = _hgrn_core(q, fl, i, gate, lower_bound.reshape(1, d), norm_g.reshape(1, HGRN_DIM), bsz, t_len)
    return _outproj_ln(o, w_out.astype(BF16), x, g, b, alpha, "hgrn_out_ln")


def _conv_kernel(x_ref, win_ref, cw_ref, wout_ref, g_ref, b_ref, y_ref, zbuf, *, alpha, tm):
    d = x_ref.shape[1]

    @pl.when(pl.program_id(1) == 0)
    def _():
        zbuf[0:8, :] = jnp.zeros((8, d), F32)

    x = x_ref[...]
    p = jnp.dot(x.astype(BF16), win_ref[...], preferred_element_type=F32)
    z = p[:, d:2 * d] * p[:, 2 * d:]
    zbuf[8:8 + tm, :] = z
    y = cw_ref[2:3, :] * z + cw_ref[1:2, :] * zbuf[7:7 + tm, :] + cw_ref[0:1, :] * zbuf[6:6 + tm, :]
    zbuf[0:8, :] = z[tm - 8:, :]
    mixed = jnp.dot((p[:, :d] * y).astype(BF16), wout_ref[...], preferred_element_type=F32)
    y_ref[...] = _layer_norm(alpha * x + mixed, g_ref[...], b_ref[...])


def _conv_layer(x, w_in, conv_w, w_out, g, b, alpha, bsz, t_len, tm=512):
    n, d = x.shape
    nt = t_len // tm
    row = lambda bb, i: (bb * nt + i, 0)
    fixed = lambda bb, i: (0, 0)
    return pl.pallas_call(
        functools.partial(_conv_kernel, alpha=alpha, tm=tm),
        grid=(bsz, nt),
        in_specs=[pl.BlockSpec((tm, d), row), pl.BlockSpec((d, 3 * d), fixed), pl.BlockSpec(conv_w.shape, fixed),
                  pl.BlockSpec((d, d), fixed), pl.BlockSpec((1, d), fixed), pl.BlockSpec((1, d), fixed)],
        out_specs=pl.BlockSpec((tm, d), row),
        out_shape=jax.ShapeDtypeStruct((n, d), F32),
        scratch_shapes=[pltpu.VMEM((tm + 8, d), F32)],
        compiler_params=_params(("arbitrary", "arbitrary"), 48),
        name="conv_layer",
    )(x, w_in.astype(BF16), conv_w, w_out.astype(BF16), g, b)


def _router_kernel(x_ref, wh_ref, wl_ref, b_ref, gates_ref):
    x = x_ref[...]
    xh = x.astype(BF16)
    xl = (x - xh.astype(F32)).astype(BF16)
    wh = wh_ref[...]
    logits = (jnp.dot(xh, wh, preferred_element_type=F32) + jnp.dot(xh, wl_ref[...], preferred_element_type=F32)
              + jnp.dot(xl, wh, preferred_element_type=F32)) + b_ref[...]
    lane = lax.broadcasted_iota(jnp.int32, logits.shape, 1)
    lane_f = lane.astype(F32)
    neg = -jnp.inf
    far = float(LANES)
    is_g = (lane >= ROUTER_GROUP_LANE0) & (lane < ROUTER_GROUP_LANE0 + N_GROUPS)
    gl = jnp.where(is_g, logits, neg)
    gmax = jnp.max(gl, axis=-1, keepdims=True)
    g_sel = jnp.min(jnp.where(gl == gmax, lane_f, far), axis=-1, keepdims=True) - float(ROUTER_GROUP_LANE0)
    grp_w = 1.0 / jnp.sum(jnp.where(is_g, jnp.exp(logits - gmax), 0.0), axis=-1, keepdims=True)
    in_g = (lane < N_EXPERTS) & ((lane // EXPERTS_PER_GROUP).astype(F32) == g_sel)
    el = jnp.where(in_g, logits, neg)
    t1 = jnp.max(el, axis=-1, keepdims=True)
    i1 = jnp.min(jnp.where(el == t1, lane_f, far), axis=-1, keepdims=True)
    el2 = jnp.where(lane_f == i1, neg, el)
    t2 = jnp.max(el2, axis=-1, keepdims=True)
    i2 = jnp.min(jnp.where(el2 == t2, lane_f, far), axis=-1, keepdims=True)
    e2 = jnp.exp(t2 - t1)
    w1 = grp_w / (1.0 + e2)
    gates_ref[...] = jnp.where(lane_f == i1, w1, 0.0) + jnp.where(lane_f == i2, w1 * e2, 0.0)


def _router(x, w_group, b_group, w_expert, b_expert, tm=512):
    n, d = x.shape
    w = jnp.zeros((d, LANES), F32).at[:, :N_EXPERTS].set(w_expert)
    w = w.at[:, ROUTER_GROUP_LANE0:ROUTER_GROUP_LANE0 + N_GROUPS].set(w_group)
    bias = jnp.zeros((1, LANES), F32).at[0, :N_EXPERTS].set(b_expert)
    bias = bias.at[0, ROUTER_GROUP_LANE0:ROUTER_GROUP_LANE0 + N_GROUPS].set(b_group)
    wh = w.astype(BF16)
    wl = (w - wh.astype(F32)).astype(BF16)
    row = lambda i: (i, 0)
    fixed = lambda i: (0, 0)
    return pl.pallas_call(
        _router_kernel,
        grid=(n // tm,),
        in_specs=[pl.BlockSpec((tm, d), row), pl.BlockSpec((d, LANES), fixed), pl.BlockSpec((d, LANES), fixed),
                  pl.BlockSpec((1, LANES), fixed)],
        out_specs=pl.BlockSpec((tm, LANES), row),
        out_shape=jax.ShapeDtypeStruct((n, LANES), F32),
        compiler_params=_params(("arbitrary",), 32),
        name="moe_router",
    )(x, wh, wl, bias)


def _moe_kernel(x_ref, gates_ref, wg_ref, wu_ref, wd_ref, g_ref, b_ref, y_ref, xb_ref, acc_ref, *, alpha):
    e = pl.program_id(1)

    @pl.when(e == 0)
    def _():
        xb_ref[...] = x_ref[...].astype(BF16)
        acc_ref[...] = jnp.zeros_like(acc_ref)

    gates = gates_ref[...]
    lane = lax.broadcasted_iota(jnp.int32, gates.shape, 1)
    gate = jnp.sum(jnp.where(lane == e, gates, 0.0), axis=-1, keepdims=True)
    xb = xb_ref[...]
    hg = jnp.dot(xb, wg_ref[0, 0].astype(BF16), preferred_element_type=F32)
    hu = jnp.dot(xb, wu_ref[0, 0].astype(BF16), preferred_element_type=F32)
    h = hg * _sigmoid(hg) * hu * gate
    acc_ref[...] += jnp.dot(h.astype(BF16), wd_ref[0, 0].astype(BF16), preferred_element_type=F32)

    @pl.when(e == pl.num_programs(1) - 1)
    def _():
        y_ref[...] = _layer_norm(alpha * x_ref[...] + acc_ref[...], g_ref[...], b_ref[...])


def _moe_ln(x, gates, w_gate, w_up, w_down, layer, g, b, alpha, tm=1024):
    n, d = x.shape
    f = w_gate.shape[-1]
    row = lambda i, e: (i, 0)
    fixed = lambda i, e: (0, 0)
    return pl.pallas_call(
        functools.partial(_moe_kernel, alpha=alpha),
        grid=(n // tm, N_EXPERTS),
        in_specs=[pl.BlockSpec((tm, d), row), pl.BlockSpec((tm, LANES), row),
                  pl.BlockSpec((1, 1, d, f), lambda i, e: (layer, e, 0, 0)),
                  pl.BlockSpec((1, 1, d, f), lambda i, e: (layer, e, 0, 0)),
                  pl.BlockSpec((1, 1, f, d), lambda i, e: (layer, e, 0, 0)),
                  pl.BlockSpec((1, d), fixed), pl.BlockSpec((1, d), fixed)],
        out_specs=pl.BlockSpec((tm, d), row),
        out_shape=jax.ShapeDtypeStruct((n, d), F32),
        scratch_shapes=[pltpu.VMEM((tm, d), BF16), pltpu.VMEM((tm, d), F32)],
        compiler_params=_params(("arbitrary", "arbitrary"), 48),
        name="moe_experts_ln",
    )(x, gates, w_gate, w_up, w_down, g, b)


def kernel(x, ln_mix_g, ln_mix_b, ln_ffn_g, ln_ffn_b, fox_w_in, fox_b_f, fox_w_out, hgrn_w_in, hgrn_lb_logits, hgrn_norm_g, hgrn_w_out, conv_w_in, conv_w, conv_w_out, moe_w_group, moe_b_group, moe_w_expert, moe_b_expert, moe_w_gate, moe_w_up, moe_w_down):
    bsz, t_len, d = x.shape
    depth = ln_mix_g.shape[0]
    alpha = float((2 * depth) ** 0.25)
    assert d == FOX_HEADS * FOX_HEAD_DIM == HGRN_HEADS * HGRN_DIM
    assert t_len % 512 == 0 and (bsz * t_len) % 1024 == 0

    lb_prob = jax.nn.softmax(hgrn_lb_logits.astype(F32), axis=0)
    lower_bounds = jnp.cumsum(lb_prob, axis=0) - lb_prob[0]

    h = x.reshape(bsz * t_len, d)
    for layer in range(depth):
        kind, j = layer % 3, layer // 3
        g_mix, b_mix = ln_mix_g[layer].reshape(1, d), ln_mix_b[layer].reshape(1, d)
        if kind == 0:
            h = _fox_layer(h, fox_w_in[j], fox_b_f[j], fox_w_out[j], g_mix, b_mix, alpha, bsz, t_len)
        elif kind == 1:
            h = _hgrn_layer(h, hgrn_w_in[j], lower_bounds[layer], hgrn_norm_g[j], hgrn_w_out[j],
                            g_mix, b_mix, alpha, bsz, t_len)
        else:
            h = _conv_layer(h, conv_w_in[j], conv_w[j], conv_w_out[j], g_mix, b_mix, alpha, bsz, t_len)
        gates = _router(h, moe_w_group[layer], moe_b_group[layer], moe_w_expert[layer], moe_b_expert[layer])
        h = _moe_ln(h, gates, moe_w_gate, moe_w_up, moe_w_down, layer,
                    ln_ffn_g[layer].reshape(1, d), ln_ffn_b[layer].reshape(1, d), alpha)
    return h.reshape(bsz, t_len, d)
```

```python
import functools

import numpy as np
import jax
import jax.numpy as jnp
from jax import lax
from jax.experimental import pallas as pl
from jax.experimental.pallas import tpu as pltpu

F32 = jnp.float32
BF16 = jnp.bfloat16

FOX_HEADS = 16
FOX_HEAD_DIM = 64
HGRN_HEADS = 8
HGRN_DIM = 128
HGRN_CHUNK = 128
N_GROUPS = 4
EXPERTS_PER_GROUP = 8
N_EXPERTS = N_GROUPS * EXPERTS_PER_GROUP
LN_EPS = 1e-5
RMS_EPS = 1e-6
LOG2E = 1.4426950408889634
LANES = 128
ROUTER_GROUP_LANE0 = N_EXPERTS
MOE_PAIRS = EXPERTS_PER_GROUP * (EXPERTS_PER_GROUP - 1) // 2
MOE_BUCKETS = N_GROUPS * MOE_PAIRS
MOE_ROW_TILE = 128
MOE_TOK_TILE = 256
META_W = LANES

_NT = (((1,), (1,)), ((), ()))
_TN = (((0,), (0,)), ((), ()))


def _params(semantics, vmem_mb):
    return pltpu.CompilerParams(dimension_semantics=semantics,
                                vmem_limit_bytes=vmem_mb * 1024 * 1024)


def _layer_norm(v, g, b):
    mu = jnp.mean(v, axis=-1, keepdims=True)
    d = v - mu
    var = jnp.mean(d * d, axis=-1, keepdims=True)
    return d * lax.rsqrt(var + LN_EPS) * g + b


def _sigmoid(v):
    return 1.0 / (1.0 + jnp.exp(-v))


def _proj_kernel(x_ref, *refs, n_out):
    xb = x_ref[...].astype(BF16)
    for w_ref, o_ref in zip(refs[:n_out], refs[n_out:]):
        o_ref[...] = jnp.dot(xb, w_ref[...], preferred_element_type=F32).astype(o_ref.dtype)


def _proj(x, ws, dtypes, name, tm=512):
    n, k = x.shape
    return pl.pallas_call(
        functools.partial(_proj_kernel, n_out=len(ws)),
        grid=(n // tm,),
        in_specs=[pl.BlockSpec((tm, k), lambda i: (i, 0))]
        + [pl.BlockSpec(w.shape, lambda i: (0, 0)) for w in ws],
        out_specs=[pl.BlockSpec((tm, w.shape[1]), lambda i: (i, 0)) for w in ws],
        out_shape=[jax.ShapeDtypeStruct((n, w.shape[1]), dt) for w, dt in zip(ws, dtypes)],
        compiler_params=_params(("arbitrary",), 48),
        name=name,
    )(x, *ws)


def _outproj_ln_kernel(o_ref, w_ref, x_ref, g_ref, b_ref, y_ref, *, alpha, transposed):
    dims = _TN if transposed else (((1,), (0,)), ((), ()))
    mixed = lax.dot_general(o_ref[...], w_ref[...], dims, preferred_element_type=F32)
    y_ref[...] = _layer_norm(alpha * x_ref[...] + mixed, g_ref[...], b_ref[...])


def _outproj_ln(o, w, x, g, b, alpha, name, tm=512, transposed=False):
    n, d = x.shape
    k = w.shape[0]
    row = lambda i: (i, 0)
    fixed = lambda i: (0, 0)
    o_spec = pl.BlockSpec((k, tm), lambda i: (0, i)) if transposed else pl.BlockSpec((tm, k), row)
    return pl.pallas_call(
        functools.partial(_outproj_ln_kernel, alpha=alpha, transposed=transposed),
        grid=(n // tm,),
        in_specs=[o_spec, pl.BlockSpec((k, d), fixed),
                  pl.BlockSpec((tm, d), row), pl.BlockSpec((1, d), fixed), pl.BlockSpec((1, d), fixed)],
        out_specs=pl.BlockSpec((tm, d), row),
        out_shape=jax.ShapeDtypeStruct((n, d), F32),
        compiler_params=_params(("arbitrary",), 32),
        name=name,
    )(o, w, x, g, b)


def _split3(v):
    hi = v.astype(BF16)
    r1 = v - hi.astype(F32)
    mid = r1.astype(BF16)
    return hi, mid, (r1 - mid.astype(F32)).astype(BF16)


def _fox_proj_kernel(x_ref, wqt_ref, wk_ref, wvt_ref, qt_ref, k_ref, vt_ref):
    xb = x_ref[...].astype(BF16)
    qt_ref[...] = lax.dot_general(wqt_ref[...], xb, _NT, preferred_element_type=F32).astype(BF16)
    k_ref[...] = jnp.dot(xb, wk_ref[...], preferred_element_type=F32).astype(BF16)
    vt_ref[...] = lax.dot_general(wvt_ref[...], xb, _NT, preferred_element_type=F32).astype(BF16)


def _fox_proj(x, wqt, wk, wvt, tm=512):
    n, d = x.shape
    fixed = lambda i: (0, 0)
    return pl.pallas_call(
        _fox_proj_kernel,
        grid=(n // tm,),
        in_specs=[pl.BlockSpec((tm, d), lambda i: (i, 0)), pl.BlockSpec((d, d), fixed),
                  pl.BlockSpec((d, d), fixed), pl.BlockSpec((d, d), fixed)],
        out_specs=[pl.BlockSpec((d, tm), lambda i: (0, i)), pl.BlockSpec((tm, d), lambda i: (i, 0)),
                   pl.BlockSpec((d, tm), lambda i: (0, i))],
        out_shape=[jax.ShapeDtypeStruct((d, n), BF16), jax.ShapeDtypeStruct((n, d), BF16),
                   jax.ShapeDtypeStruct((d, n), BF16)],
        compiler_params=_params(("arbitrary",), 48),
        name="fox_proj",
    )(x, wqt, wk, wvt)


def _fox_gate_kernel(x_ref, wf_ref, bf_ref, tri_ref, sel_ref, cb_ref, carry_ref):
    @pl.when(pl.program_id(1) == 0)
    def _():
        carry_ref[...] = jnp.zeros_like(carry_ref)

    z = jnp.dot(x_ref[...].astype(BF16), wf_ref[...], preferred_element_type=F32) + bf_ref[...]
    logf = jnp.minimum(z, 0.0) - jnp.log(1.0 + jnp.exp(-jnp.abs(z)))
    tri = tri_ref[...]
    c = carry_ref[...]
    for part in _split3(logf):
        c = c + jnp.dot(tri, part, preferred_element_type=F32)
    carry_ref[...] = c[c.shape[0] - 1:, :]
    cb = None
    for j, part in enumerate(_split3(c * (-LOG2E))):
        term = jnp.dot(part, sel_ref[j], preferred_element_type=F32)
        cb = term if cb is None else cb + term
    cb_ref[...] = cb.astype(BF16)


def _fox_gate(x, wf, bf, bsz, t_len, tg=512):
    n, d = x.shape
    nt = t_len // tg
    tri = jnp.asarray(np.tril(np.ones((tg, tg), np.float32)), BF16)
    sel = np.zeros((3, LANES, d), np.float32)
    for h in range(FOX_HEADS):
        for j in range(3):
            sel[j, h, (h // 2) * LANES + 3 * (h % 2) + j] = 1.0
    return pl.pallas_call(
        _fox_gate_kernel,
        grid=(bsz, nt),
        in_specs=[pl.BlockSpec((tg, d), lambda b, i: (b * nt + i, 0)),
                  pl.BlockSpec((d, LANES), lambda b, i: (0, 0)),
                  pl.BlockSpec((1, LANES), lambda b, i: (0, 0)),
                  pl.BlockSpec((tg, tg), lambda b, i: (0, 0)),
                  pl.BlockSpec((3, LANES, d), lambda b, i: (0, 0, 0))],
        out_specs=pl.BlockSpec((tg, d), lambda b, i: (b * nt + i, 0)),
        out_shape=jax.ShapeDtypeStruct((n, d), BF16),
        scratch_shapes=[pltpu.VMEM((1, LANES), F32)],
        compiler_params=_params(("arbitrary", "arbitrary"), 32),
        name="fox_gate",
    )(x, wf, bf, tri, jnp.asarray(sel, BF16))


def _fox_attn_kernel(qt_ref, k_ref, cb_ref, vt_ref, ot_ref, s_ref, *, tq):
    qi = pl.program_id(2)
    qt = qt_ref[...].astype(F32)
    feat = lax.broadcasted_iota(jnp.int32, qt.shape, 0)
    rhs = []
    for h in range(2):
        own = (feat >= h * FOX_HEAD_DIM) & (feat < (h + 1) * FOX_HEAD_DIM)
        bias_rows = (feat >= 3 * h) & (feat < 3 * h + 3)
        rhs.append(jnp.concatenate([jnp.where(own, qt, 0.0).astype(BF16),
                                    jnp.where(bias_rows, 1.0, 0.0).astype(BF16)], axis=0))
    key_i = lax.broadcasted_iota(jnp.int32, (tq, tq), 0)
    qry_i = lax.broadcasted_iota(jnp.int32, (tq, tq), 1)

    def scores(j, slot):
        start = pl.multiple_of(j * tq, tq)
        kext = jnp.concatenate([k_ref[pl.ds(start, tq), :], cb_ref[pl.ds(start, tq), :]], axis=1)
        for h in range(2):
            s_ref[slot, h] = jnp.dot(kext, rhs[h], preferred_element_type=F32)

    def block(j, carry, slot, masked):
        start = pl.multiple_of(j * tq, tq)
        if not masked:
            scores(j + 1, 1 - slot)
        out = []
        for h in range(2):
            m, l, acc = carry[h]
            st = s_ref[slot, h]
            if masked:
                st = jnp.where(key_i <= qry_i, st, -jnp.inf)
            m_new = jnp.maximum(m, jnp.max(st, axis=0, keepdims=True))
            p = jnp.exp2(st - m_new)
            a = jnp.exp2(m - m_new)
            vt = vt_ref[h * FOX_HEAD_DIM:(h + 1) * FOX_HEAD_DIM, pl.ds(start, tq)]
            pv = jnp.dot(vt, p.astype(BF16), preferred_element_type=F32)
            out.append((m_new, a * l + jnp.sum(p, axis=0, keepdims=True), acc * a + pv))
        return tuple(out)

    init = (jnp.full((1, tq), -jnp.inf, F32), jnp.zeros((1, tq), F32), jnp.zeros((FOX_HEAD_DIM, tq), F32))
    scores(0, 0)
    carry = lax.fori_loop(
        0, qi // 2, lambda i, c: block(2 * i + 1, block(2 * i, c, 0, False), 1, False), (init, init))
    carry = lax.cond(
        qi % 2 == 0,
        lambda c: block(qi, c, 0, True),
        lambda c: block(qi, block(qi - 1, c, 0, False), 1, True),
        carry)
    for h in range(2):
        _, l, acc = carry[h]
        ot_ref[h * FOX_HEAD_DIM:(h + 1) * FOX_HEAD_DIM, :] = (acc / l).astype(ot_ref.dtype)


def _fox_attn(qt, k, cb, vt, bsz, t_len, tq=256):
    d, n = qt.shape
    n_pairs = d // LANES
    nq = t_len // tq
    return pl.pallas_call(
        functools.partial(_fox_attn_kernel, tq=tq),
        grid=(bsz, n_pairs, nq),
        in_specs=[pl.BlockSpec((LANES, tq), lambda b, p, i: (p, b * nq + i)),
                  pl.BlockSpec((t_len, LANES), lambda b, p, i: (b, p)),
                  pl.BlockSpec((t_len, LANES), lambda b, p, i: (b, p)),
                  pl.BlockSpec((LANES, t_len), lambda b, p, i: (p, b))],
        out_specs=pl.BlockSpec((LANES, tq), lambda b, p, i: (p, b * nq + i)),
        out_shape=jax.ShapeDtypeStruct((d, n), BF16),
        scratch_shapes=[pltpu.VMEM((2, 2, tq, tq), F32)],
        compiler_params=_params(("arbitrary", "arbitrary", "arbitrary"), 32),
        name="fox_attn",
    )(qt, k, cb, vt)


def _fox_layer(x, w_in, b_f, w_out, g, b, alpha, bsz, t_len):
    d = x.shape[1]
    wqt = (w_in[:, :d] * (FOX_HEAD_DIM ** -0.5 * LOG2E)).T.astype(BF16)
    wk = w_in[:, d:2 * d].astype(BF16)
    wvt = w_in[:, 2 * d:3 * d].T.astype(BF16)
    wf = jnp.zeros((d, LANES), F32).at[:, :FOX_HEADS].set(w_in[:, 3 * d:]).astype(BF16)
    bf = jnp.zeros((1, LANES), F32).at[0, :FOX_HEADS].set(b_f)
    qt, k, vt = _fox_proj(x, wqt, wk, wvt)
    cb = _fox_gate(x, wf, bf, bsz, t_len)
    ot = _fox_attn(qt, k, cb, vt, bsz, t_len)
    return _outproj_ln(ot, w_out.astype(BF16), x, g, b, alpha, "fox_out_ln", transposed=True)


def _hgrn_constants():
    c = HGRN_CHUNK
    r = np.arange(c)[:, None]
    j = np.arange(c)[None, :]
    blocks = [j <= r, j > r]
    masks = []
    levels = []
    half = c // 2
    while half >= 1:
        ref = (r // (2 * half)) * (2 * half) + half - 1
        upper = (r % (2 * half)) >= half
        blocks.append(np.where(upper, (j > ref) & (j <= r), (j > r) & (j <= ref)))
        masks.append(((r // (2 * half)) == (j // (2 * half))) & upper & ((j % (2 * half)) < half))
        levels.append(half)
        half //= 2
    masks.append(r == j)
    wall = np.concatenate(blocks, axis=0).astype(np.float32)
    wall2 = np.concatenate([wall, wall], axis=1)
    return wall2, np.stack(masks).astype(np.float32), tuple(levels)


def _hgrn_kernel(q_ref, fl_ref, i_ref, g_ref, lb_ref, ng_ref, wall_ref, mask_ref, o_ref, s_ref, *, levels):
    c = HGRN_CHUNK

    @pl.when(pl.program_id(1) == 0)
    def _():
        s_ref[...] = jnp.zeros_like(s_ref)

    wall = wall_ref[...]
    rowi = lax.broadcasted_iota(jnp.int32, (c, HGRN_DIM), 0)
    for h in range(HGRN_HEADS):
        sl = slice(h * HGRN_DIM, (h + 1) * HGRN_DIM)
        q = q_ref[:, sl].astype(F32)
        i_b = i_ref[:, sl]
        lb = lb_ref[:, sl]
        f = lb + (1.0 - lb) * _sigmoid(fl_ref[:, sl])
        logf = jnp.log(f)
        k = 1.0 - f
        hi = logf.astype(BF16)
        mid = (logf - hi.astype(F32)).astype(BF16)
        x_all = jnp.exp(jnp.dot(wall, jnp.concatenate([hi, mid], axis=0), preferred_element_type=F32))
        x_pre = x_all[0:c]
        x_suf = x_all[c:2 * c]
        st = s_ref[h]
        o = lax.dot_general((q * x_pre).astype(BF16), st.astype(BF16), _NT, preferred_element_type=F32)
        upd = lax.dot_general(i_b, (k * x_suf).astype(BF16), _TN, preferred_element_type=F32)
        s_ref[h] = st * x_pre[c - 1:c, :] + upd
        a = jnp.zeros((c, c), F32)
        for l, half in enumerate(levels):
            upper = (rowi & half) != 0
            z = (jnp.where(upper, q, k) * x_all[(2 + l) * c:(3 + l) * c]).astype(BF16)
            a = a + lax.dot_general(z, z, _NT, preferred_element_type=F32) * mask_ref[l]
        a = a + lax.dot_general(q.astype(BF16), k.astype(BF16), _NT,
                                preferred_element_type=F32) * mask_ref[len(levels)]
        o = o + jnp.dot(a.astype(BF16), i_b, preferred_element_type=F32)
        o = o * lax.rsqrt(jnp.mean(o * o, axis=-1, keepdims=True) + RMS_EPS) * ng_ref[...]
        gate = g_ref[:, sl].astype(F32)
        o_ref[:, sl] = (o * (gate * _sigmoid(gate))).astype(o_ref.dtype)


def _hgrn_core(q, fl, i, g, lb, ng, bsz, t_len):
    n, d = q.shape
    c = HGRN_CHUNK
    nc = t_len // c
    wall2, masks, levels = _hgrn_constants()
    row = lambda b, j: (b * nc + j, 0)
    fixed2 = lambda b, j: (0, 0)
    return pl.pallas_call(
        functools.partial(_hgrn_kernel, levels=levels),
        grid=(bsz, nc),
        in_specs=[pl.BlockSpec((c, d), row), pl.BlockSpec((c, d), row), pl.BlockSpec((c, d), row),
                  pl.BlockSpec((c, d), row), pl.BlockSpec((1, d), fixed2), pl.BlockSpec((1, HGRN_DIM), fixed2),
                  pl.BlockSpec(wall2.shape, fixed2), pl.BlockSpec(masks.shape, lambda b, j: (0, 0, 0))],
        out_specs=pl.BlockSpec((c, d), row),
        out_shape=jax.ShapeDtypeStruct((n, d), BF16),
        scratch_shapes=[pltpu.VMEM((HGRN_HEADS, HGRN_DIM, HGRN_DIM), F32)],
        compiler_params=_params(("arbitrary", "arbitrary"), 32),
        name="hgrn_core",
    )(q, fl, i, g, lb, ng, jnp.asarray(wall2, BF16), jnp.asarray(masks, F32))


def _hgrn_layer(x, w_in, lower_bound, norm_g, w_out, g, b, alpha, bsz, t_len):
    d = x.shape[1]
    wb = w_in.astype(BF16)
    q, fl, i, gate = _proj(x, [wb[:, :d], wb[:, d:2 * d], wb[:, 2 * d:3 * d], wb[:, 3 * d:]],
                           [BF16, F32, BF16, BF16], "hgrn_proj")
    o = _hgrn_core(q, fl, i, gate, lower_bound.reshape(1, d), norm_g.reshape(1, HGRN_DIM), bsz, t_len)
    return _outproj_ln(o, w_out.astype(BF16), x, g, b, alpha, "hgrn_out_ln")


def _conv_kernel(x_ref, win_ref, cw_ref, wout_ref, g_ref, b_ref, y_ref, zbuf, *, alpha, tm):
    d = x_ref.shape[1]

    @pl.when(pl.program_id(1) == 0)
    def _():
        zbuf[0:8, :] = jnp.zeros((8, d), F32)

    x = x_ref[...]
    p = jnp.dot(x.astype(BF16), win_ref[...], preferred_element_type=F32)
    z = p[:, d:2 * d] * p[:, 2 * d:]
    zbuf[8:8 + tm, :] = z
    y = cw_ref[2:3, :] * z + cw_ref[1:2, :] * zbuf[7:7 + tm, :] + cw_ref[0:1, :] * zbuf[6:6 + tm, :]
    zbuf[0:8, :] = z[tm - 8:, :]
    mixed = jnp.dot((p[:, :d] * y).astype(BF16), wout_ref[...], preferred_element_type=F32)
    y_ref[...] = _layer_norm(alpha * x + mixed, g_ref[...], b_ref[...])


def _conv_layer(x, w_in, conv_w, w_out, g, b, alpha, bsz, t_len, tm=512):
    n, d = x.shape
    nt = t_len // tm
    row = lambda bb, i: (bb * nt + i, 0)
    fixed = lambda bb, i: (0, 0)
    return pl.pallas_call(
        functools.partial(_conv_kernel, alpha=alpha, tm=tm),
        grid=(bsz, nt),
        in_specs=[pl.BlockSpec((tm, d), row), pl.BlockSpec((d, 3 * d), fixed), pl.BlockSpec(conv_w.shape, fixed),
                  pl.BlockSpec((d, d), fixed), pl.BlockSpec((1, d), fixed), pl.BlockSpec((1, d), fixed)],
        out_specs=pl.BlockSpec((tm, d), row),
        out_shape=jax.ShapeDtypeStruct((n, d), F32),
        scratch_shapes=[pltpu.VMEM((tm + 8, d), F32)],
        compiler_params=_params(("arbitrary", "arbitrary"), 48),
        name="conv_layer",
    )(x, w_in.astype(BF16), conv_w, w_out.astype(BF16), g, b)


def _router_kernel(x_ref, wh_ref, wl_ref, b_ref, xw_ref, oh_ref, cnt_ref):
    d = x_ref.shape[1]
    x = x_ref[...]
    xh = x.astype(BF16)
    xl = (x - xh.astype(F32)).astype(BF16)
    wh = wh_ref[...]
    logits = (jnp.dot(xh, wh, preferred_element_type=F32) + jnp.dot(xh, wl_ref[...], preferred_element_type=F32)
              + jnp.dot(xl, wh, preferred_element_type=F32)) + b_ref[...]
    lane = lax.broadcasted_iota(jnp.int32, logits.shape, 1)
    lane_f = lane.astype(F32)
    neg = -jnp.inf
    far = float(LANES)
    is_g = (lane >= ROUTER_GROUP_LANE0) & (lane < ROUTER_GROUP_LANE0 + N_GROUPS)
    gl = jnp.where(is_g, logits, neg)
    gmax = jnp.max(gl, axis=-1, keepdims=True)
    g_sel = jnp.min(jnp.where(gl == gmax, lane_f, far), axis=-1, keepdims=True) - float(ROUTER_GROUP_LANE0)
    grp_w = 1.0 / jnp.sum(jnp.where(is_g, jnp.exp(logits - gmax), 0.0), axis=-1, keepdims=True)
    in_g = (lane < N_EXPERTS) & ((lane // EXPERTS_PER_GROUP).astype(F32) == g_sel)
    el = jnp.where(in_g, logits, neg)
    t1 = jnp.max(el, axis=-1, keepdims=True)
    i1 = jnp.min(jnp.where(el == t1, lane_f, far), axis=-1, keepdims=True)
    el2 = jnp.where(lane_f == i1, neg, el)
    t2 = jnp.max(el2, axis=-1, keepdims=True)
    i2 = jnp.min(jnp.where(el2 == t2, lane_f, far), axis=-1, keepdims=True)
    e2 = jnp.exp(t2 - t1)
    w1 = grp_w / (1.0 + e2)
    w2 = w1 * e2
    first_lo = i1 < i2
    lo = jnp.where(first_lo, i1, i2) - g_sel * EXPERTS_PER_GROUP
    hi = jnp.where(first_lo, i2, i1) - g_sel * EXPERTS_PER_GROUP
    pair = lo * (EXPERTS_PER_GROUP - 1.0) - lo * (lo - 1.0) * 0.5 + (hi - lo - 1.0)
    onehot = jnp.where(lane_f == g_sel * MOE_PAIRS + pair, 1.0, 0.0)
    xw_ref[:, :d] = x
    xw_ref[:, d:] = (jnp.where(lane == 0, jnp.where(first_lo, w1, w2), 0.0)
                     + jnp.where(lane == 1, jnp.where(first_lo, w2, w1), 0.0))
    oh_ref[...] = onehot.astype(BF16)

    @pl.when(pl.program_id(0) == 0)
    def _():
        cnt_ref[...] = jnp.zeros_like(cnt_ref)

    cnt_ref[...] += jnp.sum(onehot, axis=0, keepdims=True)


def _router(x, w_group, b_group, w_expert, b_expert, tm=MOE_TOK_TILE):
    n, d = x.shape
    w = jnp.zeros((d, LANES), F32).at[:, :N_EXPERTS].set(w_expert)
    w = w.at[:, ROUTER_GROUP_LANE0:ROUTER_GROUP_LANE0 + N_GROUPS].set(w_group)
    bias = jnp.zeros((1, LANES), F32).at[0, :N_EXPERTS].set(b_expert)
    bias = bias.at[0, ROUTER_GROUP_LANE0:ROUTER_GROUP_LANE0 + N_GROUPS].set(b_group)
    wh = w.astype(BF16)
    wl = (w - wh.astype(F32)).astype(BF16)
    row = lambda i: (i, 0)
    fixed = lambda i: (0, 0)
    return pl.pallas_call(
        _router_kernel,
        grid=(n // tm,),
        in_specs=[pl.BlockSpec((tm, d), row), pl.BlockSpec((d, LANES), fixed), pl.BlockSpec((d, LANES), fixed),
                  pl.BlockSpec((1, LANES), fixed)],
        out_specs=[pl.BlockSpec((tm, d + META_W), row), pl.BlockSpec((tm, LANES), row),
                   pl.BlockSpec((8, LANES), fixed)],
        out_shape=[jax.ShapeDtypeStruct((n, d + META_W), F32), jax.ShapeDtypeStruct((n, LANES), BF16),
                   jax.ShapeDtypeStruct((8, LANES), F32)],
        compiler_params=_params(("arbitrary",), 32),
        name="moe_router",
    )(x, wh, wl, bias)


def _moe_pos_kernel(oh_ref, base_ref, lt_ref, pos_ref, carry_ref):
    @pl.when(pl.program_id(0) == 0)
    def _():
        carry_ref[...] = jnp.zeros_like(carry_ref)

    oh = oh_ref[...]
    rank = jnp.dot(lt_ref[...], oh, preferred_element_type=F32) + carry_ref[...] + base_ref[...]
    ohf = oh.astype(F32)
    val = ohf * rank
    carry_ref[...] += jnp.sum(ohf, axis=0, keepdims=True)
    hi = jnp.floor(val * (1.0 / 256.0))
    lo = val - 256.0 * hi
    ones = jnp.ones((8, LANES), BF16)
    pos = (256.0 * lax.dot_general(ones, hi.astype(BF16), _NT, preferred_element_type=F32)
           + lax.dot_general(ones, lo.astype(BF16), _NT, preferred_element_type=F32))
    pos_ref[0] = pos.astype(jnp.int32)


def _moe_pos(onehot, base, tm=MOE_TOK_TILE):
    n = onehot.shape[0]
    lt = jnp.asarray(np.tril(np.ones((tm, tm), np.float32), -1), BF16)
    return pl.pallas_call(
        _moe_pos_kernel,
        grid=(n // tm,),
        in_specs=[pl.BlockSpec((tm, LANES), lambda i: (i, 0)), pl.BlockSpec((1, LANES), lambda i: (0, 0)),
                  pl.BlockSpec((tm, tm), lambda i: (0, 0))],
        out_specs=pl.BlockSpec((1, 8, tm), lambda i: (i, 0, 0)),
        out_shape=jax.ShapeDtypeStruct((n // tm, 8, tm), jnp.int32),
        scratch_shapes=[pltpu.VMEM((1, LANES), F32)],
        compiler_params=_params(("arbitrary",), 32),
        name="moe_pos",
    )(onehot, base, lt)


def _row_copy(src, src_row, dst, dst_row, sem):
    return pltpu.make_async_copy(src.at[pl.ds(src_row, 1)], dst.at[pl.ds(dst_row, 1)], sem)


def _moe_scatter_kernel(pos_ref, xw_ref, xs_in_ref, xs_ref, sem, *, tm):
    del xs_in_ref
    t0 = pl.program_id(0) * tm

    def start(r, c):
        _row_copy(xw_ref, t0 + r, xs_ref, pos_ref[0, 0, r], sem).start()
        return c

    def wait(r, c):
        _row_copy(xw_ref, t0 + r, xs_ref, pos_ref[0, 0, r], sem).wait()
        return c

    lax.fori_loop(0, tm, start, 0)
    lax.fori_loop(0, tm, wait, 0)


def _moe_scatter(pos, xw, xs_buf, tm=MOE_TOK_TILE):
    n = xw.shape[0]
    return pl.pallas_call(
        functools.partial(_moe_scatter_kernel, tm=tm),
        grid=(n // tm,),
        in_specs=[pl.BlockSpec((1, 8, tm), lambda i: (i, 0, 0), memory_space=pltpu.SMEM),
                  pl.BlockSpec(memory_space=pl.ANY), pl.BlockSpec(memory_space=pl.ANY)],
        out_specs=pl.BlockSpec(memory_space=pl.ANY),
        out_shape=jax.ShapeDtypeStruct(xs_buf.shape, xs_buf.dtype),
        scratch_shapes=[pltpu.SemaphoreType.DMA(())],
        input_output_aliases={2: 0},
        compiler_params=_params(("arbitrary",), 32),
        name="moe_scatter",
    )(pos, xw, xs_buf)


def _moe_experts_kernel(e1_ref, e2_ref, nused_ref, xs_ref, wg1_ref, wu1_ref, wd1_ref, wg2_ref, wu2_ref, wd2_ref,
                        ys_in_ref, ys_ref, wgu_ref, wd_ref):
    del ys_in_ref
    i = pl.program_id(0)
    d = ys_ref.shape[1]

    @pl.when(i < nused_ref[0])
    def _():
        prev = jnp.maximum(i - 1, 0)
        srcs = ((e1_ref, wg1_ref, wu1_ref, wd1_ref), (e2_ref, wg2_ref, wu2_ref, wd2_ref))
        for k, (e_ref, wg_ref, wu_ref, wdn_ref) in enumerate(srcs):
            @pl.when((i == 0) | (e_ref[i] != e_ref[prev]))
            def _():
                wgu_ref[2 * k] = wg_ref[0, 0].astype(BF16)
                wgu_ref[2 * k + 1] = wu_ref[0, 0].astype(BF16)
                wd_ref[k] = wdn_ref[0, 0].astype(BF16)

        x = xs_ref[:, :d].astype(BF16)
        meta = xs_ref[:, d:]
        y = None
        for k in range(2):
            hg = jnp.dot(x, wgu_ref[2 * k], preferred_element_type=F32)
            hu = jnp.dot(x, wgu_ref[2 * k + 1], preferred_element_type=F32)
            h = hg * _sigmoid(hg) * hu * meta[:, k:k + 1]
            yk = jnp.dot(h.astype(BF16), wd_ref[k], preferred_element_type=F32)
            y = yk if y is None else y + yk
        ys_ref[...] = y


def _moe_experts(xs_buf, ys_buf, tile_e1, tile_e2, n_used, w_gate, w_up, w_down, layer, tr=MOE_ROW_TILE):
    r_max, dw = xs_buf.shape
    d = dw - META_W
    f = w_gate.shape[-1]
    used = lambda i, e1, e2, nu: (jnp.minimum(i, nu[0] - 1), 0)
    w1 = lambda i, e1, e2, nu: (layer, e1[i], 0, 0)
    w2 = lambda i, e1, e2, nu: (layer, e2[i], 0, 0)
    grid_spec = pltpu.PrefetchScalarGridSpec(
        num_scalar_prefetch=3,
        grid=(r_max // tr,),
        in_specs=[pl.BlockSpec((tr, dw), used),
                  pl.BlockSpec((1, 1, d, f), w1), pl.BlockSpec((1, 1, d, f), w1), pl.BlockSpec((1, 1, f, d), w1),
                  pl.BlockSpec((1, 1, d, f), w2), pl.BlockSpec((1, 1, d, f), w2), pl.BlockSpec((1, 1, f, d), w2),
                  pl.BlockSpec(memory_space=pl.ANY)],
        out_specs=pl.BlockSpec((tr, d), used),
        scratch_shapes=[pltpu.VMEM((4, d, f), BF16), pltpu.VMEM((2, f, d), BF16)],
    )
    return pl.pallas_call(
        _moe_experts_kernel,
        grid_spec=grid_spec,
        out_shape=jax.ShapeDtypeStruct(ys_buf.shape, ys_buf.dtype),
        input_output_aliases={10: 0},
        compiler_params=_params(("arbitrary",), 48),
        name="moe_experts",
    )(tile_e1, tile_e2, n_used, xs_buf, w_gate, w_up, w_down, w_gate, w_up, w_down, ys_buf)


def _moe_combine_ln_kernel(pos_ref, x_ref, ys_ref, g_ref, b_ref, y_ref, buf_ref, sem, *, alpha, tm):
    def start(r, c):
        _row_copy(ys_ref, pos_ref[0, 0, r], buf_ref, r, sem).start()
        return c

    def wait(r, c):
        _row_copy(ys_ref, pos_ref[0, 0, r], buf_ref, r, sem).wait()
        return c

    lax.fori_loop(0, tm, start, 0)
    lax.fori_loop(0, tm, wait, 0)
    y_ref[...] = _layer_norm(alpha * x_ref[...] + buf_ref[...], g_ref[...], b_ref[...])


def _moe_combine_ln(pos, x, ys_buf, g, b, alpha, tm=MOE_TOK_TILE):
    n, d = x.shape
    row = lambda i: (i, 0)
    fixed = lambda i: (0, 0)
    return pl.pallas_call(
        functools.partial(_moe_combine_ln_kernel, alpha=alpha, tm=tm),
        grid=(n // tm,),
        in_specs=[pl.BlockSpec((1, 8, tm), lambda i: (i, 0, 0), memory_space=pltpu.SMEM),
                  pl.BlockSpec((tm, d), row), pl.BlockSpec(memory_space=pl.ANY),
                  pl.BlockSpec((1, d), fixed), pl.BlockSpec((1, d), fixed)],
        out_specs=pl.BlockSpec((tm, d), row),
        out_shape=jax.ShapeDtypeStruct((n, d), F32),
        scratch_shapes=[pltpu.VMEM((tm, d), F32), pltpu.SemaphoreType.DMA(())],
        compiler_params=_params(("arbitrary",), 32),
        name="moe_combine_ln",
    )(pos, x, ys_buf, g, b)


def _bucket_experts():
    lo, hi = [], []
    for a in range(EXPERTS_PER_GROUP):
        for c in range(a + 1, EXPERTS_PER_GROUP):
            lo.append(a)
            hi.append(c)
    grp = np.repeat(np.arange(N_GROUPS), MOE_PAIRS) * EXPERTS_PER_GROUP
    return (grp + np.tile(lo, N_GROUPS)).astype(np.int32), (grp + np.tile(hi, N_GROUPS)).astype(np.int32)


def _moe_layer(x, xs_buf, ys_buf, w_group, b_group, w_expert, b_expert, w_gate, w_up, w_down, layer, g, b, alpha):
    tr = MOE_ROW_TILE
    xw, onehot, cnt = _router(x, w_group, b_group, w_expert, b_expert)
    counts = cnt[0, :MOE_BUCKETS].astype(jnp.int32)
    padded = (counts + (tr - 1)) // tr * tr
    ends = jnp.cumsum(padded)
    base = jnp.zeros((1, LANES), F32).at[0, :MOE_BUCKETS].set((ends - padded).astype(F32))
    n_used = ends[-1] // tr
    n_tiles = xs_buf.shape[0] // tr
    tile = jnp.minimum(jnp.arange(n_tiles, dtype=jnp.int32), n_used - 1)
    tile_bkt = jnp.sum((ends[None, :] <= (tile * tr)[:, None]).astype(jnp.int32), axis=1)
    tile_bkt = jnp.minimum(tile_bkt, MOE_BUCKETS - 1)
    e_lo, e_hi = _bucket_experts()
    pos = _moe_pos(onehot, base)
    xs_buf = _moe_scatter(pos, xw, xs_buf)
    ys_buf = _moe_experts(xs_buf, ys_buf, jnp.asarray(e_lo)[tile_bkt], jnp.asarray(e_hi)[tile_bkt],
                          n_used.reshape(1).astype(jnp.int32), w_gate, w_up, w_down, layer)
    return _moe_combine_ln(pos, x, ys_buf, g, b, alpha), xs_buf, ys_buf


def kernel(x, ln_mix_g, ln_mix_b, ln_ffn_g, ln_ffn_b, fox_w_in, fox_b_f, fox_w_out, hgrn_w_in, hgrn_lb_logits, hgrn_norm_g, hgrn_w_out, conv_w_in, conv_w, conv_w_out, moe_w_group, moe_b_group, moe_w_expert, moe_b_expert, moe_w_gate, moe_w_up, moe_w_down):
    bsz, t_len, d = x.shape
    depth = ln_mix_g.shape[0]
    alpha = float((2 * depth) ** 0.25)
    assert d == FOX_HEADS * FOX_HEAD_DIM == HGRN_HEADS * HGRN_DIM
    assert t_len % 512 == 0 and (bsz * t_len) % 1024 == 0

    lb_prob = jax.nn.softmax(hgrn_lb_logits.astype(F32), axis=0)
    lower_bounds = jnp.cumsum(lb_prob, axis=0) - lb_prob[0]

    h = x.reshape(bsz * t_len, d)
    r_max = bsz * t_len + MOE_BUCKETS * MOE_ROW_TILE
    xs_buf = jnp.zeros((r_max, d + META_W), F32)
    ys_buf = jnp.zeros((r_max, d), F32)
    for layer in range(depth):
        kind, j = layer % 3, layer // 3
        g_mix, b_mix = ln_mix_g[layer].reshape(1, d), ln_mix_b[layer].reshape(1, d)
        if kind == 0:
            h = _fox_layer(h, fox_w_in[j], fox_b_f[j], fox_w_out[j], g_mix, b_mix, alpha, bsz, t_len)
        elif kind == 1:
            h = _hgrn_layer(h, hgrn_w_in[j], lower_bounds[layer], hgrn_norm_g[j], hgrn_w_out[j],
                            g_mix, b_mix, alpha, bsz, t_len)
        else:
            h = _conv_layer(h, conv_w_in[j], conv_w[j], conv_w_out[j], g_mix, b_mix, alpha, bsz, t_len)
        h, xs_buf, ys_buf = _moe_layer(
            h, xs_buf, ys_buf, moe_w_group[layer], moe_b_group[layer], moe_w_expert[layer], moe_b_expert[layer],
            moe_w_gate, moe_w_up, moe_w_down, layer,
            ln_ffn_g[layer].reshape(1, d), ln_ffn_b[layer].reshape(1, d), alpha)
    return h.reshape(bsz, t_len, d)
```

```python
import functools

import numpy as np
import jax
import jax.numpy as jnp
from jax import lax
from jax.experimental import pallas as pl
from jax.experimental.pallas import tpu as pltpu

F32 = jnp.float32
BF16 = jnp.bfloat16

FOX_HEADS = 16
FOX_HEAD_DIM = 64
HGRN_HEADS = 8
HGRN_DIM = 128
HGRN_CHUNK = 128
N_GROUPS = 4
EXPERTS_PER_GROUP = 8
N_EXPERTS = N_GROUPS * EXPERTS_PER_GROUP
LN_EPS = 1e-5
RMS_EPS = 1e-6
LOG2E = 1.4426950408889634
LANES = 128
ROUTER_GROUP_LANE0 = N_EXPERTS
MOE_PAIRS = EXPERTS_PER_GROUP * (EXPERTS_PER_GROUP - 1) // 2
MOE_BUCKETS = N_GROUPS * MOE_PAIRS
MOE_ROW_TILE = 128
MOE_TOK_TILE = 256
TOKEN_TILE_ROWS = 8

_NT = (((1,), (1,)), ((), ()))
_TN = (((0,), (0,)), ((), ()))


def _params(semantics, vmem_mb):
    return pltpu.CompilerParams(dimension_semantics=semantics,
                                vmem_limit_bytes=vmem_mb * 1024 * 1024)


def _layer_norm(v, g, b):
    mu = jnp.mean(v, axis=-1, keepdims=True)
    d = v - mu
    var = jnp.mean(d * d, axis=-1, keepdims=True)
    return d * lax.rsqrt(var + LN_EPS) * g + b


def _sigmoid(v):
    return 1.0 / (1.0 + jnp.exp(-v))


def _proj_kernel(x_ref, *refs, n_out):
    xb = x_ref[...].astype(BF16)
    for w_ref, o_ref in zip(refs[:n_out], refs[n_out:]):
        o_ref[...] = jnp.dot(xb, w_ref[...], preferred_element_type=F32).astype(o_ref.dtype)


def _proj(x, ws, dtypes, name, tm=512):
    n, k = x.shape
    return pl.pallas_call(
        functools.partial(_proj_kernel, n_out=len(ws)),
        grid=(n // tm,),
        in_specs=[pl.BlockSpec((tm, k), lambda i: (i, 0))]
        + [pl.BlockSpec(w.shape, lambda i: (0, 0)) for w in ws],
        out_specs=[pl.BlockSpec((tm, w.shape[1]), lambda i: (i, 0)) for w in ws],
        out_shape=[jax.ShapeDtypeStruct((n, w.shape[1]), dt) for w, dt in zip(ws, dtypes)],
        compiler_params=_params(("arbitrary",), 48),
        name=name,
    )(x, *ws)


def _outproj_ln_kernel(o_ref, w_ref, x_ref, g_ref, b_ref, y_ref, *, alpha, transposed):
    dims = _TN if transposed else (((1,), (0,)), ((), ()))
    mixed = lax.dot_general(o_ref[...], w_ref[...], dims, preferred_element_type=F32)
    y_ref[...] = _layer_norm(alpha * x_ref[...] + mixed, g_ref[...], b_ref[...])


def _outproj_ln(o, w, x, g, b, alpha, name, tm=512, transposed=False):
    n, d = x.shape
    k = w.shape[0]
    row = lambda i: (i, 0)
    fixed = lambda i: (0, 0)
    o_spec = pl.BlockSpec((k, tm), lambda i: (0, i)) if transposed else pl.BlockSpec((tm, k), row)
    return pl.pallas_call(
        functools.partial(_outproj_ln_kernel, alpha=alpha, transposed=transposed),
        grid=(n // tm,),
        in_specs=[o_spec, pl.BlockSpec((k, d), fixed),
                  pl.BlockSpec((tm, d), row), pl.BlockSpec((1, d), fixed), pl.BlockSpec((1, d), fixed)],
        out_specs=pl.BlockSpec((tm, d), row),
        out_shape=jax.ShapeDtypeStruct((n, d), F32),
        compiler_params=_params(("arbitrary",), 32),
        name=name,
    )(o, w, x, g, b)


def _split3(v):
    hi = v.astype(BF16)
    r1 = v - hi.astype(F32)
    mid = r1.astype(BF16)
    return hi, mid, (r1 - mid.astype(F32)).astype(BF16)


def _fox_proj_kernel(x_ref, wqt_ref, wk_ref, wvt_ref, qt_ref, k_ref, vt_ref):
    xb = x_ref[...].astype(BF16)
    qt_ref[...] = lax.dot_general(wqt_ref[...], xb, _NT, preferred_element_type=F32).astype(BF16)
    k_ref[...] = jnp.dot(xb, wk_ref[...], preferred_element_type=F32).astype(BF16)
    vt_ref[...] = lax.dot_general(wvt_ref[...], xb, _NT, preferred_element_type=F32).astype(BF16)


def _fox_proj(x, wqt, wk, wvt, tm=512):
    n, d = x.shape
    fixed = lambda i: (0, 0)
    return pl.pallas_call(
        _fox_proj_kernel,
        grid=(n // tm,),
        in_specs=[pl.BlockSpec((tm, d), lambda i: (i, 0)), pl.BlockSpec((d, d), fixed),
                  pl.BlockSpec((d, d), fixed), pl.BlockSpec((d, d), fixed)],
        out_specs=[pl.BlockSpec((d, tm), lambda i: (0, i)), pl.BlockSpec((tm, d), lambda i: (i, 0)),
                   pl.BlockSpec((d, tm), lambda i: (0, i))],
        out_shape=[jax.ShapeDtypeStruct((d, n), BF16), jax.ShapeDtypeStruct((n, d), BF16),
                   jax.ShapeDtypeStruct((d, n), BF16)],
        compiler_params=_params(("arbitrary",), 48),
        name="fox_proj",
    )(x, wqt, wk, wvt)


def _fox_gate_kernel(x_ref, wf_ref, bf_ref, tri_ref, sel_ref, cb_ref, carry_ref):
    @pl.when(pl.program_id(1) == 0)
    def _():
        carry_ref[...] = jnp.zeros_like(carry_ref)

    z = jnp.dot(x_ref[...].astype(BF16), wf_ref[...], preferred_element_type=F32) + bf_ref[...]
    logf = jnp.minimum(z, 0.0) - jnp.log(1.0 + jnp.exp(-jnp.abs(z)))
    tri = tri_ref[...]
    c = carry_ref[...]
    for part in _split3(logf):
        c = c + jnp.dot(tri, part, preferred_element_type=F32)
    carry_ref[...] = c[c.shape[0] - 1:, :]
    cb = None
    for j, part in enumerate(_split3(c * (-LOG2E))):
        term = jnp.dot(part, sel_ref[j], preferred_element_type=F32)
        cb = term if cb is None else cb + term
    cb_ref[...] = cb.astype(BF16)


def _fox_gate(x, wf, bf, bsz, t_len, tg=512):
    n, d = x.shape
    nt = t_len // tg
    tri = jnp.asarray(np.tril(np.ones((tg, tg), np.float32)), BF16)
    sel = np.zeros((3, LANES, d), np.float32)
    for h in range(FOX_HEADS):
        for j in range(3):
            sel[j, h, (h // 2) * LANES + 3 * (h % 2) + j] = 1.0
    return pl.pallas_call(
        _fox_gate_kernel,
        grid=(bsz, nt),
        in_specs=[pl.BlockSpec((tg, d), lambda b, i: (b * nt + i, 0)),
                  pl.BlockSpec((d, LANES), lambda b, i: (0, 0)),
                  pl.BlockSpec((1, LANES), lambda b, i: (0, 0)),
                  pl.BlockSpec((tg, tg), lambda b, i: (0, 0)),
                  pl.BlockSpec((3, LANES, d), lambda b, i: (0, 0, 0))],
        out_specs=pl.BlockSpec((tg, d), lambda b, i: (b * nt + i, 0)),
        out_shape=jax.ShapeDtypeStruct((n, d), BF16),
        scratch_shapes=[pltpu.VMEM((1, LANES), F32)],
        compiler_params=_params(("arbitrary", "arbitrary"), 32),
        name="fox_gate",
    )(x, wf, bf, tri, jnp.asarray(sel, BF16))


def _fox_attn_kernel(qt_ref, k_ref, cb_ref, vt_ref, ot_ref, s_ref, *, tq):
    qi = pl.program_id(2)
    qt = qt_ref[...].astype(F32)
    feat = lax.broadcasted_iota(jnp.int32, qt.shape, 0)
    rhs = []
    for h in range(2):
        own = (feat >= h * FOX_HEAD_DIM) & (feat < (h + 1) * FOX_HEAD_DIM)
        bias_rows = (feat >= 3 * h) & (feat < 3 * h + 3)
        rhs.append(jnp.concatenate([jnp.where(own, qt, 0.0).astype(BF16),
                                    jnp.where(bias_rows, 1.0, 0.0).astype(BF16)], axis=0))
    key_i = lax.broadcasted_iota(jnp.int32, (tq, tq), 0)
    qry_i = lax.broadcasted_iota(jnp.int32, (tq, tq), 1)

    def scores(j, slot):
        start = pl.multiple_of(j * tq, tq)
        kext = jnp.concatenate([k_ref[pl.ds(start, tq), :], cb_ref[pl.ds(start, tq), :]], axis=1)
        for h in range(2):
            s_ref[slot, h] = jnp.dot(kext, rhs[h], preferred_element_type=F32)

    def block(j, carry, slot, masked):
        start = pl.multiple_of(j * tq, tq)
        if not masked:
            scores(j + 1, 1 - slot)
        out = []
        for h in range(2):
            m, l, acc = carry[h]
            st = s_ref[slot, h]
            if masked:
                st = jnp.where(key_i <= qry_i, st, -jnp.inf)
            m_new = jnp.maximum(m, jnp.max(st, axis=0, keepdims=True))
            p = jnp.exp2(st - m_new)
            a = jnp.exp2(m - m_new)
            vt = vt_ref[h * FOX_HEAD_DIM:(h + 1) * FOX_HEAD_DIM, pl.ds(start, tq)]
            pv = jnp.dot(vt, p.astype(BF16), preferred_element_type=F32)
            out.append((m_new, a * l + jnp.sum(p, axis=0, keepdims=True), acc * a + pv))
        return tuple(out)

    init = (jnp.full((1, tq), -jnp.inf, F32), jnp.zeros((1, tq), F32), jnp.zeros((FOX_HEAD_DIM, tq), F32))
    scores(0, 0)
    carry = lax.fori_loop(
        0, qi // 2, lambda i, c: block(2 * i + 1, block(2 * i, c, 0, False), 1, False), (init, init))
    carry = lax.cond(
        qi % 2 == 0,
        lambda c: block(qi, c, 0, True),
        lambda c: block(qi, block(qi - 1, c, 0, False), 1, True),
        carry)
    for h in range(2):
        _, l, acc = carry[h]
        ot_ref[h * FOX_HEAD_DIM:(h + 1) * FOX_HEAD_DIM, :] = (acc / l).astype(ot_ref.dtype)


def _fox_attn(qt, k, cb, vt, bsz, t_len, tq=256):
    d, n = qt.shape
    n_pairs = d // LANES
    nq = t_len // tq
    return pl.pallas_call(
        functools.partial(_fox_attn_kernel, tq=tq),
        grid=(bsz, n_pairs, nq),
        in_specs=[pl.BlockSpec((LANES, tq), lambda b, p, i: (p, b * nq + i)),
                  pl.BlockSpec((t_len, LANES), lambda b, p, i: (b, p)),
                  pl.BlockSpec((t_len, LANES), lambda b, p, i: (b, p)),
                  pl.BlockSpec((LANES, t_len), lambda b, p, i: (p, b))],
        out_specs=pl.BlockSpec((LANES, tq), lambda b, p, i: (p, b * nq + i)),
        out_shape=jax.ShapeDtypeStruct((d, n), BF16),
        scratch_shapes=[pltpu.VMEM((2, 2, tq, tq), F32)],
        compiler_params=_params(("arbitrary", "arbitrary", "arbitrary"), 32),
        name="fox_attn",
    )(qt, k, cb, vt)


def _fox_layer(x, w_in, b_f, w_out, g, b, alpha, bsz, t_len):
    d = x.shape[1]
    wqt = (w_in[:, :d] * (FOX_HEAD_DIM ** -0.5 * LOG2E)).T.astype(BF16)
    wk = w_in[:, d:2 * d].astype(BF16)
    wvt = w_in[:, 2 * d:3 * d].T.astype(BF16)
    wf = jnp.zeros((d, LANES), F32).at[:, :FOX_HEADS].set(w_in[:, 3 * d:]).astype(BF16)
    bf = jnp.zeros((1, LANES), F32).at[0, :FOX_HEADS].set(b_f)
    qt, k, vt = _fox_proj(x, wqt, wk, wvt)
    cb = _fox_gate(x, wf, bf, bsz, t_len)
    ot = _fox_attn(qt, k, cb, vt, bsz, t_len)
    return _outproj_ln(ot, w_out.astype(BF16), x, g, b, alpha, "fox_out_ln", transposed=True)


def _hgrn_constants():
    c = HGRN_CHUNK
    r = np.arange(c)[:, None]
    j = np.arange(c)[None, :]
    blocks = [j <= r, j > r]
    masks = []
    levels = []
    half = c // 2
    while half >= 1:
        ref = (r // (2 * half)) * (2 * half) + half - 1
        upper = (r % (2 * half)) >= half
        blocks.append(np.where(upper, (j > ref) & (j <= r), (j > r) & (j <= ref)))
        masks.append(((r // (2 * half)) == (j // (2 * half))) & upper & ((j % (2 * half)) < half))
        levels.append(half)
        half //= 2
    masks.append(r == j)
    wall = np.concatenate(blocks, axis=0).astype(np.float32)
    wall2 = np.concatenate([wall, wall], axis=1)
    return wall2, np.stack(masks).astype(np.float32), tuple(levels)


def _hgrn_kernel(q_ref, fl_ref, i_ref, g_ref, lb_ref, ng_ref, wall_ref, mask_ref, o_ref, s_ref, *, levels):
    c = HGRN_CHUNK

    @pl.when(pl.program_id(1) == 0)
    def _():
        s_ref[...] = jnp.zeros_like(s_ref)

    wall = wall_ref[...]
    rowi = lax.broadcasted_iota(jnp.int32, (c, HGRN_DIM), 0)
    for h in range(HGRN_HEADS):
        sl = slice(h * HGRN_DIM, (h + 1) * HGRN_DIM)
        q = q_ref[:, sl].astype(F32)
        i_b = i_ref[:, sl]
        lb = lb_ref[:, sl]
        f = lb + (1.0 - lb) * _sigmoid(fl_ref[:, sl])
        logf = jnp.log(f)
        k = 1.0 - f
        hi = logf.astype(BF16)
        mid = (logf - hi.astype(F32)).astype(BF16)
        x_all = jnp.exp(jnp.dot(wall, jnp.concatenate([hi, mid], axis=0), preferred_element_type=F32))
        x_pre = x_all[0:c]
        x_suf = x_all[c:2 * c]
        st = s_ref[h]
        o = lax.dot_general((q * x_pre).astype(BF16), st.astype(BF16), _NT, preferred_element_type=F32)
        upd = lax.dot_general(i_b, (k * x_suf).astype(BF16), _TN, preferred_element_type=F32)
        s_ref[h] = st * x_pre[c - 1:c, :] + upd
        a = jnp.zeros((c, c), F32)
        for l, half in enumerate(levels):
            upper = (rowi & half) != 0
            z = (jnp.where(upper, q, k) * x_all[(2 + l) * c:(3 + l) * c]).astype(BF16)
            a = a + lax.dot_general(z, z, _NT, preferred_element_type=F32) * mask_ref[l]
        a = a + lax.dot_general(q.astype(BF16), k.astype(BF16), _NT,
                                preferred_element_type=F32) * mask_ref[len(levels)]
        o = o + jnp.dot(a.astype(BF16), i_b, preferred_element_type=F32)
        o = o * lax.rsqrt(jnp.mean(o * o, axis=-1, keepdims=True) + RMS_EPS) * ng_ref[...]
        gate = g_ref[:, sl].astype(F32)
        o_ref[:, sl] = (o * (gate * _sigmoid(gate))).astype(o_ref.dtype)


def _hgrn_core(q, fl, i, g, lb, ng, bsz, t_len):
    n, d = q.shape
    c = HGRN_CHUNK
    nc = t_len // c
    wall2, masks, levels = _hgrn_constants()
    row = lambda b, j: (b * nc + j, 0)
    fixed2 = lambda b, j: (0, 0)
    return pl.pallas_call(
        functools.partial(_hgrn_kernel, levels=levels),
        grid=(bsz, nc),
        in_specs=[pl.BlockSpec((c, d), row), pl.BlockSpec((c, d), row), pl.BlockSpec((c, d), row),
                  pl.BlockSpec((c, d), row), pl.BlockSpec((1, d), fixed2), pl.BlockSpec((1, HGRN_DIM), fixed2),
                  pl.BlockSpec(wall2.shape, fixed2), pl.BlockSpec(masks.shape, lambda b, j: (0, 0, 0))],
        out_specs=pl.BlockSpec((c, d), row),
        out_shape=jax.ShapeDtypeStruct((n, d), BF16),
        scratch_shapes=[pltpu.VMEM((HGRN_HEADS, HGRN_DIM, HGRN_DIM), F32)],
        compiler_params=_params(("arbitrary", "arbitrary"), 32),
        name="hgrn_core",
    )(q, fl, i, g, lb, ng, jnp.asarray(wall2, BF16), jnp.asarray(masks, F32))


def _hgrn_layer(x, w_in, lower_bound, norm_g, w_out, g, b, alpha, bsz, t_len):
    d = x.shape[1]
    wb = w_in.astype(BF16)
    q, fl, i, gate = _proj(x, [wb[:, :d], wb[:, d:2 * d], wb[:, 2 * d:3 * d], wb[:, 3 * d:]],
                           [BF16, F32, BF16, BF16], "hgrn_proj")
    o = _hgrn_core(q, fl, i, gate, lower_bound.reshape(1, d), norm_g.reshape(1, HGRN_DIM), bsz, t_len)
    return _outproj_ln(o, w_out.astype(BF16), x, g, b, alpha, "hgrn_out_ln")


def _conv_kernel(x_ref, win_ref, cw_ref, wout_ref, g_ref, b_ref, y_ref, zbuf, *, alpha, tm):
    d = x_ref.shape[1]

    @pl.when(pl.program_id(1) == 0)
    def _():
        zbuf[0:8, :] = jnp.zeros((8, d), F32)

    x = x_ref[...]
    p = jnp.dot(x.astype(BF16), win_ref[...], preferred_element_type=F32)
    z = p[:, d:2 * d] * p[:, 2 * d:]
    zbuf[8:8 + tm, :] = z
    y = cw_ref[2:3, :] * z + cw_ref[1:2, :] * zbuf[7:7 + tm, :] + cw_ref[0:1, :] * zbuf[6:6 + tm, :]
    zbuf[0:8, :] = z[tm - 8:, :]
    mixed = jnp.dot((p[:, :d] * y).astype(BF16), wout_ref[...], preferred_element_type=F32)
    y_ref[...] = _layer_norm(alpha * x + mixed, g_ref[...], b_ref[...])


def _conv_layer(x, w_in, conv_w, w_out, g, b, alpha, bsz, t_len, tm=512):
    n, d = x.shape
    nt = t_len // tm
    row = lambda bb, i: (bb * nt + i, 0)
    fixed = lambda bb, i: (0, 0)
    return pl.pallas_call(
        functools.partial(_conv_kernel, alpha=alpha, tm=tm),
        grid=(bsz, nt),
        in_specs=[pl.BlockSpec((tm, d), row), pl.BlockSpec((d, 3 * d), fixed), pl.BlockSpec(conv_w.shape, fixed),
                  pl.BlockSpec((d, d), fixed), pl.BlockSpec((1, d), fixed), pl.BlockSpec((1, d), fixed)],
        out_specs=pl.BlockSpec((tm, d), row),
        out_shape=jax.ShapeDtypeStruct((n, d), F32),
        scratch_shapes=[pltpu.VMEM((tm + 8, d), F32)],
        compiler_params=_params(("arbitrary", "arbitrary"), 48),
        name="conv_layer",
    )(x, w_in.astype(BF16), conv_w, w_out.astype(BF16), g, b)


def _store_token_tiles(xt_ref, v, row0=0):
    rows = v.shape[0]
    for c in range(TOKEN_TILE_ROWS):
        xt_ref[pl.ds(row0 + c, rows, stride=TOKEN_TILE_ROWS), :] = v[:, c * LANES:(c + 1) * LANES]


def _load_token_tiles(xt_ref, rows, row0=0):
    return jnp.concatenate([xt_ref[pl.ds(row0 + c, rows, stride=TOKEN_TILE_ROWS), :]
                            for c in range(TOKEN_TILE_ROWS)], axis=1)


def _router_kernel(x_ref, wh_ref, wl_ref, b_ref, xt_ref, meta_ref, oh_ref, cnt_ref):
    x = x_ref[...]
    xh = x.astype(BF16)
    xl = (x - xh.astype(F32)).astype(BF16)
    wh = wh_ref[...]
    logits = (jnp.dot(xh, wh, preferred_element_type=F32) + jnp.dot(xh, wl_ref[...], preferred_element_type=F32)
              + jnp.dot(xl, wh, preferred_element_type=F32)) + b_ref[...]
    lane = lax.broadcasted_iota(jnp.int32, logits.shape, 1)
    lane_f = lane.astype(F32)
    neg = -jnp.inf
    far = float(LANES)
    is_g = (lane >= ROUTER_GROUP_LANE0) & (lane < ROUTER_GROUP_LANE0 + N_GROUPS)
    gl = jnp.where(is_g, logits, neg)
    gmax = jnp.max(gl, axis=-1, keepdims=True)
    g_sel = jnp.min(jnp.where(gl == gmax, lane_f, far), axis=-1, keepdims=True) - float(ROUTER_GROUP_LANE0)
    grp_w = 1.0 / jnp.sum(jnp.where(is_g, jnp.exp(logits - gmax), 0.0), axis=-1, keepdims=True)
    in_g = (lane < N_EXPERTS) & ((lane // EXPERTS_PER_GROUP).astype(F32) == g_sel)
    el = jnp.where(in_g, logits, neg)
    t1 = jnp.max(el, axis=-1, keepdims=True)
    i1 = jnp.min(jnp.where(el == t1, lane_f, far), axis=-1, keepdims=True)
    el2 = jnp.where(lane_f == i1, neg, el)
    t2 = jnp.max(el2, axis=-1, keepdims=True)
    i2 = jnp.min(jnp.where(el2 == t2, lane_f, far), axis=-1, keepdims=True)
    e2 = jnp.exp(t2 - t1)
    w1 = grp_w / (1.0 + e2)
    w2 = w1 * e2
    first_lo = i1 < i2
    lo = jnp.where(first_lo, i1, i2) - g_sel * EXPERTS_PER_GROUP
    hi = jnp.where(first_lo, i2, i1) - g_sel * EXPERTS_PER_GROUP
    pair = lo * (EXPERTS_PER_GROUP - 1.0) - lo * (lo - 1.0) * 0.5 + (hi - lo - 1.0)
    onehot = jnp.where(lane_f == g_sel * MOE_PAIRS + pair, 1.0, 0.0)
    meta_ref[...] = (jnp.where(lane == 0, jnp.where(first_lo, w1, w2), 0.0)
                     + jnp.where(lane == 1, jnp.where(first_lo, w2, w1), 0.0))
    _store_token_tiles(xt_ref, x)
    oh_ref[...] = onehot.astype(BF16)

    @pl.when(pl.program_id(0) == 0)
    def _():
        cnt_ref[...] = jnp.zeros_like(cnt_ref)

    cnt_ref[...] += jnp.sum(onehot, axis=0, keepdims=True)


def _router(x, w_group, b_group, w_expert, b_expert, tm=MOE_TOK_TILE):
    n, d = x.shape
    w = jnp.zeros((d, LANES), F32).at[:, :N_EXPERTS].set(w_expert)
    w = w.at[:, ROUTER_GROUP_LANE0:ROUTER_GROUP_LANE0 + N_GROUPS].set(w_group)
    bias = jnp.zeros((1, LANES), F32).at[0, :N_EXPERTS].set(b_expert)
    bias = bias.at[0, ROUTER_GROUP_LANE0:ROUTER_GROUP_LANE0 + N_GROUPS].set(b_group)
    wh = w.astype(BF16)
    wl = (w - wh.astype(F32)).astype(BF16)
    row = lambda i: (i, 0)
    fixed = lambda i: (0, 0)
    return pl.pallas_call(
        _router_kernel,
        grid=(n // tm,),
        in_specs=[pl.BlockSpec((tm, d), row), pl.BlockSpec((d, LANES), fixed), pl.BlockSpec((d, LANES), fixed),
                  pl.BlockSpec((1, LANES), fixed)],
        out_specs=[pl.BlockSpec((tm * TOKEN_TILE_ROWS, LANES), row), pl.BlockSpec((tm, LANES), row),
                   pl.BlockSpec((tm, LANES), row), pl.BlockSpec((8, LANES), fixed)],
        out_shape=[jax.ShapeDtypeStruct((n * TOKEN_TILE_ROWS, LANES), F32), jax.ShapeDtypeStruct((n, LANES), F32),
                   jax.ShapeDtypeStruct((n, LANES), BF16), jax.ShapeDtypeStruct((8, LANES), F32)],
        compiler_params=_params(("arbitrary",), 32),
        name="moe_router",
    )(x, wh, wl, bias)


def _moe_pos_kernel(oh_ref, base_ref, lt_ref, pos_ref, carry_ref):
    @pl.when(pl.program_id(0) == 0)
    def _():
        carry_ref[...] = jnp.zeros_like(carry_ref)

    oh = oh_ref[...]
    rank = jnp.dot(lt_ref[...], oh, preferred_element_type=F32) + carry_ref[...] + base_ref[...]
    ohf = oh.astype(F32)
    val = ohf * rank
    carry_ref[...] += jnp.sum(ohf, axis=0, keepdims=True)
    hi = jnp.floor(val * (1.0 / 256.0))
    lo = val - 256.0 * hi
    ones = jnp.ones((8, LANES), BF16)
    pos = (256.0 * lax.dot_general(ones, hi.astype(BF16), _NT, preferred_element_type=F32)
           + lax.dot_general(ones, lo.astype(BF16), _NT, preferred_element_type=F32))
    pos_ref[0] = pos.astype(jnp.int32)


def _moe_pos(onehot, base, tm=MOE_TOK_TILE):
    n = onehot.shape[0]
    lt = jnp.asarray(np.tril(np.ones((tm, tm), np.float32), -1), BF16)
    return pl.pallas_call(
        _moe_pos_kernel,
        grid=(n // tm,),
        in_specs=[pl.BlockSpec((tm, LANES), lambda i: (i, 0)), pl.BlockSpec((1, LANES), lambda i: (0, 0)),
                  pl.BlockSpec((tm, tm), lambda i: (0, 0))],
        out_specs=pl.BlockSpec((1, 8, tm), lambda i: (i, 0, 0)),
        out_shape=jax.ShapeDtypeStruct((n // tm, 8, tm), jnp.int32),
        scratch_shapes=[pltpu.VMEM((1, LANES), F32)],
        compiler_params=_params(("arbitrary",), 32),
        name="moe_pos",
    )(onehot, base, lt)


def _tile_copy(src, src_tok, dst, dst_tok, sem):
    rows = TOKEN_TILE_ROWS
    return pltpu.make_async_copy(src.at[pl.ds(pl.multiple_of(src_tok * rows, rows), rows)],
                                 dst.at[pl.ds(pl.multiple_of(dst_tok * rows, rows), rows)], sem)


def _meta_copy(src, src_tok, dst, dst_tok, sem):
    return pltpu.make_async_copy(src.at[pl.ds(src_tok, 1)], dst.at[pl.ds(dst_tok, 1)], sem)


def _moe_scatter_kernel(pos_ref, xt_ref, meta_ref, xs_in_ref, xm_in_ref, xs_ref, xm_ref, sem, *, tm):
    del xs_in_ref, xm_in_ref

    def start(r, c):
        _tile_copy(xt_ref, r, xs_ref, pos_ref[0, 0, r], sem.at[0]).start(priority=0)
        _meta_copy(meta_ref, r, xm_ref, pos_ref[0, 0, r], sem.at[1]).start(priority=1)
        return c

    def wait(r, c):
        _tile_copy(xt_ref, r, xs_ref, pos_ref[0, 0, r], sem.at[0]).wait()
        _meta_copy(meta_ref, r, xm_ref, pos_ref[0, 0, r], sem.at[1]).wait()
        return c

    lax.fori_loop(0, tm, start, 0, unroll=8)
    lax.fori_loop(0, tm, wait, 0, unroll=8)


def _moe_scatter(pos, xt, meta, xs_buf, xm_buf, tm=MOE_TOK_TILE):
    n = meta.shape[0]
    return pl.pallas_call(
        functools.partial(_moe_scatter_kernel, tm=tm),
        grid=(n // tm,),
        in_specs=[pl.BlockSpec((1, 8, tm), lambda i: (i, 0, 0), memory_space=pltpu.SMEM),
                  pl.BlockSpec((tm * TOKEN_TILE_ROWS, LANES), lambda i: (i, 0)),
                  pl.BlockSpec((tm, LANES), lambda i: (i, 0)),
                  pl.BlockSpec(memory_space=pl.ANY), pl.BlockSpec(memory_space=pl.ANY)],
        out_specs=[pl.BlockSpec(memory_space=pl.ANY), pl.BlockSpec(memory_space=pl.ANY)],
        out_shape=[jax.ShapeDtypeStruct(xs_buf.shape, xs_buf.dtype), jax.ShapeDtypeStruct(xm_buf.shape, xm_buf.dtype)],
        scratch_shapes=[pltpu.SemaphoreType.DMA((2,))],
        input_output_aliases={3: 0, 4: 1},
        compiler_params=_params(("arbitrary",), 32),
        name="moe_scatter",
    )(pos, xt, meta, xs_buf, xm_buf)


def _moe_experts_kernel(grp_ref, e1_ref, e2_ref, nused_ref, xs_ref, xm_ref, wg_ref, wu_ref, wd_ref, ys_in_ref, ys_ref):
    del grp_ref, ys_in_ref
    i = pl.program_id(0)
    tr = xm_ref.shape[0]

    @pl.when(i < nused_ref[0])
    def _():
        x = _load_token_tiles(xs_ref, tr).astype(BF16)
        meta = xm_ref[...]
        y = None
        for k, e_ref in enumerate((e1_ref, e2_ref)):
            e = e_ref[i]
            hg = jnp.dot(x, wg_ref[0, e], preferred_element_type=F32)
            hu = jnp.dot(x, wu_ref[0, e], preferred_element_type=F32)
            h = hg * _sigmoid(hg) * hu * meta[:, k:k + 1]
            yk = jnp.dot(h.astype(BF16), wd_ref[0, e], preferred_element_type=F32)
            y = yk if y is None else y + yk
        _store_token_tiles(ys_ref, y)


def _moe_experts(xs_buf, xm_buf, ys_buf, tile_grp, tile_e1, tile_e2, n_used, w_gate, w_up, w_down, layer,
                 tr=MOE_ROW_TILE):
    r_max = xm_buf.shape[0]
    d, f = w_gate.shape[-2:]
    assert d == TOKEN_TILE_ROWS * LANES
    tile_rows = tr * TOKEN_TILE_ROWS
    epg = EXPERTS_PER_GROUP
    used = lambda i, grp, e1, e2, nu: (jnp.minimum(i, nu[0] - 1), 0)
    wmap = lambda i, grp, e1, e2, nu: (layer * N_GROUPS + grp[i], 0, 0, 0)
    grid_spec = pltpu.PrefetchScalarGridSpec(
        num_scalar_prefetch=4,
        grid=(r_max // tr,),
        in_specs=[pl.BlockSpec((tile_rows, LANES), used), pl.BlockSpec((tr, LANES), used),
                  pl.BlockSpec((1, epg, d, f), wmap), pl.BlockSpec((1, epg, d, f), wmap),
                  pl.BlockSpec((1, epg, f, d), wmap), pl.BlockSpec(memory_space=pl.ANY)],
        out_specs=pl.BlockSpec((tile_rows, LANES), used),
    )
    return pl.pallas_call(
        _moe_experts_kernel,
        grid_spec=grid_spec,
        out_shape=jax.ShapeDtypeStruct(ys_buf.shape, ys_buf.dtype),
        input_output_aliases={9: 0},
        compiler_params=_params(("arbitrary",), 48),
        name="moe_experts",
    )(tile_grp, tile_e1, tile_e2, n_used, xs_buf, xm_buf, w_gate, w_up, w_down, ys_buf)


def _moe_combine_ln_kernel(pos_ref, nxt_ref, x_ref, ys_ref, g_ref, b_ref, y_ref, buf_ref, sem, *, alpha, tm):
    i = pl.program_id(0)
    slot = i % 2

    def gather(p_ref, s):
        def start(r, c):
            _tile_copy(ys_ref, p_ref[0, 0, r], buf_ref, s * tm + r, sem.at[s]).start()
            return c
        lax.fori_loop(0, tm, start, 0, unroll=8)

    @pl.when(i == 0)
    def _():
        gather(pos_ref, 0)

    @pl.when(i + 1 < pl.num_programs(0))
    def _():
        gather(nxt_ref, 1 - slot)

    def wait(r, c):
        _tile_copy(ys_ref, 0, buf_ref, slot * tm + r, sem.at[slot]).wait()
        return c

    lax.fori_loop(0, tm, wait, 0, unroll=8)
    base = pl.multiple_of(slot * (tm * TOKEN_TILE_ROWS), tm * TOKEN_TILE_ROWS)
    ffn = _load_token_tiles(buf_ref, tm, base)
    y_ref[...] = _layer_norm(alpha * x_ref[...] + ffn, g_ref[...], b_ref[...])


def _moe_combine_ln(pos, x, ys_buf, g, b, alpha, tm=MOE_TOK_TILE):
    n, d = x.shape
    nt = n // tm
    row = lambda i: (i, 0)
    fixed = lambda i: (0, 0)
    return pl.pallas_call(
        functools.partial(_moe_combine_ln_kernel, alpha=alpha, tm=tm),
        grid=(nt,),
        in_specs=[pl.BlockSpec((1, 8, tm), lambda i: (i, 0, 0), memory_space=pltpu.SMEM),
                  pl.BlockSpec((1, 8, tm), lambda i: (jnp.minimum(i + 1, nt - 1), 0, 0), memory_space=pltpu.SMEM),
                  pl.BlockSpec((tm, d), row), pl.BlockSpec(memory_space=pl.ANY),
                  pl.BlockSpec((1, d), fixed), pl.BlockSpec((1, d), fixed)],
        out_specs=pl.BlockSpec((tm, d), row),
        out_shape=jax.ShapeDtypeStruct((n, d), F32),
        scratch_shapes=[pltpu.VMEM((2 * tm * TOKEN_TILE_ROWS, LANES), F32), pltpu.SemaphoreType.DMA((2,))],
        compiler_params=_params(("arbitrary",), 32),
        name="moe_combine_ln",
    )(pos, pos, x, ys_buf, g, b)


def _bucket_experts():
    lo, hi = [], []
    for a in range(EXPERTS_PER_GROUP):
        for c in range(a + 1, EXPERTS_PER_GROUP):
            lo.append(a)
            hi.append(c)
    grp = np.repeat(np.arange(N_GROUPS), MOE_PAIRS).astype(np.int32)
    return grp, np.tile(lo, N_GROUPS).astype(np.int32), np.tile(hi, N_GROUPS).astype(np.int32)


def _moe_layer(x, bufs, w_group, b_group, w_expert, b_expert, w_gate, w_up, w_down, layer, g, b, alpha):
    tr = MOE_ROW_TILE
    xs_buf, xm_buf, ys_buf = bufs
    xt, meta, onehot, cnt = _router(x, w_group, b_group, w_expert, b_expert)
    counts = cnt[0, :MOE_BUCKETS].astype(jnp.int32)
    padded = (counts + (tr - 1)) // tr * tr
    ends = jnp.cumsum(padded)
    base = jnp.zeros((1, LANES), F32).at[0, :MOE_BUCKETS].set((ends - padded).astype(F32))
    n_used = ends[-1] // tr
    n_tiles = xm_buf.shape[0] // tr
    tile = jnp.minimum(jnp.arange(n_tiles, dtype=jnp.int32), n_used - 1)
    tile_bkt = jnp.sum((ends[None, :] <= (tile * tr)[:, None]).astype(jnp.int32), axis=1)
    tile_bkt = jnp.minimum(tile_bkt, MOE_BUCKETS - 1)
    b_grp, b_lo, b_hi = (jnp.asarray(t)[tile_bkt] for t in _bucket_experts())
    pos = _moe_pos(onehot, base)
    xs_buf, xm_buf = _moe_scatter(pos, xt, meta, xs_buf, xm_buf)
    ys_buf = _moe_experts(xs_buf, xm_buf, ys_buf, b_grp, b_lo, b_hi, n_used.reshape(1).astype(jnp.int32),
                          w_gate, w_up, w_down, layer)
    return _moe_combine_ln(pos, x, ys_buf, g, b, alpha), (xs_buf, xm_buf, ys_buf)


def kernel(x, ln_mix_g, ln_mix_b, ln_ffn_g, ln_ffn_b, fox_w_in, fox_b_f, fox_w_out, hgrn_w_in, hgrn_lb_logits, hgrn_norm_g, hgrn_w_out, conv_w_in, conv_w, conv_w_out, moe_w_group, moe_b_group, moe_w_expert, moe_b_expert, moe_w_gate, moe_w_up, moe_w_down):
    bsz, t_len, d = x.shape
    depth = ln_mix_g.shape[0]
    alpha = float((2 * depth) ** 0.25)
    assert d == FOX_HEADS * FOX_HEAD_DIM == HGRN_HEADS * HGRN_DIM
    assert t_len % 512 == 0 and (bsz * t_len) % 1024 == 0

    lb_prob = jax.nn.softmax(hgrn_lb_logits.astype(F32), axis=0)
    lower_bounds = jnp.cumsum(lb_prob, axis=0) - lb_prob[0]

    h = x.reshape(bsz * t_len, d)
    r_max = bsz * t_len + MOE_BUCKETS * MOE_ROW_TILE
    by_group = lambda w: w.astype(BF16).reshape((depth * N_GROUPS, EXPERTS_PER_GROUP) + w.shape[2:])
    moe_w_gate, moe_w_up, moe_w_down = by_group(moe_w_gate), by_group(moe_w_up), by_group(moe_w_down)
    bufs = (jnp.zeros((r_max * TOKEN_TILE_ROWS, LANES), F32), jnp.zeros((r_max, LANES), F32),
            jnp.zeros((r_max * TOKEN_TILE_ROWS, LANES), F32))
    for layer in range(depth):
        kind, j = layer % 3, layer // 3
        g_mix, b_mix = ln_mix_g[layer].reshape(1, d), ln_mix_b[layer].reshape(1, d)
        if kind == 0:
            h = _fox_layer(h, fox_w_in[j], fox_b_f[j], fox_w_out[j], g_mix, b_mix, alpha, bsz, t_len)
        elif kind == 1:
            h = _hgrn_layer(h, hgrn_w_in[j], lower_bounds[layer], hgrn_norm_g[j], hgrn_w_out[j],
                            g_mix, b_mix, alpha, bsz, t_len)
        else:
            h = _conv_layer(h, conv_w_in[j], conv_w[j], conv_w_out[j], g_mix, b_mix, alpha, bsz, t_len)
        h, bufs = _moe_layer(
            h, bufs, moe_w_group[layer], moe_b_group[layer], moe_w_expert[layer], moe_b_expert[layer],
            moe_w_gate, moe_w_up, moe_w_down, layer,
            ln_ffn_g[layer].reshape(1, d), ln_ffn_b[layer].reshape(1, d), alpha)
    return h.reshape(bsz, t_len, d)
```

```python
import functools

import numpy as np
import jax
import jax.numpy as jnp
from jax import lax
from jax.experimental import pallas as pl
from jax.experimental.pallas import tpu as pltpu

F32 = jnp.float32
BF16 = jnp.bfloat16

FOX_HEADS = 16
FOX_HEAD_DIM = 64
HGRN_HEADS = 8
HGRN_DIM = 128
HGRN_CHUNK = 128
HGRN_SMALL_LEVEL = 8
N_GROUPS = 4
EXPERTS_PER_GROUP = 8
N_EXPERTS = N_GROUPS * EXPERTS_PER_GROUP
LN_EPS = 1e-5
RMS_EPS = 1e-6
LOG2E = 1.4426950408889634
LANES = 128
ROUTER_GROUP_LANE0 = N_EXPERTS
MOE_PAIRS = EXPERTS_PER_GROUP * (EXPERTS_PER_GROUP - 1) // 2
MOE_BUCKETS = N_GROUPS * MOE_PAIRS
MOE_ROW_TILE = 128
MOE_TOK_TILE = 256
TOKEN_TILE_ROWS = 8

_NT = (((1,), (1,)), ((), ()))
_TN = (((0,), (0,)), ((), ()))


def _params(semantics, vmem_mb):
    return pltpu.CompilerParams(dimension_semantics=semantics,
                                vmem_limit_bytes=vmem_mb * 1024 * 1024)


def _layer_norm(v, g, b):
    mu = jnp.mean(v, axis=-1, keepdims=True)
    d = v - mu
    var = jnp.mean(d * d, axis=-1, keepdims=True)
    return d * lax.rsqrt(var + LN_EPS) * g + b


def _sigmoid(v):
    return 1.0 / (1.0 + jnp.exp(-v))


def _proj_kernel(x_ref, *refs, n_out):
    xb = x_ref[...].astype(BF16)
    for w_ref, o_ref in zip(refs[:n_out], refs[n_out:]):
        o_ref[...] = jnp.dot(xb, w_ref[...], preferred_element_type=F32).astype(o_ref.dtype)


def _proj(x, ws, dtypes, name, tm=512):
    n, k = x.shape
    return pl.pallas_call(
        functools.partial(_proj_kernel, n_out=len(ws)),
        grid=(n // tm,),
        in_specs=[pl.BlockSpec((tm, k), lambda i: (i, 0))]
        + [pl.BlockSpec(w.shape, lambda i: (0, 0)) for w in ws],
        out_specs=[pl.BlockSpec((tm, w.shape[1]), lambda i: (i, 0)) for w in ws],
        out_shape=[jax.ShapeDtypeStruct((n, w.shape[1]), dt) for w, dt in zip(ws, dtypes)],
        compiler_params=_params(("arbitrary",), 48),
        name=name,
    )(x, *ws)


def _outproj_ln_kernel(o_ref, w_ref, x_ref, g_ref, b_ref, y_ref, *, alpha, transposed):
    dims = _TN if transposed else (((1,), (0,)), ((), ()))
    mixed = lax.dot_general(o_ref[...], w_ref[...], dims, preferred_element_type=F32)
    y_ref[...] = _layer_norm(alpha * x_ref[...] + mixed, g_ref[...], b_ref[...])


def _outproj_ln(o, w, x, g, b, alpha, name, tm=512, transposed=False):
    n, d = x.shape
    k = w.shape[0]
    row = lambda i: (i, 0)
    fixed = lambda i: (0, 0)
    o_spec = pl.BlockSpec((k, tm), lambda i: (0, i)) if transposed else pl.BlockSpec((tm, k), row)
    return pl.pallas_call(
        functools.partial(_outproj_ln_kernel, alpha=alpha, transposed=transposed),
        grid=(n // tm,),
        in_specs=[o_spec, pl.BlockSpec((k, d), fixed),
                  pl.BlockSpec((tm, d), row), pl.BlockSpec((1, d), fixed), pl.BlockSpec((1, d), fixed)],
        out_specs=pl.BlockSpec((tm, d), row),
        out_shape=jax.ShapeDtypeStruct((n, d), F32),
        compiler_params=_params(("arbitrary",), 32),
        name=name,
    )(o, w, x, g, b)


def _split3(v):
    hi = v.astype(BF16)
    r1 = v - hi.astype(F32)
    mid = r1.astype(BF16)
    return hi, mid, (r1 - mid.astype(F32)).astype(BF16)


def _fox_proj_kernel(x_ref, wqt_ref, wk_ref, wvt_ref, qt_ref, k_ref, vt_ref):
    xb = x_ref[...].astype(BF16)
    qt_ref[...] = lax.dot_general(wqt_ref[...], xb, _NT, preferred_element_type=F32).astype(BF16)
    k_ref[...] = jnp.dot(xb, wk_ref[...], preferred_element_type=F32).astype(BF16)
    vt_ref[...] = lax.dot_general(wvt_ref[...], xb, _NT, preferred_element_type=F32).astype(BF16)


def _fox_proj(x, wqt, wk, wvt, tm=512):
    n, d = x.shape
    fixed = lambda i: (0, 0)
    return pl.pallas_call(
        _fox_proj_kernel,
        grid=(n // tm,),
        in_specs=[pl.BlockSpec((tm, d), lambda i: (i, 0)), pl.BlockSpec((d, d), fixed),
                  pl.BlockSpec((d, d), fixed), pl.BlockSpec((d, d), fixed)],
        out_specs=[pl.BlockSpec((d, tm), lambda i: (0, i)), pl.BlockSpec((tm, d), lambda i: (i, 0)),
                   pl.BlockSpec((d, tm), lambda i: (0, i))],
        out_shape=[jax.ShapeDtypeStruct((d, n), BF16), jax.ShapeDtypeStruct((n, d), BF16),
                   jax.ShapeDtypeStruct((d, n), BF16)],
        compiler_params=_params(("arbitrary",), 48),
        name="fox_proj",
    )(x, wqt, wk, wvt)


def _fox_gate_kernel(x_ref, wf_ref, bf_ref, tri_ref, sel_ref, cb_ref, carry_ref):
    @pl.when(pl.program_id(1) == 0)
    def _():
        carry_ref[...] = jnp.zeros_like(carry_ref)

    z = jnp.dot(x_ref[...].astype(BF16), wf_ref[...], preferred_element_type=F32) + bf_ref[...]
    logf = jnp.minimum(z, 0.0) - jnp.log(1.0 + jnp.exp(-jnp.abs(z)))
    tri = tri_ref[...]
    c = carry_ref[...]
    for part in _split3(logf):
        c = c + jnp.dot(tri, part, preferred_element_type=F32)
    carry_ref[...] = c[c.shape[0] - 1:, :]
    cb = None
    for j, part in enumerate(_split3(c * (-LOG2E))):
        term = jnp.dot(part, sel_ref[j], preferred_element_type=F32)
        cb = term if cb is None else cb + term
    cb_ref[...] = cb.astype(BF16)


def _fox_gate(x, wf, bf, bsz, t_len, tg=512):
    n, d = x.shape
    nt = t_len // tg
    tri = jnp.asarray(np.tril(np.ones((tg, tg), np.float32)), BF16)
    sel = np.zeros((3, LANES, d), np.float32)
    for h in range(FOX_HEADS):
        for j in range(3):
            sel[j, h, (h // 2) * LANES + 3 * (h % 2) + j] = 1.0
    return pl.pallas_call(
        _fox_gate_kernel,
        grid=(bsz, nt),
        in_specs=[pl.BlockSpec((tg, d), lambda b, i: (b * nt + i, 0)),
                  pl.BlockSpec((d, LANES), lambda b, i: (0, 0)),
                  pl.BlockSpec((1, LANES), lambda b, i: (0, 0)),
                  pl.BlockSpec((tg, tg), lambda b, i: (0, 0)),
                  pl.BlockSpec((3, LANES, d), lambda b, i: (0, 0, 0))],
        out_specs=pl.BlockSpec((tg, d), lambda b, i: (b * nt + i, 0)),
        out_shape=jax.ShapeDtypeStruct((n, d), BF16),
        scratch_shapes=[pltpu.VMEM((1, LANES), F32)],
        compiler_params=_params(("arbitrary", "arbitrary"), 32),
        name="fox_gate",
    )(x, wf, bf, tri, jnp.asarray(sel, BF16))


def _fox_attn_kernel(qt_ref, k_ref, cb_ref, vt_ref, ot_ref, s_ref, *, tq):
    qi = pl.program_id(2)
    qt = qt_ref[...].astype(F32)
    feat = lax.broadcasted_iota(jnp.int32, qt.shape, 0)
    rhs = []
    for h in range(2):
        own = (feat >= h * FOX_HEAD_DIM) & (feat < (h + 1) * FOX_HEAD_DIM)
        bias_rows = (feat >= 3 * h) & (feat < 3 * h + 3)
        rhs.append(jnp.concatenate([jnp.where(own, qt, 0.0).astype(BF16),
                                    jnp.where(bias_rows, 1.0, 0.0).astype(BF16)], axis=0))
    key_i = lax.broadcasted_iota(jnp.int32, (tq, tq), 0)
    qry_i = lax.broadcasted_iota(jnp.int32, (tq, tq), 1)

    def scores(j, slot):
        start = pl.multiple_of(j * tq, tq)
        kext = jnp.concatenate([k_ref[pl.ds(start, tq), :], cb_ref[pl.ds(start, tq), :]], axis=1)
        for h in range(2):
            s_ref[slot, h] = jnp.dot(kext, rhs[h], preferred_element_type=F32)

    def block(j, carry, slot, masked):
        start = pl.multiple_of(j * tq, tq)
        if not masked:
            scores(j + 1, 1 - slot)
        out = []
        for h in range(2):
            m, l, acc = carry[h]
            st = s_ref[slot, h]
            if masked:
                st = jnp.where(key_i <= qry_i, st, -jnp.inf)
            m_new = jnp.maximum(m, jnp.max(st, axis=0, keepdims=True))
            p = jnp.exp2(st - m_new)
            a = jnp.exp2(m - m_new)
            vt = vt_ref[h * FOX_HEAD_DIM:(h + 1) * FOX_HEAD_DIM, pl.ds(start, tq)]
            pv = jnp.dot(vt, p.astype(BF16), preferred_element_type=F32)
            out.append((m_new, a * l + jnp.sum(p, axis=0, keepdims=True), acc * a + pv))
        return tuple(out)

    init = (jnp.full((1, tq), -jnp.inf, F32), jnp.zeros((1, tq), F32), jnp.zeros((FOX_HEAD_DIM, tq), F32))
    scores(0, 0)
    carry = lax.fori_loop(
        0, qi // 2, lambda i, c: block(2 * i + 1, block(2 * i, c, 0, False), 1, False), (init, init))
    carry = lax.cond(
        qi % 2 == 0,
        lambda c: block(qi, c, 0, True),
        lambda c: block(qi, block(qi - 1, c, 0, False), 1, True),
        carry)
    for h in range(2):
        _, l, acc = carry[h]
        ot_ref[h * FOX_HEAD_DIM:(h + 1) * FOX_HEAD_DIM, :] = (acc / l).astype(ot_ref.dtype)


def _fox_attn(qt, k, cb, vt, bsz, t_len, tq=256):
    d, n = qt.shape
    n_pairs = d // LANES
    nq = t_len // tq
    return pl.pallas_call(
        functools.partial(_fox_attn_kernel, tq=tq),
        grid=(bsz, n_pairs, nq),
        in_specs=[pl.BlockSpec((LANES, tq), lambda b, p, i: (p, b * nq + i)),
                  pl.BlockSpec((t_len, LANES), lambda b, p, i: (b, p)),
                  pl.BlockSpec((t_len, LANES), lambda b, p, i: (b, p)),
                  pl.BlockSpec((LANES, t_len), lambda b, p, i: (p, b))],
        out_specs=pl.BlockSpec((LANES, tq), lambda b, p, i: (p, b * nq + i)),
        out_shape=jax.ShapeDtypeStruct((d, n), BF16),
        scratch_shapes=[pltpu.VMEM((2, 2, tq, tq), F32)],
        compiler_params=_params(("arbitrary", "arbitrary", "arbitrary"), 32),
        name="fox_attn",
    )(qt, k, cb, vt)


def _fox_layer(x, w_in, b_f, w_out, g, b, alpha, bsz, t_len):
    d = x.shape[1]
    wqt = (w_in[:, :d] * (FOX_HEAD_DIM ** -0.5 * LOG2E)).T.astype(BF16)
    wk = w_in[:, d:2 * d].astype(BF16)
    wvt = w_in[:, 2 * d:3 * d].T.astype(BF16)
    wf = jnp.zeros((d, LANES), F32).at[:, :FOX_HEADS].set(w_in[:, 3 * d:]).astype(BF16)
    bf = jnp.zeros((1, LANES), F32).at[0, :FOX_HEADS].set(b_f)
    qt, k, vt = _fox_proj(x, wqt, wk, wvt)
    cb = _fox_gate(x, wf, bf, bsz, t_len)
    ot = _fox_attn(qt, k, cb, vt, bsz, t_len)
    return _outproj_ln(ot, w_out.astype(BF16), x, g, b, alpha, "fox_out_ln", transposed=True)


def _hgrn_constants():
    c = HGRN_CHUNK
    r = np.arange(c)[:, None]
    j = np.arange(c)[None, :]
    blocks = [j <= r]
    masks = []
    levels = []
    half = c // 2
    while half >= 1:
        ref = (r // (2 * half)) * (2 * half) + half - 1
        upper = (r % (2 * half)) >= half
        if half < HGRN_SMALL_LEVEL:
            blocks.append(np.where(upper, (j > ref) & (j <= r), (j > r) & (j <= ref)))
        masks.append(((r // (2 * half)) == (j // (2 * half))) & upper & ((j % (2 * half)) < half))
        levels.append(half)
        half //= 2
    masks.append(r == j)
    wall = np.concatenate(blocks, axis=0).astype(np.float32)
    wall2 = np.concatenate([wall, wall], axis=1)
    return wall2, np.stack(masks).astype(np.float32), tuple(levels)


def _hgrn_kernel(q_ref, fl_ref, i_ref, g_ref, lb_ref, ng_ref, wall_ref, mask_ref, o_ref, s_ref, *, levels):
    c = HGRN_CHUNK

    @pl.when(pl.program_id(1) == 0)
    def _():
        s_ref[...] = jnp.zeros_like(s_ref)

    wall = wall_ref[...]
    rowi = lax.broadcasted_iota(jnp.int32, (c, HGRN_DIM), 0)
    e_pair = None
    for h in range(HGRN_HEADS):
        sl = slice(h * HGRN_DIM, (h + 1) * HGRN_DIM)
        if h % 2 == 0:
            sl2 = slice(h * HGRN_DIM, (h + 2) * HGRN_DIM)
            lb2 = lb_ref[:, sl2]
            f2 = lb2 + (1.0 - lb2) * _sigmoid(fl_ref[:, sl2])
            logf2 = jnp.log(f2)
            hi = logf2.astype(BF16)
            mid = (logf2 - hi.astype(F32)).astype(BF16)
            e_pair = jnp.dot(wall, jnp.concatenate([hi, mid], axis=0), preferred_element_type=F32)
        lane0 = (h % 2) * HGRN_DIM
        q = q_ref[:, sl].astype(F32)
        i_b = i_ref[:, sl]
        k = 1.0 - f2[:, lane0:lane0 + HGRN_DIM]
        e_all = e_pair[:, lane0:lane0 + HGRN_DIM]
        b = e_all[0:c]
        x_pre = jnp.exp(b)
        x_suf = jnp.exp(b[c - 1:c, :] - b)
        st = s_ref[h]
        o = lax.dot_general((q * x_pre).astype(BF16), st.astype(BF16), _NT, preferred_element_type=F32)
        upd = lax.dot_general(i_b, (k * x_suf).astype(BF16), _TN, preferred_element_type=F32)
        s_ref[h] = st * x_pre[c - 1:c, :] + upd
        a = jnp.zeros((c, c), F32)
        n_big = sum(half >= HGRN_SMALL_LEVEL for half in levels)
        for l, half in enumerate(levels):
            upper = (rowi & half) != 0
            if half >= HGRN_SMALL_LEVEL:
                parts = []
                for r0 in range(0, c, 2 * half):
                    ref = b[r0 + half - 1:r0 + half, :]
                    parts += [ref - b[r0:r0 + half], b[r0 + half:r0 + 2 * half] - ref]
                e_lvl = jnp.concatenate(parts, axis=0)
            else:
                e_lvl = e_all[(1 + l - n_big) * c:(2 + l - n_big) * c]
            z = (jnp.where(upper, q, k) * jnp.exp(e_lvl)).astype(BF16)
            a = a + lax.dot_general(z, z, _NT, preferred_element_type=F32) * mask_ref[l]
        a = a + lax.dot_general(q.astype(BF16), k.astype(BF16), _NT,
                                preferred_element_type=F32) * mask_ref[len(levels)]
        o = o + jnp.dot(a.astype(BF16), i_b, preferred_element_type=F32)
        o = o * lax.rsqrt(jnp.mean(o * o, axis=-1, keepdims=True) + RMS_EPS) * ng_ref[...]
        gate = g_ref[:, sl].astype(F32)
        o_ref[:, sl] = (o * (gate * _sigmoid(gate))).astype(o_ref.dtype)


def _hgrn_core(q, fl, i, g, lb, ng, bsz, t_len):
    n, d = q.shape
    c = HGRN_CHUNK
    nc = t_len // c
    wall2, masks, levels = _hgrn_constants()
    row = lambda b, j: (b * nc + j, 0)
    fixed2 = lambda b, j: (0, 0)
    return pl.pallas_call(
        functools.partial(_hgrn_kernel, levels=levels),
        grid=(bsz, nc),
        in_specs=[pl.BlockSpec((c, d), row), pl.BlockSpec((c, d), row), pl.BlockSpec((c, d), row),
                  pl.BlockSpec((c, d), row), pl.BlockSpec((1, d), fixed2), pl.BlockSpec((1, HGRN_DIM), fixed2),
                  pl.BlockSpec(wall2.shape, fixed2), pl.BlockSpec(masks.shape, lambda b, j: (0, 0, 0))],
        out_specs=pl.BlockSpec((c, d), row),
        out_shape=jax.ShapeDtypeStruct((n, d), BF16),
        scratch_shapes=[pltpu.VMEM((HGRN_HEADS, HGRN_DIM, HGRN_DIM), F32)],
        compiler_params=_params(("arbitrary", "arbitrary"), 32),
        name="hgrn_core",
    )(q, fl, i, g, lb, ng, jnp.asarray(wall2, BF16), jnp.asarray(masks, F32))


def _hgrn_layer(x, w_in, lower_bound, norm_g, w_out, g, b, alpha, bsz, t_len):
    d = x.shape[1]
    wb = w_in.astype(BF16)
    q, fl, i, gate = _proj(x, [wb[:, :d], wb[:, d:2 * d], wb[:, 2 * d:3 * d], wb[:, 3 * d:]],
                           [BF16, F32, BF16, BF16], "hgrn_proj")
    o = _hgrn_core(q, fl, i, gate, lower_bound.reshape(1, d), norm_g.reshape(1, HGRN_DIM), bsz, t_len)
    return _outproj_ln(o, w_out.astype(BF16), x, g, b, alpha, "hgrn_out_ln")


def _conv_kernel(x_ref, win_ref, cw_ref, wout_ref, g_ref, b_ref, y_ref, zbuf, *, alpha, tm):
    d = x_ref.shape[1]

    @pl.when(pl.program_id(1) == 0)
    def _():
        zbuf[0:8, :] = jnp.zeros((8, d), F32)

    x = x_ref[...]
    p = jnp.dot(x.astype(BF16), win_ref[...], preferred_element_type=F32)
    z = p[:, d:2 * d] * p[:, 2 * d:]
    zbuf[8:8 + tm, :] = z
    y = cw_ref[2:3, :] * z + cw_ref[1:2, :] * zbuf[7:7 + tm, :] + cw_ref[0:1, :] * zbuf[6:6 + tm, :]
    zbuf[0:8, :] = z[tm - 8:, :]
    mixed = jnp.dot((p[:, :d] * y).astype(BF16), wout_ref[...], preferred_element_type=F32)
    y_ref[...] = _layer_norm(alpha * x + mixed, g_ref[...], b_ref[...])


def _conv_layer(x, w_in, conv_w, w_out, g, b, alpha, bsz, t_len, tm=512):
    n, d = x.shape
    nt = t_len // tm
    row = lambda bb, i: (bb * nt + i, 0)
    fixed = lambda bb, i: (0, 0)
    return pl.pallas_call(
        functools.partial(_conv_kernel, alpha=alpha, tm=tm),
        grid=(bsz, nt),
        in_specs=[pl.BlockSpec((tm, d), row), pl.BlockSpec((d, 3 * d), fixed), pl.BlockSpec(conv_w.shape, fixed),
                  pl.BlockSpec((d, d), fixed), pl.BlockSpec((1, d), fixed), pl.BlockSpec((1, d), fixed)],
        out_specs=pl.BlockSpec((tm, d), row),
        out_shape=jax.ShapeDtypeStruct((n, d), F32),
        scratch_shapes=[pltpu.VMEM((tm + 8, d), F32)],
        compiler_params=_params(("arbitrary", "arbitrary"), 48),
        name="conv_layer",
    )(x, w_in.astype(BF16), conv_w, w_out.astype(BF16), g, b)


def _store_token_tiles(xt_ref, v, row0=0):
    rows = v.shape[0]
    for c in range(TOKEN_TILE_ROWS):
        xt_ref[pl.ds(row0 + c, rows, stride=TOKEN_TILE_ROWS), :] = v[:, c * LANES:(c + 1) * LANES]


def _load_token_tiles(xt_ref, rows, row0=0):
    return jnp.concatenate([xt_ref[pl.ds(row0 + c, rows, stride=TOKEN_TILE_ROWS), :]
                            for c in range(TOKEN_TILE_ROWS)], axis=1)


def _router_kernel(x_ref, wh_ref, wl_ref, b_ref, xt_ref, meta_ref, oh_ref, cnt_ref):
    x = x_ref[...]
    xh = x.astype(BF16)
    xl = (x - xh.astype(F32)).astype(BF16)
    wh = wh_ref[...]
    logits = (jnp.dot(xh, wh, preferred_element_type=F32) + jnp.dot(xh, wl_ref[...], preferred_element_type=F32)
              + jnp.dot(xl, wh, preferred_element_type=F32)) + b_ref[...]
    lane = lax.broadcasted_iota(jnp.int32, logits.shape, 1)
    lane_f = lane.astype(F32)
    neg = -jnp.inf
    far = float(LANES)
    is_g = (lane >= ROUTER_GROUP_LANE0) & (lane < ROUTER_GROUP_LANE0 + N_GROUPS)
    gl = jnp.where(is_g, logits, neg)
    gmax = jnp.max(gl, axis=-1, keepdims=True)
    g_sel = jnp.min(jnp.where(gl == gmax, lane_f, far), axis=-1, keepdims=True) - float(ROUTER_GROUP_LANE0)
    grp_w = 1.0 / jnp.sum(jnp.where(is_g, jnp.exp(logits - gmax), 0.0), axis=-1, keepdims=True)
    in_g = (lane < N_EXPERTS) & ((lane // EXPERTS_PER_GROUP).astype(F32) == g_sel)
    el = jnp.where(in_g, logits, neg)
    t1 = jnp.max(el, axis=-1, keepdims=True)
    i1 = jnp.min(jnp.where(el == t1, lane_f, far), axis=-1, keepdims=True)
    el2 = jnp.where(lane_f == i1, neg, el)
    t2 = jnp.max(el2, axis=-1, keepdims=True)
    i2 = jnp.min(jnp.where(el2 == t2, lane_f, far), axis=-1, keepdims=True)
    e2 = jnp.exp(t2 - t1)
    w1 = grp_w / (1.0 + e2)
    w2 = w1 * e2
    first_lo = i1 < i2
    lo = jnp.where(first_lo, i1, i2) - g_sel * EXPERTS_PER_GROUP
    hi = jnp.where(first_lo, i2, i1) - g_sel * EXPERTS_PER_GROUP
    pair = lo * (EXPERTS_PER_GROUP - 1.0) - lo * (lo - 1.0) * 0.5 + (hi - lo - 1.0)
    onehot = jnp.where(lane_f == g_sel * MOE_PAIRS + pair, 1.0, 0.0)
    meta_ref[...] = (jnp.where(lane == 0, jnp.where(first_lo, w1, w2), 0.0)
                     + jnp.where(lane == 1, jnp.where(first_lo, w2, w1), 0.0))
    _store_token_tiles(xt_ref, x)
    oh_ref[...] = onehot.astype(BF16)

    @pl.when(pl.program_id(0) == 0)
    def _():
        cnt_ref[...] = jnp.zeros_like(cnt_ref)

    cnt_ref[...] += jnp.sum(onehot, axis=0, keepdims=True)


def _router(x, w_group, b_group, w_expert, b_expert, tm=MOE_TOK_TILE):
    n, d = x.shape
    w = jnp.zeros((d, LANES), F32).at[:, :N_EXPERTS].set(w_expert)
    w = w.at[:, ROUTER_GROUP_LANE0:ROUTER_GROUP_LANE0 + N_GROUPS].set(w_group)
    bias = jnp.zeros((1, LANES), F32).at[0, :N_EXPERTS].set(b_expert)
    bias = bias.at[0, ROUTER_GROUP_LANE0:ROUTER_GROUP_LANE0 + N_GROUPS].set(b_group)
    wh = w.astype(BF16)
    wl = (w - wh.astype(F32)).astype(BF16)
    row = lambda i: (i, 0)
    fixed = lambda i: (0, 0)
    return pl.pallas_call(
        _router_kernel,
        grid=(n // tm,),
        in_specs=[pl.BlockSpec((tm, d), row), pl.BlockSpec((d, LANES), fixed), pl.BlockSpec((d, LANES), fixed),
                  pl.BlockSpec((1, LANES), fixed)],
        out_specs=[pl.BlockSpec((tm * TOKEN_TILE_ROWS, LANES), row), pl.BlockSpec((tm, LANES), row),
                   pl.BlockSpec((tm, LANES), row), pl.BlockSpec((8, LANES), fixed)],
        out_shape=[jax.ShapeDtypeStruct((n * TOKEN_TILE_ROWS, LANES), F32), jax.ShapeDtypeStruct((n, LANES), F32),
                   jax.ShapeDtypeStruct((n, LANES), BF16), jax.ShapeDtypeStruct((8, LANES), F32)],
        compiler_params=_params(("arbitrary",), 32),
        name="moe_router",
    )(x, wh, wl, bias)


def _moe_pos_kernel(oh_ref, base_ref, lt_ref, pos_ref, carry_ref):
    @pl.when(pl.program_id(0) == 0)
    def _():
        carry_ref[...] = jnp.zeros_like(carry_ref)

    oh = oh_ref[...]
    rank = jnp.dot(lt_ref[...], oh, preferred_element_type=F32) + carry_ref[...] + base_ref[...]
    ohf = oh.astype(F32)
    val = ohf * rank
    carry_ref[...] += jnp.sum(ohf, axis=0, keepdims=True)
    hi = jnp.floor(val * (1.0 / 256.0))
    lo = val - 256.0 * hi
    ones = jnp.ones((8, LANES), BF16)
    pos = (256.0 * lax.dot_general(ones, hi.astype(BF16), _NT, preferred_element_type=F32)
           + lax.dot_general(ones, lo.astype(BF16), _NT, preferred_element_type=F32))
    pos_ref[0] = pos[0:1].astype(jnp.int32)


def _moe_pos(onehot, base, tm=MOE_TOK_TILE):
    n = onehot.shape[0]
    lt = jnp.asarray(np.tril(np.ones((tm, tm), np.float32), -1), BF16)
    return pl.pallas_call(
        _moe_pos_kernel,
        grid=(n // tm,),
        in_specs=[pl.BlockSpec((tm, LANES), lambda i: (i, 0)), pl.BlockSpec((1, LANES), lambda i: (0, 0)),
                  pl.BlockSpec((tm, tm), lambda i: (0, 0))],
        out_specs=pl.BlockSpec((1, 1, tm), lambda i: (i, 0, 0)),
        out_shape=jax.ShapeDtypeStruct((n // tm, 1, tm), jnp.int32),
        scratch_shapes=[pltpu.VMEM((1, LANES), F32)],
        compiler_params=_params(("arbitrary",), 32),
        name="moe_pos",
    )(onehot, base, lt)


def _tile_copy(src, src_tok, dst, dst_tok, sem):
    rows = TOKEN_TILE_ROWS
    return pltpu.make_async_copy(src.at[pl.ds(pl.multiple_of(src_tok * rows, rows), rows)],
                                 dst.at[pl.ds(pl.multiple_of(dst_tok * rows, rows), rows)], sem)


def _meta_copy(src, src_tok, dst, dst_tok, sem):
    return pltpu.make_async_copy(src.at[pl.ds(src_tok, 1)], dst.at[pl.ds(dst_tok, 1)], sem)


def _moe_scatter_kernel(pos_ref, xt_ref, meta_ref, xs_in_ref, xm_in_ref, xs_ref, xm_ref, sem, *, tm):
    del xs_in_ref, xm_in_ref

    def start(r, c):
        p = pos_ref[0, 0, r]
        _tile_copy(xt_ref, r, xs_ref, p, sem.at[0]).start(priority=0)
        _meta_copy(meta_ref, r, xm_ref, p, sem.at[1]).start(priority=1)
        return c

    lax.fori_loop(0, tm, start, 0, unroll=8)
    pltpu.make_async_copy(xt_ref, xs_ref.at[pl.ds(0, tm * TOKEN_TILE_ROWS)], sem.at[0]).wait()
    pltpu.make_async_copy(meta_ref, xm_ref.at[pl.ds(0, tm)], sem.at[1]).wait()


def _moe_scatter(pos, xt, meta, xs_buf, xm_buf, tm=MOE_TOK_TILE):
    n = meta.shape[0]
    return pl.pallas_call(
        functools.partial(_moe_scatter_kernel, tm=tm),
        grid=(n // tm,),
        in_specs=[pl.BlockSpec((1, 1, tm), lambda i: (i, 0, 0), memory_space=pltpu.SMEM),
                  pl.BlockSpec((tm * TOKEN_TILE_ROWS, LANES), lambda i: (i, 0)),
                  pl.BlockSpec((tm, LANES), lambda i: (i, 0)),
                  pl.BlockSpec(memory_space=pl.ANY), pl.BlockSpec(memory_space=pl.ANY)],
        out_specs=[pl.BlockSpec(memory_space=pl.ANY), pl.BlockSpec(memory_space=pl.ANY)],
        out_shape=[jax.ShapeDtypeStruct(xs_buf.shape, xs_buf.dtype), jax.ShapeDtypeStruct(xm_buf.shape, xm_buf.dtype)],
        scratch_shapes=[pltpu.SemaphoreType.DMA((2,))],
        input_output_aliases={3: 0, 4: 1},
        compiler_params=_params(("arbitrary",), 32),
        name="moe_scatter",
    )(pos, xt, meta, xs_buf, xm_buf)


def _moe_experts_kernel(grp_ref, e1_ref, e2_ref, nused_ref, xs_ref, xm_ref, wg_ref, wu_ref, wd_ref, ys_in_ref, ys_ref):
    del grp_ref, ys_in_ref
    i = pl.program_id(0)
    tr = xm_ref.shape[0]

    @pl.when(i < nused_ref[0])
    def _():
        x = _load_token_tiles(xs_ref, tr).astype(BF16)
        meta = xm_ref[...]
        y = None
        for k, e_ref in enumerate((e1_ref, e2_ref)):
            e = e_ref[i]
            hg = jnp.dot(x, wg_ref[0, e], preferred_element_type=F32)
            hu = jnp.dot(x, wu_ref[0, e], preferred_element_type=F32)
            h = hg * _sigmoid(hg) * hu * meta[:, k:k + 1]
            yk = jnp.dot(h.astype(BF16), wd_ref[0, e], preferred_element_type=F32)
            y = yk if y is None else y + yk
        _store_token_tiles(ys_ref, y)


def _moe_experts(xs_buf, xm_buf, ys_buf, tile_grp, tile_e1, tile_e2, n_used, w_gate, w_up, w_down, layer,
                 tr=MOE_ROW_TILE):
    r_max = xm_buf.shape[0]
    d, f = w_gate.shape[-2:]
    assert d == TOKEN_TILE_ROWS * LANES
    tile_rows = tr * TOKEN_TILE_ROWS
    epg = EXPERTS_PER_GROUP
    used = lambda i, grp, e1, e2, nu: (jnp.minimum(i, nu[0] - 1), 0)
    wmap = lambda i, grp, e1, e2, nu: (layer * N_GROUPS + grp[i], 0, 0, 0)
    grid_spec = pltpu.PrefetchScalarGridSpec(
        num_scalar_prefetch=4,
        grid=(r_max // tr,),
        in_specs=[pl.BlockSpec((tile_rows, LANES), used), pl.BlockSpec((tr, LANES), used),
                  pl.BlockSpec((1, epg, d, f), wmap), pl.BlockSpec((1, epg, d, f), wmap),
                  pl.BlockSpec((1, epg, f, d), wmap), pl.BlockSpec(memory_space=pl.ANY)],
        out_specs=pl.BlockSpec((tile_rows, LANES), used),
    )
    return pl.pallas_call(
        _moe_experts_kernel,
        grid_spec=grid_spec,
        out_shape=jax.ShapeDtypeStruct(ys_buf.shape, ys_buf.dtype),
        input_output_aliases={9: 0},
        compiler_params=_params(("arbitrary",), 48),
        name="moe_experts",
    )(tile_grp, tile_e1, tile_e2, n_used, xs_buf, xm_buf, w_gate, w_up, w_down, ys_buf)


def _moe_combine_ln_kernel(pos_ref, nxt_ref, x_ref, ys_ref, g_ref, b_ref, y_ref, buf_ref, sem, *, alpha, tm):
    i = pl.program_id(0)
    slot = i % 2

    def gather(p_ref, s):
        def start(r, c):
            _tile_copy(ys_ref, p_ref[0, 0, r], buf_ref, s * tm + r, sem.at[s]).start()
            return c
        lax.fori_loop(0, tm, start, 0, unroll=8)

    @pl.when(i == 0)
    def _():
        gather(pos_ref, 0)

    @pl.when(i + 1 < pl.num_programs(0))
    def _():
        gather(nxt_ref, 1 - slot)

    base = pl.multiple_of(slot * (tm * TOKEN_TILE_ROWS), tm * TOKEN_TILE_ROWS)
    slot_rows = pl.ds(base, tm * TOKEN_TILE_ROWS)
    pltpu.make_async_copy(ys_ref.at[pl.ds(0, tm * TOKEN_TILE_ROWS)], buf_ref.at[slot_rows], sem.at[slot]).wait()
    ffn = _load_token_tiles(buf_ref, tm, base)
    y_ref[...] = _layer_norm(alpha * x_ref[...] + ffn, g_ref[...], b_ref[...])


def _moe_combine_ln(pos, x, ys_buf, g, b, alpha, tm=MOE_TOK_TILE):
    n, d = x.shape
    nt = n // tm
    row = lambda i: (i, 0)
    fixed = lambda i: (0, 0)
    return pl.pallas_call(
        functools.partial(_moe_combine_ln_kernel, alpha=alpha, tm=tm),
        grid=(nt,),
        in_specs=[pl.BlockSpec((1, 1, tm), lambda i: (i, 0, 0), memory_space=pltpu.SMEM),
                  pl.BlockSpec((1, 1, tm), lambda i: (jnp.minimum(i + 1, nt - 1), 0, 0), memory_space=pltpu.SMEM),
                  pl.BlockSpec((tm, d), row), pl.BlockSpec(memory_space=pl.ANY),
                  pl.BlockSpec((1, d), fixed), pl.BlockSpec((1, d), fixed)],
        out_specs=pl.BlockSpec((tm, d), row),
        out_shape=jax.ShapeDtypeStruct((n, d), F32),
        scratch_shapes=[pltpu.VMEM((2 * tm * TOKEN_TILE_ROWS, LANES), F32), pltpu.SemaphoreType.DMA((2,))],
        compiler_params=_params(("arbitrary",), 32),
        name="moe_combine_ln",
    )(pos, pos, x, ys_buf, g, b)


def _bucket_experts():
    lo, hi = [], []
    for a in range(EXPERTS_PER_GROUP):
        for c in range(a + 1, EXPERTS_PER_GROUP):
            lo.append(a)
            hi.append(c)
    grp = np.repeat(np.arange(N_GROUPS), MOE_PAIRS).astype(np.int32)
    return grp, np.tile(lo, N_GROUPS).astype(np.int32), np.tile(hi, N_GROUPS).astype(np.int32)


def _moe_layer(x, bufs, w_group, b_group, w_expert, b_expert, w_gate, w_up, w_down, layer, g, b, alpha):
    tr = MOE_ROW_TILE
    xs_buf, xm_buf, ys_buf = bufs
    xt, meta, onehot, cnt = _router(x, w_group, b_group, w_expert, b_expert)
    counts = cnt[0, :MOE_BUCKETS].astype(jnp.int32)
    padded = (counts + (tr - 1)) // tr * tr
    ends = jnp.cumsum(padded)
    base = jnp.zeros((1, LANES), F32).at[0, :MOE_BUCKETS].set((ends - padded).astype(F32))
    n_used = ends[-1] // tr
    n_tiles = xm_buf.shape[0] // tr
    tile = jnp.minimum(jnp.arange(n_tiles, dtype=jnp.int32), n_used - 1)
    tile_bkt = jnp.sum((ends[None, :] <= (tile * tr)[:, None]).astype(jnp.int32), axis=1)
    tile_bkt = jnp.minimum(tile_bkt, MOE_BUCKETS - 1)
    b_grp, b_lo, b_hi = (jnp.asarray(t)[tile_bkt] for t in _bucket_experts())
    pos = _moe_pos(onehot, base)
    xs_buf, xm_buf = _moe_scatter(pos, xt, meta, xs_buf, xm_buf)
    ys_buf = _moe_experts(xs_buf, xm_buf, ys_buf, b_grp, b_lo, b_hi, n_used.reshape(1).astype(jnp.int32),
                          w_gate, w_up, w_down, layer)
    return _moe_combine_ln(pos, x, ys_buf, g, b, alpha), (xs_buf, xm_buf, ys_buf)


def kernel(x, ln_mix_g, ln_mix_b, ln_ffn_g, ln_ffn_b, fox_w_in, fox_b_f, fox_w_out, hgrn_w_in, hgrn_lb_logits, hgrn_norm_g, hgrn_w_out, conv_w_in, conv_w, conv_w_out, moe_w_group, moe_b_group, moe_w_expert, moe_b_expert, moe_w_gate, moe_w_up, moe_w_down):
    bsz, t_len, d = x.shape
    depth = ln_mix_g.shape[0]
    alpha = float((2 * depth) ** 0.25)
    assert d == FOX_HEADS * FOX_HEAD_DIM == HGRN_HEADS * HGRN_DIM
    assert t_len % 512 == 0 and (bsz * t_len) % 1024 == 0

    lb_prob = jax.nn.softmax(hgrn_lb_logits.astype(F32), axis=0)
    lower_bounds = jnp.cumsum(lb_prob, axis=0) - lb_prob[0]

    h = x.reshape(bsz * t_len, d)
    r_max = bsz * t_len + MOE_BUCKETS * MOE_ROW_TILE
    by_group = lambda w: w.astype(BF16).reshape((depth * N_GROUPS, EXPERTS_PER_GROUP) + w.shape[2:])
    moe_w_gate, moe_w_up, moe_w_down = by_group(moe_w_gate), by_group(moe_w_up), by_group(moe_w_down)
    bufs = (jnp.zeros((r_max * TOKEN_TILE_ROWS, LANES), F32), jnp.zeros((r_max, LANES), F32),
            jnp.zeros((r_max * TOKEN_TILE_ROWS, LANES), F32))
    for layer in range(depth):
        kind, j = layer % 3, layer // 3
        g_mix, b_mix = ln_mix_g[layer].reshape(1, d), ln_mix_b[layer].reshape(1, d)
        if kind == 0:
            h = _fox_layer(h, fox_w_in[j], fox_b_f[j], fox_w_out[j], g_mix, b_mix, alpha, bsz, t_len)
        elif kind == 1:
            h = _hgrn_layer(h, hgrn_w_in[j], lower_bounds[layer], hgrn_norm_g[j], hgrn_w_out[j],
                            g_mix, b_mix, alpha, bsz, t_len)
        else:
            h = _conv_layer(h, conv_w_in[j], conv_w[j], conv_w_out[j], g_mix, b_mix, alpha, bsz, t_len)
        h, bufs = _moe_layer(
            h, bufs, moe_w_group[layer], moe_b_group[layer], moe_w_expert[layer], moe_b_expert[layer],
            moe_w_gate, moe_w_up, moe_w_down, layer,
            ln_ffn_g[layer].reshape(1, d), ln_ffn_b[layer].reshape(1, d), alpha)
    return h.reshape(bsz, t_len, d)
```

```python
import functools

import numpy as np
import jax
import jax.numpy as jnp
from jax import lax
from jax.experimental import pallas as pl
from jax.experimental.pallas import tpu as pltpu

F32 = jnp.float32
BF16 = jnp.bfloat16

FOX_HEADS = 16
FOX_HEAD_DIM = 64
FOX_PAIRS_PER_STEP = 2
HGRN_HEADS = 8
HGRN_DIM = 128
HGRN_CHUNK = 128
HGRN_SMALL_LEVEL = 8
N_GROUPS = 4
EXPERTS_PER_GROUP = 8
N_EXPERTS = N_GROUPS * EXPERTS_PER_GROUP
LN_EPS = 1e-5
RMS_EPS = 1e-6
LOG2E = 1.4426950408889634
LANES = 128
ROUTER_GROUP_LANE0 = N_EXPERTS
MOE_PAIRS = EXPERTS_PER_GROUP * (EXPERTS_PER_GROUP - 1) // 2
MOE_BUCKETS = N_GROUPS * MOE_PAIRS
MOE_ROW_TILE = 128
MOE_TOK_TILE = 256
TOKEN_TILE_ROWS = 8

_NT = (((1,), (1,)), ((), ()))
_TN = (((0,), (0,)), ((), ()))


def _params(semantics, vmem_mb):
    return pltpu.CompilerParams(dimension_semantics=semantics,
                                vmem_limit_bytes=vmem_mb * 1024 * 1024)


def _layer_norm(v, g, b):
    mu = jnp.mean(v, axis=-1, keepdims=True)
    d = v - mu
    var = jnp.mean(d * d, axis=-1, keepdims=True)
    return d * lax.rsqrt(var + LN_EPS) * g + b


def _sigmoid(v):
    return 1.0 / (1.0 + jnp.exp(-v))


def _proj_kernel(x_ref, *refs, n_out):
    xb = x_ref[...].astype(BF16)
    for w_ref, o_ref in zip(refs[:n_out], refs[n_out:]):
        o_ref[...] = jnp.dot(xb, w_ref[...], preferred_element_type=F32).astype(o_ref.dtype)


def _proj(x, ws, dtypes, name, tm=512):
    n, k = x.shape
    return pl.pallas_call(
        functools.partial(_proj_kernel, n_out=len(ws)),
        grid=(n // tm,),
        in_specs=[pl.BlockSpec((tm, k), lambda i: (i, 0))]
        + [pl.BlockSpec(w.shape, lambda i: (0, 0)) for w in ws],
        out_specs=[pl.BlockSpec((tm, w.shape[1]), lambda i: (i, 0)) for w in ws],
        out_shape=[jax.ShapeDtypeStruct((n, w.shape[1]), dt) for w, dt in zip(ws, dtypes)],
        compiler_params=_params(("arbitrary",), 48),
        name=name,
    )(x, *ws)


def _outproj_ln_kernel(o_ref, w_ref, x_ref, g_ref, b_ref, y_ref, *, alpha, transposed):
    dims = _TN if transposed else (((1,), (0,)), ((), ()))
    mixed = lax.dot_general(o_ref[...], w_ref[...], dims, preferred_element_type=F32)
    y_ref[...] = _layer_norm(alpha * x_ref[...] + mixed, g_ref[...], b_ref[...])


def _outproj_ln(o, w, x, g, b, alpha, name, tm=512, transposed=False):
    n, d = x.shape
    k = w.shape[0]
    row = lambda i: (i, 0)
    fixed = lambda i: (0, 0)
    o_spec = pl.BlockSpec((k, tm), lambda i: (0, i)) if transposed else pl.BlockSpec((tm, k), row)
    return pl.pallas_call(
        functools.partial(_outproj_ln_kernel, alpha=alpha, transposed=transposed),
        grid=(n // tm,),
        in_specs=[o_spec, pl.BlockSpec((k, d), fixed),
                  pl.BlockSpec((tm, d), row), pl.BlockSpec((1, d), fixed), pl.BlockSpec((1, d), fixed)],
        out_specs=pl.BlockSpec((tm, d), row),
        out_shape=jax.ShapeDtypeStruct((n, d), F32),
        compiler_params=_params(("arbitrary",), 32),
        name=name,
    )(o, w, x, g, b)


def _split3(v):
    hi = v.astype(BF16)
    r1 = v - hi.astype(F32)
    mid = r1.astype(BF16)
    return hi, mid, (r1 - mid.astype(F32)).astype(BF16)


def _fox_proj_kernel(x_ref, wqt_ref, wk_ref, wvt_ref, qt_ref, k_ref, vt_ref):
    xb = x_ref[...].astype(BF16)
    qt_ref[...] = lax.dot_general(wqt_ref[...], xb, _NT, preferred_element_type=F32).astype(BF16)
    k_ref[...] = jnp.dot(xb, wk_ref[...], preferred_element_type=F32).astype(BF16)
    vt_ref[...] = lax.dot_general(wvt_ref[...], xb, _NT, preferred_element_type=F32).astype(BF16)


def _fox_proj(x, wqt, wk, wvt, tm=512):
    n, d = x.shape
    fixed = lambda i: (0, 0)
    return pl.pallas_call(
        _fox_proj_kernel,
        grid=(n // tm,),
        in_specs=[pl.BlockSpec((tm, d), lambda i: (i, 0)), pl.BlockSpec((d, d), fixed),
                  pl.BlockSpec((d, d), fixed), pl.BlockSpec((d, d), fixed)],
        out_specs=[pl.BlockSpec((d, tm), lambda i: (0, i)), pl.BlockSpec((tm, d), lambda i: (i, 0)),
                   pl.BlockSpec((d, tm), lambda i: (0, i))],
        out_shape=[jax.ShapeDtypeStruct((d, n), BF16), jax.ShapeDtypeStruct((n, d), BF16),
                   jax.ShapeDtypeStruct((d, n), BF16)],
        compiler_params=_params(("arbitrary",), 48),
        name="fox_proj",
    )(x, wqt, wk, wvt)


def _fox_gate_kernel(x_ref, wf_ref, bf_ref, tri_ref, sel_ref, cb_ref, carry_ref):
    @pl.when(pl.program_id(1) == 0)
    def _():
        carry_ref[...] = jnp.zeros_like(carry_ref)

    z = jnp.dot(x_ref[...].astype(BF16), wf_ref[...], preferred_element_type=F32) + bf_ref[...]
    logf = jnp.minimum(z, 0.0) - jnp.log(1.0 + jnp.exp(-jnp.abs(z)))
    tri = tri_ref[...]
    c = carry_ref[...]
    for part in _split3(logf):
        c = c + jnp.dot(tri, part, preferred_element_type=F32)
    carry_ref[...] = c[c.shape[0] - 1:, :]
    cb = None
    for j, part in enumerate(_split3(c * (-LOG2E))):
        term = jnp.dot(part, sel_ref[j], preferred_element_type=F32)
        cb = term if cb is None else cb + term
    cb_ref[...] = cb.astype(BF16)


def _fox_gate(x, wf, bf, bsz, t_len, tg=512):
    n, d = x.shape
    nt = t_len // tg
    tri = jnp.asarray(np.tril(np.ones((tg, tg), np.float32)), BF16)
    sel = np.zeros((3, LANES, d), np.float32)
    for h in range(FOX_HEADS):
        for j in range(3):
            sel[j, h, (h // 2) * LANES + 3 * (h % 2) + j] = 1.0
    return pl.pallas_call(
        _fox_gate_kernel,
        grid=(bsz, nt),
        in_specs=[pl.BlockSpec((tg, d), lambda b, i: (b * nt + i, 0)),
                  pl.BlockSpec((d, LANES), lambda b, i: (0, 0)),
                  pl.BlockSpec((1, LANES), lambda b, i: (0, 0)),
                  pl.BlockSpec((tg, tg), lambda b, i: (0, 0)),
                  pl.BlockSpec((3, LANES, d), lambda b, i: (0, 0, 0))],
        out_specs=pl.BlockSpec((tg, d), lambda b, i: (b * nt + i, 0)),
        out_shape=jax.ShapeDtypeStruct((n, d), BF16),
        scratch_shapes=[pltpu.VMEM((1, LANES), F32)],
        compiler_params=_params(("arbitrary", "arbitrary"), 32),
        name="fox_gate",
    )(x, wf, bf, tri, jnp.asarray(sel, BF16))


def _fox_attn_kernel(qt_ref, k_ref, cb_ref, vt_ref, ot_ref, s_ref, *, tq):
    qi = pl.program_id(2)
    n_heads = 2 * FOX_PAIRS_PER_STEP
    feat = lax.broadcasted_iota(jnp.int32, (LANES, tq), 0)
    rhs = []
    for h in range(n_heads):
        pair, hh = divmod(h, 2)
        qt = qt_ref[pair * LANES:(pair + 1) * LANES, :].astype(F32)
        own = (feat >= hh * FOX_HEAD_DIM) & (feat < (hh + 1) * FOX_HEAD_DIM)
        bias_rows = (feat >= 3 * hh) & (feat < 3 * hh + 3)
        rhs.append(jnp.concatenate([jnp.where(own, qt, 0.0).astype(BF16),
                                    jnp.where(bias_rows, 1.0, 0.0).astype(BF16)], axis=0))
    key_i = lax.broadcasted_iota(jnp.int32, (tq, tq), 0)
    qry_i = lax.broadcasted_iota(jnp.int32, (tq, tq), 1)
    ones_rows = jnp.ones((16, tq), BF16)

    def scores(j, slot):
        start = pl.multiple_of(j * tq, tq)
        for pair in range(FOX_PAIRS_PER_STEP):
            lanes = slice(pair * LANES, (pair + 1) * LANES)
            kext = jnp.concatenate([k_ref[pl.ds(start, tq), lanes], cb_ref[pl.ds(start, tq), lanes]], axis=1)
            for h in (2 * pair, 2 * pair + 1):
                s_ref[slot, h] = jnp.dot(kext, rhs[h], preferred_element_type=F32)

    def block(j, carry, slot, masked):
        start = pl.multiple_of(j * tq, tq)
        if not masked:
            scores(j + 1, 1 - slot)
        out = []
        for h in range(n_heads):
            m, l, acc = carry[h]
            st = s_ref[slot, h]
            if masked:
                st = jnp.where(key_i <= qry_i, st, -jnp.inf)
            m_new = jnp.maximum(m, jnp.max(st, axis=0, keepdims=True))
            p = jnp.exp2(st - m_new)
            a = jnp.exp2(m - m_new)
            vt = vt_ref[h * FOX_HEAD_DIM:(h + 1) * FOX_HEAD_DIM, pl.ds(start, tq)]
            pv = jnp.dot(jnp.concatenate([vt, ones_rows], axis=0), p.astype(BF16), preferred_element_type=F32)
            out.append((m_new, a * l + pv[FOX_HEAD_DIM:FOX_HEAD_DIM + 1], acc * a + pv[:FOX_HEAD_DIM]))
        return tuple(out)

    init = (jnp.full((1, tq), -jnp.inf, F32), jnp.zeros((1, tq), F32), jnp.zeros((FOX_HEAD_DIM, tq), F32))
    scores(0, 0)
    carry = lax.fori_loop(
        0, qi // 2, lambda i, c: block(2 * i + 1, block(2 * i, c, 0, False), 1, False), (init,) * n_heads)
    carry = lax.cond(
        qi % 2 == 0,
        lambda c: block(qi, c, 0, True),
        lambda c: block(qi, block(qi - 1, c, 0, False), 1, True),
        carry)
    for h in range(n_heads):
        _, l, acc = carry[h]
        ot_ref[h * FOX_HEAD_DIM:(h + 1) * FOX_HEAD_DIM, :] = (acc / l).astype(ot_ref.dtype)


def _fox_attn(qt, k, cb, vt, bsz, t_len, tq=256):
    d, n = qt.shape
    w = FOX_PAIRS_PER_STEP * LANES
    nq = t_len // tq
    return pl.pallas_call(
        functools.partial(_fox_attn_kernel, tq=tq),
        grid=(bsz, d // w, nq),
        in_specs=[pl.BlockSpec((w, tq), lambda b, p, i: (p, b * nq + i)),
                  pl.BlockSpec((t_len, w), lambda b, p, i: (b, p)),
                  pl.BlockSpec((t_len, w), lambda b, p, i: (b, p)),
                  pl.BlockSpec((w, t_len), lambda b, p, i: (p, b))],
        out_specs=pl.BlockSpec((w, tq), lambda b, p, i: (p, b * nq + i)),
        out_shape=jax.ShapeDtypeStruct((d, n), BF16),
        scratch_shapes=[pltpu.VMEM((2, 2 * FOX_PAIRS_PER_STEP, tq, tq), F32)],
        compiler_params=_params(("arbitrary", "arbitrary", "arbitrary"), 32),
        name="fox_attn",
    )(qt, k, cb, vt)


def _fox_layer(x, w_in, b_f, w_out, g, b, alpha, bsz, t_len):
    d = x.shape[1]
    wqt = (w_in[:, :d] * (FOX_HEAD_DIM ** -0.5 * LOG2E)).T.astype(BF16)
    wk = w_in[:, d:2 * d].astype(BF16)
    wvt = w_in[:, 2 * d:3 * d].T.astype(BF16)
    wf = jnp.zeros((d, LANES), F32).at[:, :FOX_HEADS].set(w_in[:, 3 * d:]).astype(BF16)
    bf = jnp.zeros((1, LANES), F32).at[0, :FOX_HEADS].set(b_f)
    qt, k, vt = _fox_proj(x, wqt, wk, wvt)
    cb = _fox_gate(x, wf, bf, bsz, t_len)
    ot = _fox_attn(qt, k, cb, vt, bsz, t_len)
    return _outproj_ln(ot, w_out.astype(BF16), x, g, b, alpha, "fox_out_ln", transposed=True)


def _hgrn_constants():
    c = HGRN_CHUNK
    r = np.arange(c)[:, None]
    j = np.arange(c)[None, :]
    blocks = [j <= r]
    masks = []
    levels = []
    half = c // 2
    while half >= 1:
        ref = (r // (2 * half)) * (2 * half) + half - 1
        upper = (r % (2 * half)) >= half
        if half < HGRN_SMALL_LEVEL:
            blocks.append(np.where(upper, (j > ref) & (j <= r), (j > r) & (j <= ref)))
        masks.append(((r // (2 * half)) == (j // (2 * half))) & upper & ((j % (2 * half)) < half))
        levels.append(half)
        half //= 2
    masks.append(r == j)
    wall = np.concatenate(blocks, axis=0).astype(np.float32)
    wall2 = np.concatenate([wall, wall], axis=1)
    return wall2, np.stack(masks).astype(np.float32), tuple(levels)


def _hgrn_kernel(q_ref, fl_ref, i_ref, g_ref, lb_ref, ng_ref, wall_ref, mask_ref, o_ref, s_ref, *, levels):
    c = HGRN_CHUNK

    @pl.when(pl.program_id(1) == 0)
    def _():
        s_ref[...] = jnp.zeros_like(s_ref)

    wall = wall_ref[...]
    rowi = lax.broadcasted_iota(jnp.int32, (c, HGRN_DIM), 0)
    e_pair = None
    for h in range(HGRN_HEADS):
        sl = slice(h * HGRN_DIM, (h + 1) * HGRN_DIM)
        if h % 2 == 0:
            sl2 = slice(h * HGRN_DIM, (h + 2) * HGRN_DIM)
            lb2 = lb_ref[:, sl2]
            f2 = lb2 + (1.0 - lb2) * _sigmoid(fl_ref[:, sl2])
            logf2 = jnp.log(f2)
            hi = logf2.astype(BF16)
            mid = (logf2 - hi.astype(F32)).astype(BF16)
            e_pair = jnp.dot(wall, jnp.concatenate([hi, mid], axis=0), preferred_element_type=F32)
        lane0 = (h % 2) * HGRN_DIM
        q = q_ref[:, sl].astype(F32)
        i_b = i_ref[:, sl]
        k = 1.0 - f2[:, lane0:lane0 + HGRN_DIM]
        e_all = e_pair[:, lane0:lane0 + HGRN_DIM]
        b = e_all[0:c]
        x_pre = jnp.exp(b)
        x_suf = jnp.exp(b[c - 1:c, :] - b)
        st = s_ref[h]
        o = lax.dot_general((q * x_pre).astype(BF16), st.astype(BF16), _NT, preferred_element_type=F32)
        upd = lax.dot_general(i_b, (k * x_suf).astype(BF16), _TN, preferred_element_type=F32)
        s_ref[h] = st * x_pre[c - 1:c, :] + upd
        a = jnp.zeros((c, c), F32)
        n_big = sum(half >= HGRN_SMALL_LEVEL for half in levels)
        for l, half in enumerate(levels):
            upper = (rowi & half) != 0
            if half >= HGRN_SMALL_LEVEL:
                parts = []
                for r0 in range(0, c, 2 * half):
                    ref = b[r0 + half - 1:r0 + half, :]
                    parts += [ref - b[r0:r0 + half], b[r0 + half:r0 + 2 * half] - ref]
                e_lvl = jnp.concatenate(parts, axis=0)
            else:
                e_lvl = e_all[(1 + l - n_big) * c:(2 + l - n_big) * c]
            z = (jnp.where(upper, q, k) * jnp.exp(e_lvl)).astype(BF16)
            a = a + lax.dot_general(z, z, _NT, preferred_element_type=F32) * mask_ref[l]
        a = a + lax.dot_general(q.astype(BF16), k.astype(BF16), _NT,
                                preferred_element_type=F32) * mask_ref[len(levels)]
        o = o + jnp.dot(a.astype(BF16), i_b, preferred_element_type=F32)
        o = o * lax.rsqrt(jnp.mean(o * o, axis=-1, keepdims=True) + RMS_EPS) * ng_ref[...]
        gate = g_ref[:, sl].astype(F32)
        o_ref[:, sl] = (o * (gate * _sigmoid(gate))).astype(o_ref.dtype)


def _hgrn_core(q, fl, i, g, lb, ng, bsz, t_len):
    n, d = q.shape
    c = HGRN_CHUNK
    nc = t_len // c
    wall2, masks, levels = _hgrn_constants()
    row = lambda b, j: (b * nc + j, 0)
    fixed2 = lambda b, j: (0, 0)
    return pl.pallas_call(
        functools.partial(_hgrn_kernel, levels=levels),
        grid=(bsz, nc),
        in_specs=[pl.BlockSpec((c, d), row), pl.BlockSpec((c, d), row), pl.BlockSpec((c, d), row),
                  pl.BlockSpec((c, d), row), pl.BlockSpec((1, d), fixed2), pl.BlockSpec((1, HGRN_DIM), fixed2),
                  pl.BlockSpec(wall2.shape, fixed2), pl.BlockSpec(masks.shape, lambda b, j: (0, 0, 0))],
        out_specs=pl.BlockSpec((c, d), row),
        out_shape=jax.ShapeDtypeStruct((n, d), BF16),
        scratch_shapes=[pltpu.VMEM((HGRN_HEADS, HGRN_DIM, HGRN_DIM), F32)],
        compiler_params=_params(("arbitrary", "arbitrary"), 32),
        name="hgrn_core",
    )(q, fl, i, g, lb, ng, jnp.asarray(wall2, BF16), jnp.asarray(masks, F32))


def _hgrn_layer(x, w_in, lower_bound, norm_g, w_out, g, b, alpha, bsz, t_len):
    d = x.shape[1]
    wb = w_in.astype(BF16)
    q, fl, i, gate = _proj(x, [wb[:, :d], wb[:, d:2 * d], wb[:, 2 * d:3 * d], wb[:, 3 * d:]],
                           [BF16, F32, BF16, BF16], "hgrn_proj")
    o = _hgrn_core(q, fl, i, gate, lower_bound.reshape(1, d), norm_g.reshape(1, HGRN_DIM), bsz, t_len)
    return _outproj_ln(o, w_out.astype(BF16), x, g, b, alpha, "hgrn_out_ln")


def _conv_kernel(x_ref, win_ref, cw_ref, wout_ref, g_ref, b_ref, y_ref, zbuf, *, alpha, tm):
    d = x_ref.shape[1]

    @pl.when(pl.program_id(1) == 0)
    def _():
        zbuf[0:8, :] = jnp.zeros((8, d), F32)

    x = x_ref[...]
    p = jnp.dot(x.astype(BF16), win_ref[...], preferred_element_type=F32)
    z = p[:, d:2 * d] * p[:, 2 * d:]
    zbuf[8:8 + tm, :] = z
    y = cw_ref[2:3, :] * z + cw_ref[1:2, :] * zbuf[7:7 + tm, :] + cw_ref[0:1, :] * zbuf[6:6 + tm, :]
    zbuf[0:8, :] = z[tm - 8:, :]
    mixed = jnp.dot((p[:, :d] * y).astype(BF16), wout_ref[...], preferred_element_type=F32)
    y_ref[...] = _layer_norm(alpha * x + mixed, g_ref[...], b_ref[...])


def _conv_layer(x, w_in, conv_w, w_out, g, b, alpha, bsz, t_len, tm=512):
    n, d = x.shape
    nt = t_len // tm
    row = lambda bb, i: (bb * nt + i, 0)
    fixed = lambda bb, i: (0, 0)
    return pl.pallas_call(
        functools.partial(_conv_kernel, alpha=alpha, tm=tm),
        grid=(bsz, nt),
        in_specs=[pl.BlockSpec((tm, d), row), pl.BlockSpec((d, 3 * d), fixed), pl.BlockSpec(conv_w.shape, fixed),
                  pl.BlockSpec((d, d), fixed), pl.BlockSpec((1, d), fixed), pl.BlockSpec((1, d), fixed)],
        out_specs=pl.BlockSpec((tm, d), row),
        out_shape=jax.ShapeDtypeStruct((n, d), F32),
        scratch_shapes=[pltpu.VMEM((tm + 8, d), F32)],
        compiler_params=_params(("arbitrary", "arbitrary"), 48),
        name="conv_layer",
    )(x, w_in.astype(BF16), conv_w, w_out.astype(BF16), g, b)


def _store_token_tiles(xt_ref, v, row0=0):
    rows = v.shape[0]
    for c in range(TOKEN_TILE_ROWS):
        xt_ref[pl.ds(row0 + c, rows, stride=TOKEN_TILE_ROWS), :] = v[:, c * LANES:(c + 1) * LANES]


def _load_token_tiles(xt_ref, rows, row0=0):
    return jnp.concatenate([xt_ref[pl.ds(row0 + c, rows, stride=TOKEN_TILE_ROWS), :]
                            for c in range(TOKEN_TILE_ROWS)], axis=1)


def _router_kernel(x_ref, wh_ref, wl_ref, b_ref, xt_ref, meta_ref, oh_ref, cnt_ref):
    x = x_ref[...]
    xh = x.astype(BF16)
    xl = (x - xh.astype(F32)).astype(BF16)
    wh = wh_ref[...]
    logits = (jnp.dot(xh, wh, preferred_element_type=F32) + jnp.dot(xh, wl_ref[...], preferred_element_type=F32)
              + jnp.dot(xl, wh, preferred_element_type=F32)) + b_ref[...]
    lane = lax.broadcasted_iota(jnp.int32, logits.shape, 1)
    lane_f = lane.astype(F32)
    neg = -jnp.inf
    far = float(LANES)
    is_g = (lane >= ROUTER_GROUP_LANE0) & (lane < ROUTER_GROUP_LANE0 + N_GROUPS)
    gl = jnp.where(is_g, logits, neg)
    gmax = jnp.max(gl, axis=-1, keepdims=True)
    g_sel = jnp.min(jnp.where(gl == gmax, lane_f, far), axis=-1, keepdims=True) - float(ROUTER_GROUP_LANE0)
    grp_w = 1.0 / jnp.sum(jnp.where(is_g, jnp.exp(logits - gmax), 0.0), axis=-1, keepdims=True)
    in_g = (lane < N_EXPERTS) & ((lane // EXPERTS_PER_GROUP).astype(F32) == g_sel)
    el = jnp.where(in_g, logits, neg)
    t1 = jnp.max(el, axis=-1, keepdims=True)
    i1 = jnp.min(jnp.where(el == t1, lane_f, far), axis=-1, keepdims=True)
    el2 = jnp.where(lane_f == i1, neg, el)
    t2 = jnp.max(el2, axis=-1, keepdims=True)
    i2 = jnp.min(jnp.where(el2 == t2, lane_f, far), axis=-1, keepdims=True)
    e2 = jnp.exp(t2 - t1)
    w1 = grp_w / (1.0 + e2)
    w2 = w1 * e2
    first_lo = i1 < i2
    lo = jnp.where(first_lo, i1, i2) - g_sel * EXPERTS_PER_GROUP
    hi = jnp.where(first_lo, i2, i1) - g_sel * EXPERTS_PER_GROUP
    pair = lo * (EXPERTS_PER_GROUP - 1.0) - lo * (lo - 1.0) * 0.5 + (hi - lo - 1.0)
    onehot = jnp.where(lane_f == g_sel * MOE_PAIRS + pair, 1.0, 0.0)
    meta_ref[...] = (jnp.where(lane == 0, jnp.where(first_lo, w1, w2), 0.0)
                     + jnp.where(lane == 1, jnp.where(first_lo, w2, w1), 0.0))
    _store_token_tiles(xt_ref, x)
    oh_ref[...] = onehot.astype(BF16)

    @pl.when(pl.program_id(0) == 0)
    def _():
        cnt_ref[...] = jnp.zeros_like(cnt_ref)

    cnt_ref[...] += jnp.sum(onehot, axis=0, keepdims=True)


def _router(x, w_group, b_group, w_expert, b_expert, tm=MOE_TOK_TILE):
    n, d = x.shape
    w = jnp.zeros((d, LANES), F32).at[:, :N_EXPERTS].set(w_expert)
    w = w.at[:, ROUTER_GROUP_LANE0:ROUTER_GROUP_LANE0 + N_GROUPS].set(w_group)
    bias = jnp.zeros((1, LANES), F32).at[0, :N_EXPERTS].set(b_expert)
    bias = bias.at[0, ROUTER_GROUP_LANE0:ROUTER_GROUP_LANE0 + N_GROUPS].set(b_group)
    wh = w.astype(BF16)
    wl = (w - wh.astype(F32)).astype(BF16)
    row = lambda i: (i, 0)
    fixed = lambda i: (0, 0)
    return pl.pallas_call(
        _router_kernel,
        grid=(n // tm,),
        in_specs=[pl.BlockSpec((tm, d), row), pl.BlockSpec((d, LANES), fixed), pl.BlockSpec((d, LANES), fixed),
                  pl.BlockSpec((1, LANES), fixed)],
        out_specs=[pl.BlockSpec((tm * TOKEN_TILE_ROWS, LANES), row), pl.BlockSpec((tm, LANES), row),
                   pl.BlockSpec((tm, LANES), row), pl.BlockSpec((8, LANES), fixed)],
        out_shape=[jax.ShapeDtypeStruct((n * TOKEN_TILE_ROWS, LANES), F32), jax.ShapeDtypeStruct((n, LANES), F32),
                   jax.ShapeDtypeStruct((n, LANES), BF16), jax.ShapeDtypeStruct((8, LANES), F32)],
        compiler_params=_params(("arbitrary",), 32),
        name="moe_router",
    )(x, wh, wl, bias)


def _moe_pos_kernel(oh_ref, base_ref, lt_ref, pos_ref, carry_ref):
    @pl.when(pl.program_id(0) == 0)
    def _():
        carry_ref[...] = jnp.zeros_like(carry_ref)

    oh = oh_ref[...]
    rank = jnp.dot(lt_ref[...], oh, preferred_element_type=F32) + carry_ref[...] + base_ref[...]
    ohf = oh.astype(F32)
    val = ohf * rank
    carry_ref[...] += jnp.sum(ohf, axis=0, keepdims=True)
    hi = jnp.floor(val * (1.0 / 256.0))
    lo = val - 256.0 * hi
    ones = jnp.ones((8, LANES), BF16)
    pos = (256.0 * lax.dot_general(ones, hi.astype(BF16), _NT, preferred_element_type=F32)
           + lax.dot_general(ones, lo.astype(BF16), _NT, preferred_element_type=F32))
    pos_ref[0] = pos[0:1].astype(jnp.int32)


def _moe_pos(onehot, base, tm=MOE_TOK_TILE):
    n = onehot.shape[0]
    lt = jnp.asarray(np.tril(np.ones((tm, tm), np.float32), -1), BF16)
    return pl.pallas_call(
        _moe_pos_kernel,
        grid=(n // tm,),
        in_specs=[pl.BlockSpec((tm, LANES), lambda i: (i, 0)), pl.BlockSpec((1, LANES), lambda i: (0, 0)),
                  pl.BlockSpec((tm, tm), lambda i: (0, 0))],
        out_specs=pl.BlockSpec((1, 1, tm), lambda i: (i, 0, 0)),
        out_shape=jax.ShapeDtypeStruct((n // tm, 1, tm), jnp.int32),
        scratch_shapes=[pltpu.VMEM((1, LANES), F32)],
        compiler_params=_params(("arbitrary",), 32),
        name="moe_pos",
    )(onehot, base, lt)


def _tile_copy(src, src_tok, dst, dst_tok, sem):
    rows = TOKEN_TILE_ROWS
    return pltpu.make_async_copy(src.at[pl.ds(pl.multiple_of(src_tok * rows, rows), rows)],
                                 dst.at[pl.ds(pl.multiple_of(dst_tok * rows, rows), rows)], sem)


def _meta_copy(src, src_tok, dst, dst_tok, sem):
    return pltpu.make_async_copy(src.at[pl.ds(src_tok, 1)], dst.at[pl.ds(dst_tok, 1)], sem)


def _moe_scatter_kernel(pos_ref, xt_ref, meta_ref, xs_in_ref, xm_in_ref, xs_ref, xm_ref, sem, *, tm):
    del xs_in_ref, xm_in_ref

    def start(r, c):
        p = pos_ref[0, 0, r]
        _tile_copy(xt_ref, r, xs_ref, p, sem.at[0]).start(priority=0)
        _meta_copy(meta_ref, r, xm_ref, p, sem.at[1]).start(priority=1)
        return c

    lax.fori_loop(0, tm, start, 0, unroll=8)
    pltpu.make_async_copy(xt_ref, xs_ref.at[pl.ds(0, tm * TOKEN_TILE_ROWS)], sem.at[0]).wait()
    pltpu.make_async_copy(meta_ref, xm_ref.at[pl.ds(0, tm)], sem.at[1]).wait()


def _moe_scatter(pos, xt, meta, xs_buf, xm_buf, tm=MOE_TOK_TILE):
    n = meta.shape[0]
    return pl.pallas_call(
        functools.partial(_moe_scatter_kernel, tm=tm),
        grid=(n // tm,),
        in_specs=[pl.BlockSpec((1, 1, tm), lambda i: (i, 0, 0), memory_space=pltpu.SMEM),
                  pl.BlockSpec((tm * TOKEN_TILE_ROWS, LANES), lambda i: (i, 0)),
                  pl.BlockSpec((tm, LANES), lambda i: (i, 0)),
                  pl.BlockSpec(memory_space=pl.ANY), pl.BlockSpec(memory_space=pl.ANY)],
        out_specs=[pl.BlockSpec(memory_space=pl.ANY), pl.BlockSpec(memory_space=pl.ANY)],
        out_shape=[jax.ShapeDtypeStruct(xs_buf.shape, xs_buf.dtype), jax.ShapeDtypeStruct(xm_buf.shape, xm_buf.dtype)],
        scratch_shapes=[pltpu.SemaphoreType.DMA((2,))],
        input_output_aliases={3: 0, 4: 1},
        compiler_params=_params(("arbitrary",), 32),
        name="moe_scatter",
    )(pos, xt, meta, xs_buf, xm_buf)


def _moe_experts_kernel(grp_ref, e1_ref, e2_ref, nused_ref, xs_ref, xm_ref, wg_ref, wu_ref, wd_ref, ys_in_ref, ys_ref):
    del grp_ref, ys_in_ref
    i = pl.program_id(0)
    tr = xm_ref.shape[0]

    @pl.when(i < nused_ref[0])
    def _():
        x = _load_token_tiles(xs_ref, tr).astype(BF16)
        meta = xm_ref[...]
        y = None
        for k, e_ref in enumerate((e1_ref, e2_ref)):
            e = e_ref[i]
            hg = jnp.dot(x, wg_ref[0, e], preferred_element_type=F32)
            hu = jnp.dot(x, wu_ref[0, e], preferred_element_type=F32)
            h = hg * _sigmoid(hg) * hu * meta[:, k:k + 1]
            yk = jnp.dot(h.astype(BF16), wd_ref[0, e], preferred_element_type=F32)
            y = yk if y is None else y + yk
        _store_token_tiles(ys_ref, y)


def _moe_experts(xs_buf, xm_buf, ys_buf, tile_grp, tile_e1, tile_e2, n_used, w_gate, w_up, w_down, layer,
                 tr=MOE_ROW_TILE):
    r_max = xm_buf.shape[0]
    d, f = w_gate.shape[-2:]
    assert d == TOKEN_TILE_ROWS * LANES
    tile_rows = tr * TOKEN_TILE_ROWS
    epg = EXPERTS_PER_GROUP
    used = lambda i, grp, e1, e2, nu: (jnp.minimum(i, nu[0] - 1), 0)
    wmap = lambda i, grp, e1, e2, nu: (layer * N_GROUPS + grp[i], 0, 0, 0)
    grid_spec = pltpu.PrefetchScalarGridSpec(
        num_scalar_prefetch=4,
        grid=(r_max // tr,),
        in_specs=[pl.BlockSpec((tile_rows, LANES), used), pl.BlockSpec((tr, LANES), used),
                  pl.BlockSpec((1, epg, d, f), wmap), pl.BlockSpec((1, epg, d, f), wmap),
                  pl.BlockSpec((1, epg, f, d), wmap), pl.BlockSpec(memory_space=pl.ANY)],
        out_specs=pl.BlockSpec((tile_rows, LANES), used),
    )
    return pl.pallas_call(
        _moe_experts_kernel,
        grid_spec=grid_spec,
        out_shape=jax.ShapeDtypeStruct(ys_buf.shape, ys_buf.dtype),
        input_output_aliases={9: 0},
        compiler_params=_params(("arbitrary",), 48),
        name="moe_experts",
    )(tile_grp, tile_e1, tile_e2, n_used, xs_buf, xm_buf, w_gate, w_up, w_down, ys_buf)


def _moe_combine_ln_kernel(pos_ref, nxt_ref, x_ref, ys_ref, g_ref, b_ref, y_ref, buf_ref, sem, *, alpha, tm):
    i = pl.program_id(0)
    slot = i % 2

    def gather(p_ref, s):
        def start(r, c):
            _tile_copy(ys_ref, p_ref[0, 0, r], buf_ref, s * tm + r, sem.at[s]).start()
            return c
        lax.fori_loop(0, tm, start, 0, unroll=8)

    @pl.when(i == 0)
    def _():
        gather(pos_ref, 0)

    @pl.when(i + 1 < pl.num_programs(0))
    def _():
        gather(nxt_ref, 1 - slot)

    base = pl.multiple_of(slot * (tm * TOKEN_TILE_ROWS), tm * TOKEN_TILE_ROWS)
    slot_rows = pl.ds(base, tm * TOKEN_TILE_ROWS)
    pltpu.make_async_copy(ys_ref.at[pl.ds(0, tm * TOKEN_TILE_ROWS)], buf_ref.at[slot_rows], sem.at[slot]).wait()
    ffn = _load_token_tiles(buf_ref, tm, base)
    y_ref[...] = _layer_norm(alpha * x_ref[...] + ffn, g_ref[...], b_ref[...])


def _moe_combine_ln(pos, x, ys_buf, g, b, alpha, tm=MOE_TOK_TILE):
    n, d = x.shape
    nt = n // tm
    row = lambda i: (i, 0)
    fixed = lambda i: (0, 0)
    return pl.pallas_call(
        functools.partial(_moe_combine_ln_kernel, alpha=alpha, tm=tm),
        grid=(nt,),
        in_specs=[pl.BlockSpec((1, 1, tm), lambda i: (i, 0, 0), memory_space=pltpu.SMEM),
                  pl.BlockSpec((1, 1, tm), lambda i: (jnp.minimum(i + 1, nt - 1), 0, 0), memory_space=pltpu.SMEM),
                  pl.BlockSpec((tm, d), row), pl.BlockSpec(memory_space=pl.ANY),
                  pl.BlockSpec((1, d), fixed), pl.BlockSpec((1, d), fixed)],
        out_specs=pl.BlockSpec((tm, d), row),
        out_shape=jax.ShapeDtypeStruct((n, d), F32),
        scratch_shapes=[pltpu.VMEM((2 * tm * TOKEN_TILE_ROWS, LANES), F32), pltpu.SemaphoreType.DMA((2,))],
        compiler_params=_params(("arbitrary",), 32),
        name="moe_combine_ln",
    )(pos, pos, x, ys_buf, g, b)


def _bucket_experts():
    lo, hi = [], []
    for a in range(EXPERTS_PER_GROUP):
        for c in range(a + 1, EXPERTS_PER_GROUP):
            lo.append(a)
            hi.append(c)
    grp = np.repeat(np.arange(N_GROUPS), MOE_PAIRS).astype(np.int32)
    return grp, np.tile(lo, N_GROUPS).astype(np.int32), np.tile(hi, N_GROUPS).astype(np.int32)


def _moe_layer(x, bufs, w_group, b_group, w_expert, b_expert, w_gate, w_up, w_down, layer, g, b, alpha):
    tr = MOE_ROW_TILE
    xs_buf, xm_buf, ys_buf = bufs
    xt, meta, onehot, cnt = _router(x, w_group, b_group, w_expert, b_expert)
    counts = cnt[0, :MOE_BUCKETS].astype(jnp.int32)
    padded = (counts + (tr - 1)) // tr * tr
    ends = jnp.cumsum(padded)
    base = jnp.zeros((1, LANES), F32).at[0, :MOE_BUCKETS].set((ends - padded).astype(F32))
    n_used = ends[-1] // tr
    n_tiles = xm_buf.shape[0] // tr
    tile = jnp.minimum(jnp.arange(n_tiles, dtype=jnp.int32), n_used - 1)
    tile_bkt = jnp.sum((ends[None, :] <= (tile * tr)[:, None]).astype(jnp.int32), axis=1)
    tile_bkt = jnp.minimum(tile_bkt, MOE_BUCKETS - 1)
    b_grp, b_lo, b_hi = (jnp.asarray(t)[tile_bkt] for t in _bucket_experts())
    pos = _moe_pos(onehot, base)
    xs_buf, xm_buf = _moe_scatter(pos, xt, meta, xs_buf, xm_buf)
    ys_buf = _moe_experts(xs_buf, xm_buf, ys_buf, b_grp, b_lo, b_hi, n_used.reshape(1).astype(jnp.int32),
                          w_gate, w_up, w_down, layer)
    return _moe_combine_ln(pos, x, ys_buf, g, b, alpha), (xs_buf, xm_buf, ys_buf)


def kernel(x, ln_mix_g, ln_mix_b, ln_ffn_g, ln_ffn_b, fox_w_in, fox_b_f, fox_w_out, hgrn_w_in, hgrn_lb_logits, hgrn_norm_g, hgrn_w_out, conv_w_in, conv_w, conv_w_out, moe_w_group, moe_b_group, moe_w_expert, moe_b_expert, moe_w_gate, moe_w_up, moe_w_down):
    bsz, t_len, d = x.shape
    depth = ln_mix_g.shape[0]
    alpha = float((2 * depth) ** 0.25)
    assert d == FOX_HEADS * FOX_HEAD_DIM == HGRN_HEADS * HGRN_DIM
    assert t_len % 512 == 0 and (bsz * t_len) % 1024 == 0

    lb_prob = jax.nn.softmax(hgrn_lb_logits.astype(F32), axis=0)
    lower_bounds = jnp.cumsum(lb_prob, axis=0) - lb_prob[0]

    h = x.reshape(bsz * t_len, d)
    r_max = bsz * t_len + MOE_BUCKETS * MOE_ROW_TILE
    by_group = lambda w: w.astype(BF16).reshape((depth * N_GROUPS, EXPERTS_PER_GROUP) + w.shape[2:])
    moe_w_gate, moe_w_up, moe_w_down = by_group(moe_w_gate), by_group(moe_w_up), by_group(moe_w_down)
    bufs = (jnp.zeros((r_max * TOKEN_TILE_ROWS, LANES), F32), jnp.zeros((r_max, LANES), F32),
            jnp.zeros((r_max * TOKEN_TILE_ROWS, LANES), F32))
    for layer in range(depth):
        kind, j = layer % 3, layer // 3
        g_mix, b_mix = ln_mix_g[layer].reshape(1, d), ln_mix_b[layer].reshape(1, d)
        if kind == 0:
            h = _fox_layer(h, fox_w_in[j], fox_b_f[j], fox_w_out[j], g_mix, b_mix, alpha, bsz, t_len)
        elif kind == 1:
            h = _hgrn_layer(h, hgrn_w_in[j], lower_bounds[layer], hgrn_norm_g[j], hgrn_w_out[j],
                            g_mix, b_mix, alpha, bsz, t_len)
        else:
            h = _conv_layer(h, conv_w_in[j], conv_w[j], conv_w_out[j], g_mix, b_mix, alpha, bsz, t_len)
        h, bufs = _moe_layer(
            h, bufs, moe_w_group[layer], moe_b_group[layer], moe_w_expert[layer], moe_b_expert[layer],
            moe_w_gate, moe_w_up, moe_w_down, layer,
            ln_ffn_g[layer].reshape(1, d), ln_ffn_b[layer].reshape(1, d), alpha)
    return h.reshape(bsz, t_len, d)
```

```python
import functools

import numpy as np
import jax
import jax.numpy as jnp
from jax import lax
from jax.experimental import pallas as pl
from jax.experimental.pallas import tpu as pltpu

F32 = jnp.float32
BF16 = jnp.bfloat16

FOX_HEADS = 16
FOX_HEAD_DIM = 64
FOX_PAIRS_PER_STEP = 4
HGRN_HEADS = 8
HGRN_DIM = 128
HGRN_CHUNK = 128
HGRN_SMALL_LEVEL = 8
N_GROUPS = 4
EXPERTS_PER_GROUP = 8
N_EXPERTS = N_GROUPS * EXPERTS_PER_GROUP
LN_EPS = 1e-5
RMS_EPS = 1e-6
LOG2E = 1.4426950408889634
LANES = 128
ROUTER_GROUP_LANE0 = N_EXPERTS
MOE_PAIRS = EXPERTS_PER_GROUP * (EXPERTS_PER_GROUP - 1) // 2
MOE_BUCKETS = N_GROUPS * MOE_PAIRS
MOE_ROW_TILE = 128
MOE_TILES_PER_STEP = 4
MOE_TOK_TILE = 256
TOKEN_TILE_ROWS = 8

_NT = (((1,), (1,)), ((), ()))
_TN = (((0,), (0,)), ((), ()))


def _params(semantics, vmem_mb):
    return pltpu.CompilerParams(dimension_semantics=semantics,
                                vmem_limit_bytes=vmem_mb * 1024 * 1024)


def _layer_norm(v, g, b):
    mu = jnp.mean(v, axis=-1, keepdims=True)
    d = v - mu
    var = jnp.mean(d * d, axis=-1, keepdims=True)
    return d * lax.rsqrt(var + LN_EPS) * g + b


def _sigmoid(v):
    return 1.0 / (1.0 + jnp.exp(-v))


def _proj_kernel(x_ref, *refs, n_out):
    xb = x_ref[...].astype(BF16)
    for w_ref, o_ref in zip(refs[:n_out], refs[n_out:]):
        o_ref[...] = jnp.dot(xb, w_ref[...], preferred_element_type=F32).astype(o_ref.dtype)


def _proj(x, ws, dtypes, name, tm=512):
    n, k = x.shape
    return pl.pallas_call(
        functools.partial(_proj_kernel, n_out=len(ws)),
        grid=(n // tm,),
        in_specs=[pl.BlockSpec((tm, k), lambda i: (i, 0))]
        + [pl.BlockSpec(w.shape, lambda i: (0, 0)) for w in ws],
        out_specs=[pl.BlockSpec((tm, w.shape[1]), lambda i: (i, 0)) for w in ws],
        out_shape=[jax.ShapeDtypeStruct((n, w.shape[1]), dt) for w, dt in zip(ws, dtypes)],
        compiler_params=_params(("arbitrary",), 48),
        name=name,
    )(x, *ws)


def _outproj_ln_kernel(o_ref, w_ref, x_ref, g_ref, b_ref, y_ref, *, alpha, transposed):
    dims = _TN if transposed else (((1,), (0,)), ((), ()))
    mixed = lax.dot_general(o_ref[...], w_ref[...], dims, preferred_element_type=F32)
    y_ref[...] = _layer_norm(alpha * x_ref[...] + mixed, g_ref[...], b_ref[...])


def _outproj_ln(o, w, x, g, b, alpha, name, tm=512, transposed=False):
    n, d = x.shape
    k = w.shape[0]
    row = lambda i: (i, 0)
    fixed = lambda i: (0, 0)
    o_spec = pl.BlockSpec((k, tm), lambda i: (0, i)) if transposed else pl.BlockSpec((tm, k), row)
    return pl.pallas_call(
        functools.partial(_outproj_ln_kernel, alpha=alpha, transposed=transposed),
        grid=(n // tm,),
        in_specs=[o_spec, pl.BlockSpec((k, d), fixed),
                  pl.BlockSpec((tm, d), row), pl.BlockSpec((1, d), fixed), pl.BlockSpec((1, d), fixed)],
        out_specs=pl.BlockSpec((tm, d), row),
        out_shape=jax.ShapeDtypeStruct((n, d), F32),
        compiler_params=_params(("arbitrary",), 32),
        name=name,
    )(o, w, x, g, b)


def _split3(v):
    hi = v.astype(BF16)
    r1 = v - hi.astype(F32)
    mid = r1.astype(BF16)
    return hi, mid, (r1 - mid.astype(F32)).astype(BF16)


def _fox_proj_kernel(x_ref, wqt_ref, wk_ref, wvt_ref, qt_ref, k_ref, vt_ref):
    xb = x_ref[...].astype(BF16)
    qt_ref[...] = lax.dot_general(wqt_ref[...], xb, _NT, preferred_element_type=F32).astype(BF16)
    k_ref[...] = jnp.dot(xb, wk_ref[...], preferred_element_type=F32).astype(BF16)
    vt_ref[...] = lax.dot_general(wvt_ref[...], xb, _NT, preferred_element_type=F32).astype(BF16)


def _fox_proj(x, wqt, wk, wvt, tm=512):
    n, d = x.shape
    fixed = lambda i: (0, 0)
    return pl.pallas_call(
        _fox_proj_kernel,
        grid=(n // tm,),
        in_specs=[pl.BlockSpec((tm, d), lambda i: (i, 0)), pl.BlockSpec((d, d), fixed),
                  pl.BlockSpec((d, d), fixed), pl.BlockSpec((d, d), fixed)],
        out_specs=[pl.BlockSpec((d, tm), lambda i: (0, i)), pl.BlockSpec((tm, d), lambda i: (i, 0)),
                   pl.BlockSpec((d, tm), lambda i: (0, i))],
        out_shape=[jax.ShapeDtypeStruct((d, n), BF16), jax.ShapeDtypeStruct((n, d), BF16),
                   jax.ShapeDtypeStruct((d, n), BF16)],
        compiler_params=_params(("arbitrary",), 48),
        name="fox_proj",
    )(x, wqt, wk, wvt)


def _fox_gate_kernel(x_ref, wf_ref, bf_ref, tri_ref, sel_ref, cb_ref, carry_ref):
    @pl.when(pl.program_id(1) == 0)
    def _():
        carry_ref[...] = jnp.zeros_like(carry_ref)

    z = jnp.dot(x_ref[...].astype(BF16), wf_ref[...], preferred_element_type=F32) + bf_ref[...]
    logf = jnp.minimum(z, 0.0) - jnp.log(1.0 + jnp.exp(-jnp.abs(z)))
    tri = tri_ref[...]
    c = carry_ref[...]
    for part in _split3(logf):
        c = c + jnp.dot(tri, part, preferred_element_type=F32)
    carry_ref[...] = c[c.shape[0] - 1:, :]
    cb = None
    for j, part in enumerate(_split3(c * (-LOG2E))):
        term = jnp.dot(part, sel_ref[j], preferred_element_type=F32)
        cb = term if cb is None else cb + term
    cb_ref[...] = cb.astype(BF16)


def _fox_gate(x, wf, bf, bsz, t_len, tg=512):
    n, d = x.shape
    nt = t_len // tg
    tri = jnp.asarray(np.tril(np.ones((tg, tg), np.float32)), BF16)
    sel = np.zeros((3, LANES, d), np.float32)
    for h in range(FOX_HEADS):
        for j in range(3):
            sel[j, h, (h // 2) * LANES + 3 * (h % 2) + j] = 1.0
    return pl.pallas_call(
        _fox_gate_kernel,
        grid=(bsz, nt),
        in_specs=[pl.BlockSpec((tg, d), lambda b, i: (b * nt + i, 0)),
                  pl.BlockSpec((d, LANES), lambda b, i: (0, 0)),
                  pl.BlockSpec((1, LANES), lambda b, i: (0, 0)),
                  pl.BlockSpec((tg, tg), lambda b, i: (0, 0)),
                  pl.BlockSpec((3, LANES, d), lambda b, i: (0, 0, 0))],
        out_specs=pl.BlockSpec((tg, d), lambda b, i: (b * nt + i, 0)),
        out_shape=jax.ShapeDtypeStruct((n, d), BF16),
        scratch_shapes=[pltpu.VMEM((1, LANES), F32)],
        compiler_params=_params(("arbitrary", "arbitrary"), 32),
        name="fox_gate",
    )(x, wf, bf, tri, jnp.asarray(sel, BF16))


def _fox_attn_kernel(qt_ref, k_ref, cb_ref, vt_ref, ot_ref, s_ref, *, tq):
    qi = pl.program_id(2)
    n_heads = 2 * FOX_PAIRS_PER_STEP
    feat = lax.broadcasted_iota(jnp.int32, (LANES, tq), 0)
    rhs = []
    for h in range(n_heads):
        pair, hh = divmod(h, 2)
        qt = qt_ref[pair * LANES:(pair + 1) * LANES, :].astype(F32)
        own = (feat >= hh * FOX_HEAD_DIM) & (feat < (hh + 1) * FOX_HEAD_DIM)
        bias_rows = (feat >= 3 * hh) & (feat < 3 * hh + 3)
        rhs.append(jnp.concatenate([jnp.where(own, qt, 0.0).astype(BF16),
                                    jnp.where(bias_rows, 1.0, 0.0).astype(BF16)], axis=0))
    key_i = lax.broadcasted_iota(jnp.int32, (tq, tq), 0)
    qry_i = lax.broadcasted_iota(jnp.int32, (tq, tq), 1)
    ones_rows = jnp.ones((16, tq), BF16)

    def scores(j, slot):
        start = pl.multiple_of(j * tq, tq)
        for pair in range(FOX_PAIRS_PER_STEP):
            lanes = slice(pair * LANES, (pair + 1) * LANES)
            kext = jnp.concatenate([k_ref[pl.ds(start, tq), lanes], cb_ref[pl.ds(start, tq), lanes]], axis=1)
            for h in (2 * pair, 2 * pair + 1):
                s_ref[slot, h] = jnp.dot(kext, rhs[h], preferred_element_type=F32)

    def block(j, carry, slot, masked):
        start = pl.multiple_of(j * tq, tq)
        if not masked:
            scores(j + 1, 1 - slot)
        out = []
        for h in range(n_heads):
            m, l, acc = carry[h]
            st = s_ref[slot, h]
            if masked:
                st = jnp.where(key_i <= qry_i, st, -jnp.inf)
            m_new = jnp.maximum(m, jnp.max(st, axis=0, keepdims=True))
            p = jnp.exp2(st - m_new)
            a = jnp.exp2(m - m_new)
            vt = vt_ref[h * FOX_HEAD_DIM:(h + 1) * FOX_HEAD_DIM, pl.ds(start, tq)]
            pv = jnp.dot(jnp.concatenate([vt, ones_rows], axis=0), p.astype(BF16), preferred_element_type=F32)
            out.append((m_new, a * l + pv[FOX_HEAD_DIM:FOX_HEAD_DIM + 1], acc * a + pv[:FOX_HEAD_DIM]))
        return tuple(out)

    init = (jnp.full((1, tq), -jnp.inf, F32), jnp.zeros((1, tq), F32), jnp.zeros((FOX_HEAD_DIM, tq), F32))
    scores(0, 0)
    carry = lax.fori_loop(
        0, qi // 2, lambda i, c: block(2 * i + 1, block(2 * i, c, 0, False), 1, False), (init,) * n_heads)
    carry = lax.cond(
        qi % 2 == 0,
        lambda c: block(qi, c, 0, True),
        lambda c: block(qi, block(qi - 1, c, 0, False), 1, True),
        carry)
    for h in range(n_heads):
        _, l, acc = carry[h]
        ot_ref[h * FOX_HEAD_DIM:(h + 1) * FOX_HEAD_DIM, :] = (acc / l).astype(ot_ref.dtype)


def _fox_attn(qt, k, cb, vt, bsz, t_len, tq=256):
    d, n = qt.shape
    w = FOX_PAIRS_PER_STEP * LANES
    nq = t_len // tq
    return pl.pallas_call(
        functools.partial(_fox_attn_kernel, tq=tq),
        grid=(bsz, d // w, nq),
        in_specs=[pl.BlockSpec((w, tq), lambda b, p, i: (p, b * nq + i)),
                  pl.BlockSpec((t_len, w), lambda b, p, i: (b, p)),
                  pl.BlockSpec((t_len, w), lambda b, p, i: (b, p)),
                  pl.BlockSpec((w, t_len), lambda b, p, i: (p, b))],
        out_specs=pl.BlockSpec((w, tq), lambda b, p, i: (p, b * nq + i)),
        out_shape=jax.ShapeDtypeStruct((d, n), BF16),
        scratch_shapes=[pltpu.VMEM((2, 2 * FOX_PAIRS_PER_STEP, tq, tq), F32)],
        compiler_params=_params(("arbitrary", "arbitrary", "arbitrary"), 32),
        name="fox_attn",
    )(qt, k, cb, vt)


def _fox_layer(x, w_in, b_f, w_out, g, b, alpha, bsz, t_len):
    d = x.shape[1]
    wqt = (w_in[:, :d] * (FOX_HEAD_DIM ** -0.5 * LOG2E)).T.astype(BF16)
    wk = w_in[:, d:2 * d].astype(BF16)
    wvt = w_in[:, 2 * d:3 * d].T.astype(BF16)
    wf = jnp.zeros((d, LANES), F32).at[:, :FOX_HEADS].set(w_in[:, 3 * d:]).astype(BF16)
    bf = jnp.zeros((1, LANES), F32).at[0, :FOX_HEADS].set(b_f)
    qt, k, vt = _fox_proj(x, wqt, wk, wvt)
    cb = _fox_gate(x, wf, bf, bsz, t_len)
    ot = _fox_attn(qt, k, cb, vt, bsz, t_len)
    return _outproj_ln(ot, w_out.astype(BF16), x, g, b, alpha, "fox_out_ln", transposed=True)


def _hgrn_constants():
    c = HGRN_CHUNK
    r = np.arange(c)[:, None]
    j = np.arange(c)[None, :]
    blocks = [j <= r]
    masks = []
    levels = []
    half = c // 2
    while half >= 1:
        ref = (r // (2 * half)) * (2 * half) + half - 1
        upper = (r % (2 * half)) >= half
        if half < HGRN_SMALL_LEVEL:
            blocks.append(np.where(upper, (j > ref) & (j <= r), (j > r) & (j <= ref)))
        masks.append(((r // (2 * half)) == (j // (2 * half))) & upper & ((j % (2 * half)) < half))
        levels.append(half)
        half //= 2
    masks.append(r == j)
    wall = np.concatenate(blocks, axis=0).astype(np.float32)
    wall2 = np.concatenate([wall, wall], axis=1)
    return wall2, np.stack(masks).astype(np.float32), tuple(levels)


def _hgrn_kernel(q_ref, fl_ref, i_ref, g_ref, lb_ref, ng_ref, wall_ref, mask_ref, o_ref, s_ref, *, levels):
    c = HGRN_CHUNK

    @pl.when(pl.program_id(1) == 0)
    def _():
        s_ref[...] = jnp.zeros_like(s_ref)

    wall = wall_ref[...]
    rowi = lax.broadcasted_iota(jnp.int32, (c, HGRN_DIM), 0)
    e_pair = None
    for h in range(HGRN_HEADS):
        sl = slice(h * HGRN_DIM, (h + 1) * HGRN_DIM)
        if h % 2 == 0:
            sl2 = slice(h * HGRN_DIM, (h + 2) * HGRN_DIM)
            lb2 = lb_ref[:, sl2]
            f2 = lb2 + (1.0 - lb2) * _sigmoid(fl_ref[:, sl2])
            logf2 = jnp.log(f2)
            hi = logf2.astype(BF16)
            mid = (logf2 - hi.astype(F32)).astype(BF16)
            e_pair = jnp.dot(wall, jnp.concatenate([hi, mid], axis=0), preferred_element_type=F32)
        lane0 = (h % 2) * HGRN_DIM
        q = q_ref[:, sl].astype(F32)
        i_b = i_ref[:, sl]
        k = 1.0 - f2[:, lane0:lane0 + HGRN_DIM]
        e_all = e_pair[:, lane0:lane0 + HGRN_DIM]
        b = e_all[0:c]
        x_pre = jnp.exp(b)
        x_suf = jnp.exp(b[c - 1:c, :] - b)
        st = s_ref[h]
        o = lax.dot_general((q * x_pre).astype(BF16), st.astype(BF16), _NT, preferred_element_type=F32)
        upd = lax.dot_general(i_b, (k * x_suf).astype(BF16), _TN, preferred_element_type=F32)
        s_ref[h] = st * x_pre[c - 1:c, :] + upd
        a = jnp.zeros((c, c), F32)
        n_big = sum(half >= HGRN_SMALL_LEVEL for half in levels)
        for l, half in enumerate(levels):
            upper = (rowi & half) != 0
            if half >= HGRN_SMALL_LEVEL:
                parts = []
                for r0 in range(0, c, 2 * half):
                    ref = b[r0 + half - 1:r0 + half, :]
                    parts += [ref - b[r0:r0 + half], b[r0 + half:r0 + 2 * half] - ref]
                e_lvl = jnp.concatenate(parts, axis=0)
            else:
                e_lvl = e_all[(1 + l - n_big) * c:(2 + l - n_big) * c]
            z = (jnp.where(upper, q, k) * jnp.exp(e_lvl)).astype(BF16)
            a = a + lax.dot_general(z, z, _NT, preferred_element_type=F32) * mask_ref[l]
        a = a + lax.dot_general(q.astype(BF16), k.astype(BF16), _NT,
                                preferred_element_type=F32) * mask_ref[len(levels)]
        o = o + jnp.dot(a.astype(BF16), i_b, preferred_element_type=F32)
        o = o * lax.rsqrt(jnp.mean(o * o, axis=-1, keepdims=True) + RMS_EPS) * ng_ref[...]
        gate = g_ref[:, sl].astype(F32)
        o_ref[:, sl] = (o * (gate * _sigmoid(gate))).astype(o_ref.dtype)


def _hgrn_core(q, fl, i, g, lb, ng, bsz, t_len):
    n, d = q.shape
    c = HGRN_CHUNK
    nc = t_len // c
    wall2, masks, levels = _hgrn_constants()
    row = lambda b, j: (b * nc + j, 0)
    fixed2 = lambda b, j: (0, 0)
    return pl.pallas_call(
        functools.partial(_hgrn_kernel, levels=levels),
        grid=(bsz, nc),
        in_specs=[pl.BlockSpec((c, d), row), pl.BlockSpec((c, d), row), pl.BlockSpec((c, d), row),
                  pl.BlockSpec((c, d), row), pl.BlockSpec((1, d), fixed2), pl.BlockSpec((1, HGRN_DIM), fixed2),
                  pl.BlockSpec(wall2.shape, fixed2), pl.BlockSpec(masks.shape, lambda b, j: (0, 0, 0))],
        out_specs=pl.BlockSpec((c, d), row),
        out_shape=jax.ShapeDtypeStruct((n, d), BF16),
        scratch_shapes=[pltpu.VMEM((HGRN_HEADS, HGRN_DIM, HGRN_DIM), F32)],
        compiler_params=_params(("arbitrary", "arbitrary"), 32),
        name="hgrn_core",
    )(q, fl, i, g, lb, ng, jnp.asarray(wall2, BF16), jnp.asarray(masks, F32))


def _hgrn_layer(x, w_in, lower_bound, norm_g, w_out, g, b, alpha, bsz, t_len):
    d = x.shape[1]
    wb = w_in.astype(BF16)
    q, fl, i, gate = _proj(x, [wb[:, :d], wb[:, d:2 * d], wb[:, 2 * d:3 * d], wb[:, 3 * d:]],
                           [BF16, F32, BF16, BF16], "hgrn_proj")
    o = _hgrn_core(q, fl, i, gate, lower_bound.reshape(1, d), norm_g.reshape(1, HGRN_DIM), bsz, t_len)
    return _outproj_ln(o, w_out.astype(BF16), x, g, b, alpha, "hgrn_out_ln")


def _conv_kernel(x_ref, win_ref, cw_ref, wout_ref, g_ref, b_ref, y_ref, zbuf, *, alpha, tm):
    d = x_ref.shape[1]

    @pl.when(pl.program_id(1) == 0)
    def _():
        zbuf[0:8, :] = jnp.zeros((8, d), F32)

    x = x_ref[...]
    p = jnp.dot(x.astype(BF16), win_ref[...], preferred_element_type=F32)
    z = p[:, d:2 * d] * p[:, 2 * d:]
    zbuf[8:8 + tm, :] = z
    y = cw_ref[2:3, :] * z + cw_ref[1:2, :] * zbuf[7:7 + tm, :] + cw_ref[0:1, :] * zbuf[6:6 + tm, :]
    zbuf[0:8, :] = z[tm - 8:, :]
    mixed = jnp.dot((p[:, :d] * y).astype(BF16), wout_ref[...], preferred_element_type=F32)
    y_ref[...] = _layer_norm(alpha * x + mixed, g_ref[...], b_ref[...])


def _conv_layer(x, w_in, conv_w, w_out, g, b, alpha, bsz, t_len, tm=512):
    n, d = x.shape
    nt = t_len // tm
    row = lambda bb, i: (bb * nt + i, 0)
    fixed = lambda bb, i: (0, 0)
    return pl.pallas_call(
        functools.partial(_conv_kernel, alpha=alpha, tm=tm),
        grid=(bsz, nt),
        in_specs=[pl.BlockSpec((tm, d), row), pl.BlockSpec((d, 3 * d), fixed), pl.BlockSpec(conv_w.shape, fixed),
                  pl.BlockSpec((d, d), fixed), pl.BlockSpec((1, d), fixed), pl.BlockSpec((1, d), fixed)],
        out_specs=pl.BlockSpec((tm, d), row),
        out_shape=jax.ShapeDtypeStruct((n, d), F32),
        scratch_shapes=[pltpu.VMEM((tm + 8, d), F32)],
        compiler_params=_params(("arbitrary", "arbitrary"), 48),
        name="conv_layer",
    )(x, w_in.astype(BF16), conv_w, w_out.astype(BF16), g, b)


def _store_token_tiles(xt_ref, v, row0=0):
    rows = v.shape[0]
    for c in range(TOKEN_TILE_ROWS):
        xt_ref[pl.ds(row0 + c, rows, stride=TOKEN_TILE_ROWS), :] = v[:, c * LANES:(c + 1) * LANES]


def _load_token_tiles(xt_ref, rows, row0=0):
    return jnp.concatenate([xt_ref[pl.ds(row0 + c, rows, stride=TOKEN_TILE_ROWS), :]
                            for c in range(TOKEN_TILE_ROWS)], axis=1)


def _router_kernel(x_ref, wh_ref, wl_ref, b_ref, xt_ref, meta_ref, oh_ref, cnt_ref):
    x = x_ref[...]
    xh = x.astype(BF16)
    xl = (x - xh.astype(F32)).astype(BF16)
    wh = wh_ref[...]
    logits = (jnp.dot(xh, wh, preferred_element_type=F32) + jnp.dot(xh, wl_ref[...], preferred_element_type=F32)
              + jnp.dot(xl, wh, preferred_element_type=F32)) + b_ref[...]
    lane = lax.broadcasted_iota(jnp.int32, logits.shape, 1)
    lane_f = lane.astype(F32)
    neg = -jnp.inf
    far = float(LANES)
    is_g = (lane >= ROUTER_GROUP_LANE0) & (lane < ROUTER_GROUP_LANE0 + N_GROUPS)
    gl = jnp.where(is_g, logits, neg)
    gmax = jnp.max(gl, axis=-1, keepdims=True)
    g_sel = jnp.min(jnp.where(gl == gmax, lane_f, far), axis=-1, keepdims=True) - float(ROUTER_GROUP_LANE0)
    grp_w = 1.0 / jnp.sum(jnp.where(is_g, jnp.exp(logits - gmax), 0.0), axis=-1, keepdims=True)
    in_g = (lane < N_EXPERTS) & ((lane // EXPERTS_PER_GROUP).astype(F32) == g_sel)
    el = jnp.where(in_g, logits, neg)
    t1 = jnp.max(el, axis=-1, keepdims=True)
    i1 = jnp.min(jnp.where(el == t1, lane_f, far), axis=-1, keepdims=True)
    el2 = jnp.where(lane_f == i1, neg, el)
    t2 = jnp.max(el2, axis=-1, keepdims=True)
    i2 = jnp.min(jnp.where(el2 == t2, lane_f, far), axis=-1, keepdims=True)
    e2 = jnp.exp(t2 - t1)
    w1 = grp_w / (1.0 + e2)
    w2 = w1 * e2
    first_lo = i1 < i2
    lo = jnp.where(first_lo, i1, i2) - g_sel * EXPERTS_PER_GROUP
    hi = jnp.where(first_lo, i2, i1) - g_sel * EXPERTS_PER_GROUP
    pair = lo * (EXPERTS_PER_GROUP - 1.0) - lo * (lo - 1.0) * 0.5 + (hi - lo - 1.0)
    onehot = jnp.where(lane_f == g_sel * MOE_PAIRS + pair, 1.0, 0.0)
    meta_ref[...] = (jnp.where(lane == 0, jnp.where(first_lo, w1, w2), 0.0)
                     + jnp.where(lane == 1, jnp.where(first_lo, w2, w1), 0.0))
    _store_token_tiles(xt_ref, x)
    oh_ref[...] = onehot.astype(BF16)

    @pl.when(pl.program_id(0) == 0)
    def _():
        cnt_ref[...] = jnp.zeros_like(cnt_ref)

    cnt_ref[...] += jnp.sum(onehot, axis=0, keepdims=True)


def _router(x, w_group, b_group, w_expert, b_expert, tm=MOE_TOK_TILE):
    n, d = x.shape
    w = jnp.zeros((d, LANES), F32).at[:, :N_EXPERTS].set(w_expert)
    w = w.at[:, ROUTER_GROUP_LANE0:ROUTER_GROUP_LANE0 + N_GROUPS].set(w_group)
    bias = jnp.zeros((1, LANES), F32).at[0, :N_EXPERTS].set(b_expert)
    bias = bias.at[0, ROUTER_GROUP_LANE0:ROUTER_GROUP_LANE0 + N_GROUPS].set(b_group)
    wh = w.astype(BF16)
    wl = (w - wh.astype(F32)).astype(BF16)
    row = lambda i: (i, 0)
    fixed = lambda i: (0, 0)
    return pl.pallas_call(
        _router_kernel,
        grid=(n // tm,),
        in_specs=[pl.BlockSpec((tm, d), row), pl.BlockSpec((d, LANES), fixed), pl.BlockSpec((d, LANES), fixed),
                  pl.BlockSpec((1, LANES), fixed)],
        out_specs=[pl.BlockSpec((tm * TOKEN_TILE_ROWS, LANES), row), pl.BlockSpec((tm, LANES), row),
                   pl.BlockSpec((tm, LANES), row), pl.BlockSpec((8, LANES), fixed)],
        out_shape=[jax.ShapeDtypeStruct((n * TOKEN_TILE_ROWS, LANES), F32), jax.ShapeDtypeStruct((n, LANES), F32),
                   jax.ShapeDtypeStruct((n, LANES), BF16), jax.ShapeDtypeStruct((8, LANES), F32)],
        compiler_params=_params(("arbitrary",), 32),
        name="moe_router",
    )(x, wh, wl, bias)


def _moe_pos_kernel(oh_ref, base_ref, lt_ref, pos_ref, carry_ref):
    @pl.when(pl.program_id(0) == 0)
    def _():
        carry_ref[...] = jnp.zeros_like(carry_ref)

    oh = oh_ref[...]
    rank = jnp.dot(lt_ref[...], oh, preferred_element_type=F32) + carry_ref[...] + base_ref[...]
    ohf = oh.astype(F32)
    val = ohf * rank
    carry_ref[...] += jnp.sum(ohf, axis=0, keepdims=True)
    hi = jnp.floor(val * (1.0 / 256.0))
    lo = val - 256.0 * hi
    ones = jnp.ones((8, LANES), BF16)
    pos = (256.0 * lax.dot_general(ones, hi.astype(BF16), _NT, preferred_element_type=F32)
           + lax.dot_general(ones, lo.astype(BF16), _NT, preferred_element_type=F32))
    pos_ref[0] = pos[0:1].astype(jnp.int32)


def _moe_pos(onehot, base, tm=MOE_TOK_TILE):
    n = onehot.shape[0]
    lt = jnp.asarray(np.tril(np.ones((tm, tm), np.float32), -1), BF16)
    return pl.pallas_call(
        _moe_pos_kernel,
        grid=(n // tm,),
        in_specs=[pl.BlockSpec((tm, LANES), lambda i: (i, 0)), pl.BlockSpec((1, LANES), lambda i: (0, 0)),
                  pl.BlockSpec((tm, tm), lambda i: (0, 0))],
        out_specs=pl.BlockSpec((1, 1, tm), lambda i: (i, 0, 0)),
        out_shape=jax.ShapeDtypeStruct((n // tm, 1, tm), jnp.int32),
        scratch_shapes=[pltpu.VMEM((1, LANES), F32)],
        compiler_params=_params(("arbitrary",), 32),
        name="moe_pos",
    )(onehot, base, lt)


def _tile_copy(src, src_tok, dst, dst_tok, sem):
    rows = TOKEN_TILE_ROWS
    return pltpu.make_async_copy(src.at[pl.ds(pl.multiple_of(src_tok * rows, rows), rows)],
                                 dst.at[pl.ds(pl.multiple_of(dst_tok * rows, rows), rows)], sem)


def _meta_copy(src, src_tok, dst, dst_tok, sem):
    return pltpu.make_async_copy(src.at[pl.ds(src_tok, 1)], dst.at[pl.ds(dst_tok, 1)], sem)


def _moe_scatter_kernel(pos_ref, xt_ref, meta_ref, xs_in_ref, xm_in_ref, xs_ref, xm_ref, sem, *, tm):
    del xs_in_ref, xm_in_ref

    def start(r2, c):
        for par in range(2):
            r = 2 * r2 + par
            p = pos_ref[0, 0, r]
            _tile_copy(xt_ref, r, xs_ref, p, sem.at[0]).start(priority=par)
            _meta_copy(meta_ref, r, xm_ref, p, sem.at[1]).start(priority=1 - par)
        return c

    lax.fori_loop(0, tm // 2, start, 0, unroll=4)
    pltpu.make_async_copy(xt_ref, xs_ref.at[pl.ds(0, tm * TOKEN_TILE_ROWS)], sem.at[0]).wait()
    pltpu.make_async_copy(meta_ref, xm_ref.at[pl.ds(0, tm)], sem.at[1]).wait()


def _moe_scatter(pos, xt, meta, xs_buf, xm_buf, tm=MOE_TOK_TILE):
    n = meta.shape[0]
    return pl.pallas_call(
        functools.partial(_moe_scatter_kernel, tm=tm),
        grid=(n // tm,),
        in_specs=[pl.BlockSpec((1, 1, tm), lambda i: (i, 0, 0), memory_space=pltpu.SMEM),
                  pl.BlockSpec((tm * TOKEN_TILE_ROWS, LANES), lambda i: (i, 0)),
                  pl.BlockSpec((tm, LANES), lambda i: (i, 0)),
                  pl.BlockSpec(memory_space=pl.ANY), pl.BlockSpec(memory_space=pl.ANY)],
        out_specs=[pl.BlockSpec(memory_space=pl.ANY), pl.BlockSpec(memory_space=pl.ANY)],
        out_shape=[jax.ShapeDtypeStruct(xs_buf.shape, xs_buf.dtype), jax.ShapeDtypeStruct(xm_buf.shape, xm_buf.dtype)],
        scratch_shapes=[pltpu.SemaphoreType.DMA((2,))],
        input_output_aliases={3: 0, 4: 1},
        compiler_params=_params(("arbitrary",), 32),
        name="moe_scatter",
    )(pos, xt, meta, xs_buf, xm_buf)


def _moe_experts_kernel(grp_ref, e1_ref, e2_ref, nused_ref, xs_ref, xm_ref, wg_ref, wu_ref, wd_ref, ys_in_ref, ys_ref):
    del grp_ref, ys_in_ref
    i = pl.program_id(0)
    tr = xm_ref.shape[0] // MOE_TILES_PER_STEP

    @pl.when(i * MOE_TILES_PER_STEP < nused_ref[0])
    def _():
        for t in range(MOE_TILES_PER_STEP):
            x = _load_token_tiles(xs_ref, tr, t * tr * TOKEN_TILE_ROWS).astype(BF16)
            meta = xm_ref[t * tr:(t + 1) * tr, :]
            y = None
            for k, e_ref in enumerate((e1_ref, e2_ref)):
                e = e_ref[i * MOE_TILES_PER_STEP + t]
                hg = jnp.dot(x, wg_ref[0, e], preferred_element_type=F32)
                hu = jnp.dot(x, wu_ref[0, e], preferred_element_type=F32)
                h = hg * _sigmoid(hg) * hu * meta[:, k:k + 1]
                yk = jnp.dot(h.astype(BF16), wd_ref[0, e], preferred_element_type=F32)
                y = yk if y is None else y + yk
            _store_token_tiles(ys_ref, y, t * tr * TOKEN_TILE_ROWS)


def _moe_experts(xs_buf, xm_buf, ys_buf, tile_grp, tile_e1, tile_e2, n_used, w_gate, w_up, w_down, layer,
                 tr=MOE_ROW_TILE):
    r_max = xm_buf.shape[0]
    d, f = w_gate.shape[-2:]
    assert d == TOKEN_TILE_ROWS * LANES
    tps = MOE_TILES_PER_STEP
    step_rows = tps * tr
    tile_rows = step_rows * TOKEN_TILE_ROWS
    epg = EXPERTS_PER_GROUP
    used = lambda i, grp, e1, e2, nu: (jnp.minimum(i, nu[0] // tps - 1), 0)
    wmap = lambda i, grp, e1, e2, nu: (layer * N_GROUPS + grp[i * tps], 0, 0, 0)
    grid_spec = pltpu.PrefetchScalarGridSpec(
        num_scalar_prefetch=4,
        grid=(r_max // step_rows,),
        in_specs=[pl.BlockSpec((tile_rows, LANES), used), pl.BlockSpec((step_rows, LANES), used),
                  pl.BlockSpec((1, epg, d, f), wmap), pl.BlockSpec((1, epg, d, f), wmap),
                  pl.BlockSpec((1, epg, f, d), wmap), pl.BlockSpec(memory_space=pl.ANY)],
        out_specs=pl.BlockSpec((tile_rows, LANES), used),
    )
    return pl.pallas_call(
        _moe_experts_kernel,
        grid_spec=grid_spec,
        out_shape=jax.ShapeDtypeStruct(ys_buf.shape, ys_buf.dtype),
        input_output_aliases={9: 0},
        compiler_params=_params(("arbitrary",), 48),
        name="moe_experts",
    )(tile_grp, tile_e1, tile_e2, n_used, xs_buf, xm_buf, w_gate, w_up, w_down, ys_buf)


def _moe_combine_ln_kernel(pos_ref, nxt_ref, x_ref, ys_ref, g_ref, b_ref, y_ref, buf_ref, sem, *, alpha, tm):
    i = pl.program_id(0)
    slot = i % 2

    def gather(p_ref, s):
        def start(r2, c):
            for par in range(2):
                r = 2 * r2 + par
                _tile_copy(ys_ref, p_ref[0, 0, r], buf_ref, s * tm + r, sem.at[s]).start(priority=par)
            return c
        lax.fori_loop(0, tm // 2, start, 0, unroll=4)

    @pl.when(i == 0)
    def _():
        gather(pos_ref, 0)

    @pl.when(i + 1 < pl.num_programs(0))
    def _():
        gather(nxt_ref, 1 - slot)

    base = pl.multiple_of(slot * (tm * TOKEN_TILE_ROWS), tm * TOKEN_TILE_ROWS)
    slot_rows = pl.ds(base, tm * TOKEN_TILE_ROWS)
    pltpu.make_async_copy(ys_ref.at[pl.ds(0, tm * TOKEN_TILE_ROWS)], buf_ref.at[slot_rows], sem.at[slot]).wait()
    ffn = _load_token_tiles(buf_ref, tm, base)
    y_ref[...] = _layer_norm(alpha * x_ref[...] + ffn, g_ref[...], b_ref[...])


def _moe_combine_ln(pos, x, ys_buf, g, b, alpha, tm=MOE_TOK_TILE):
    n, d = x.shape
    nt = n // tm
    row = lambda i: (i, 0)
    fixed = lambda i: (0, 0)
    return pl.pallas_call(
        functools.partial(_moe_combine_ln_kernel, alpha=alpha, tm=tm),
        grid=(nt,),
        in_specs=[pl.BlockSpec((1, 1, tm), lambda i: (i, 0, 0), memory_space=pltpu.SMEM),
                  pl.BlockSpec((1, 1, tm), lambda i: (jnp.minimum(i + 1, nt - 1), 0, 0), memory_space=pltpu.SMEM),
                  pl.BlockSpec((tm, d), row), pl.BlockSpec(memory_space=pl.ANY),
                  pl.BlockSpec((1, d), fixed), pl.BlockSpec((1, d), fixed)],
        out_specs=pl.BlockSpec((tm, d), row),
        out_shape=jax.ShapeDtypeStruct((n, d), F32),
        scratch_shapes=[pltpu.VMEM((2 * tm * TOKEN_TILE_ROWS, LANES), F32), pltpu.SemaphoreType.DMA((2,))],
        compiler_params=_params(("arbitrary",), 32),
        name="moe_combine_ln",
    )(pos, pos, x, ys_buf, g, b)


def _bucket_experts():
    lo, hi = [], []
    for a in range(EXPERTS_PER_GROUP):
        for c in range(a + 1, EXPERTS_PER_GROUP):
            lo.append(a)
            hi.append(c)
    grp = np.repeat(np.arange(N_GROUPS), MOE_PAIRS).astype(np.int32)
    return grp, np.tile(lo, N_GROUPS).astype(np.int32), np.tile(hi, N_GROUPS).astype(np.int32)


def _moe_layer(x, bufs, w_group, b_group, w_expert, b_expert, w_gate, w_up, w_down, layer, g, b, alpha):
    tr = MOE_ROW_TILE
    xs_buf, xm_buf, ys_buf = bufs
    xt, meta, onehot, cnt = _router(x, w_group, b_group, w_expert, b_expert)
    counts = cnt[0, :MOE_BUCKETS].astype(jnp.int32)
    padded = ((counts + (tr - 1)) // tr * tr).reshape(N_GROUPS, MOE_PAIRS)
    step_rows = MOE_TILES_PER_STEP * tr
    slack = -jnp.sum(padded, axis=1) % step_rows
    padded = padded.at[:, MOE_PAIRS - 1].add(slack).reshape(MOE_BUCKETS)
    ends = jnp.cumsum(padded)
    base = jnp.zeros((1, LANES), F32).at[0, :MOE_BUCKETS].set((ends - padded).astype(F32))
    n_used = ends[-1] // tr
    n_tiles = xm_buf.shape[0] // tr
    tile = jnp.minimum(jnp.arange(n_tiles, dtype=jnp.int32), n_used - 1)
    tile_bkt = jnp.sum((ends[None, :] <= (tile * tr)[:, None]).astype(jnp.int32), axis=1)
    tile_bkt = jnp.minimum(tile_bkt, MOE_BUCKETS - 1)
    b_grp, b_lo, b_hi = (jnp.asarray(t)[tile_bkt] for t in _bucket_experts())
    pos = _moe_pos(onehot, base)
    xs_buf, xm_buf = _moe_scatter(pos, xt, meta, xs_buf, xm_buf)
    ys_buf = _moe_experts(xs_buf, xm_buf, ys_buf, b_grp, b_lo, b_hi, n_used.reshape(1).astype(jnp.int32),
                          w_gate, w_up, w_down, layer)
    return _moe_combine_ln(pos, x, ys_buf, g, b, alpha), (xs_buf, xm_buf, ys_buf)


def kernel(x, ln_mix_g, ln_mix_b, ln_ffn_g, ln_ffn_b, fox_w_in, fox_b_f, fox_w_out, hgrn_w_in, hgrn_lb_logits, hgrn_norm_g, hgrn_w_out, conv_w_in, conv_w, conv_w_out, moe_w_group, moe_b_group, moe_w_expert, moe_b_expert, moe_w_gate, moe_w_up, moe_w_down):
    bsz, t_len, d = x.shape
    depth = ln_mix_g.shape[0]
    alpha = float((2 * depth) ** 0.25)
    assert d == FOX_HEADS * FOX_HEAD_DIM == HGRN_HEADS * HGRN_DIM
    assert t_len % 512 == 0 and (bsz * t_len) % 1024 == 0

    lb_prob = jax.nn.softmax(hgrn_lb_logits.astype(F32), axis=0)
    lower_bounds = jnp.cumsum(lb_prob, axis=0) - lb_prob[0]

    h = x.reshape(bsz * t_len, d)
    r_max = bsz * t_len + MOE_BUCKETS * MOE_ROW_TILE + N_GROUPS * MOE_TILES_PER_STEP * MOE_ROW_TILE
    assert r_max % (MOE_TILES_PER_STEP * MOE_ROW_TILE) == 0
    by_group = lambda w: w.astype(BF16).reshape((depth * N_GROUPS, EXPERTS_PER_GROUP) + w.shape[2:])
    moe_w_gate, moe_w_up, moe_w_down = by_group(moe_w_gate), by_group(moe_w_up), by_group(moe_w_down)
    bufs = (jnp.zeros((r_max * TOKEN_TILE_ROWS, LANES), F32), jnp.zeros((r_max, LANES), F32),
            jnp.zeros((r_max * TOKEN_TILE_ROWS, LANES), F32))
    for layer in range(depth):
        kind, j = layer % 3, layer // 3
        g_mix, b_mix = ln_mix_g[layer].reshape(1, d), ln_mix_b[layer].reshape(1, d)
        if kind == 0:
            h = _fox_layer(h, fox_w_in[j], fox_b_f[j], fox_w_out[j], g_mix, b_mix, alpha, bsz, t_len)
        elif kind == 1:
            h = _hgrn_layer(h, hgrn_w_in[j], lower_bounds[layer], hgrn_norm_g[j], hgrn_w_out[j],
                            g_mix, b_mix, alpha, bsz, t_len)
        else:
            h = _conv_layer(h, conv_w_in[j], conv_w[j], conv_w_out[j], g_mix, b_mix, alpha, bsz, t_len)
        h, bufs = _moe_layer(
            h, bufs, moe_w_group[layer], moe_b_group[layer], moe_w_expert[layer], moe_b_expert[layer],
            moe_w_gate, moe_w_up, moe_w_down, layer,
            ln_ffn_g[layer].reshape(1, d), ln_ffn_b[layer].reshape(1, d), alpha)
    return h.reshape(bsz, t_len, d)
```

```python
import functools

import numpy as np
import jax
import jax.numpy as jnp
from jax import lax
from jax.experimental import pallas as pl
from jax.experimental.pallas import tpu as pltpu

F32 = jnp.float32
BF16 = jnp.bfloat16

FOX_HEADS = 16
FOX_HEAD_DIM = 64
FOX_PAIRS_PER_STEP = 4
HGRN_HEADS = 8
HGRN_DIM = 128
HGRN_CHUNK = 128
HGRN_SMALL_LEVEL = 8
N_GROUPS = 4
EXPERTS_PER_GROUP = 8
N_EXPERTS = N_GROUPS * EXPERTS_PER_GROUP
LN_EPS = 1e-5
RMS_EPS = 1e-6
LOG2E = 1.4426950408889634
LANES = 128
ROUTER_GROUP_LANE0 = N_EXPERTS
MOE_PAIRS = EXPERTS_PER_GROUP * (EXPERTS_PER_GROUP - 1) // 2
MOE_BUCKETS = N_GROUPS * MOE_PAIRS
MOE_ROW_TILE = 128
MOE_TILES_PER_STEP = 4
MOE_TOK_TILE = 256
TOKEN_TILE_ROWS = 8

_NT = (((1,), (1,)), ((), ()))
_TN = (((0,), (0,)), ((), ()))


def _params(semantics, vmem_mb):
    return pltpu.CompilerParams(dimension_semantics=semantics,
                                vmem_limit_bytes=vmem_mb * 1024 * 1024)


def _layer_norm(v, g, b):
    mu = jnp.mean(v, axis=-1, keepdims=True)
    d = v - mu
    var = jnp.mean(d * d, axis=-1, keepdims=True)
    return d * lax.rsqrt(var + LN_EPS) * g + b


def _sigmoid(v):
    return 1.0 / (1.0 + jnp.exp(-v))


def _proj_kernel(x_ref, *refs, n_out):
    xb = x_ref[...].astype(BF16)
    for w_ref, o_ref in zip(refs[:n_out], refs[n_out:]):
        o_ref[...] = jnp.dot(xb, w_ref[...], preferred_element_type=F32).astype(o_ref.dtype)


def _proj(x, ws, dtypes, name, tm=512):
    n, k = x.shape
    return pl.pallas_call(
        functools.partial(_proj_kernel, n_out=len(ws)),
        grid=(n // tm,),
        in_specs=[pl.BlockSpec((tm, k), lambda i: (i, 0))]
        + [pl.BlockSpec(w.shape, lambda i: (0, 0)) for w in ws],
        out_specs=[pl.BlockSpec((tm, w.shape[1]), lambda i: (i, 0)) for w in ws],
        out_shape=[jax.ShapeDtypeStruct((n, w.shape[1]), dt) for w, dt in zip(ws, dtypes)],
        compiler_params=_params(("arbitrary",), 48),
        name=name,
    )(x, *ws)


def _outproj_ln_kernel(o_ref, w_ref, x_ref, g_ref, b_ref, y_ref, *, alpha, transposed):
    dims = _TN if transposed else (((1,), (0,)), ((), ()))
    mixed = lax.dot_general(o_ref[...], w_ref[...], dims, preferred_element_type=F32)
    y_ref[...] = _layer_norm(alpha * x_ref[...] + mixed, g_ref[...], b_ref[...])


def _outproj_ln(o, w, x, g, b, alpha, name, tm=512, transposed=False):
    n, d = x.shape
    k = w.shape[0]
    row = lambda i: (i, 0)
    fixed = lambda i: (0, 0)
    o_spec = pl.BlockSpec((k, tm), lambda i: (0, i)) if transposed else pl.BlockSpec((tm, k), row)
    return pl.pallas_call(
        functools.partial(_outproj_ln_kernel, alpha=alpha, transposed=transposed),
        grid=(n // tm,),
        in_specs=[o_spec, pl.BlockSpec((k, d), fixed),
                  pl.BlockSpec((tm, d), row), pl.BlockSpec((1, d), fixed), pl.BlockSpec((1, d), fixed)],
        out_specs=pl.BlockSpec((tm, d), row),
        out_shape=jax.ShapeDtypeStruct((n, d), F32),
        compiler_params=_params(("arbitrary",), 32),
        name=name,
    )(o, w, x, g, b)


def _split3(v):
    hi = v.astype(BF16)
    r1 = v - hi.astype(F32)
    mid = r1.astype(BF16)
    return hi, mid, (r1 - mid.astype(F32)).astype(BF16)


def _fox_proj_kernel(x_ref, wqt_ref, wk_ref, wvt_ref, qt_ref, k_ref, vt_ref):
    xb = x_ref[...].astype(BF16)
    qt_ref[...] = lax.dot_general(wqt_ref[...], xb, _NT, preferred_element_type=F32).astype(BF16)
    k_ref[...] = jnp.dot(xb, wk_ref[...], preferred_element_type=F32).astype(BF16)
    vt_ref[...] = lax.dot_general(wvt_ref[...], xb, _NT, preferred_element_type=F32).astype(BF16)


def _fox_proj(x, wqt, wk, wvt, tm=512):
    n, d = x.shape
    fixed = lambda i: (0, 0)
    return pl.pallas_call(
        _fox_proj_kernel,
        grid=(n // tm,),
        in_specs=[pl.BlockSpec((tm, d), lambda i: (i, 0)), pl.BlockSpec((d, d), fixed),
                  pl.BlockSpec((d, d), fixed), pl.BlockSpec((d, d), fixed)],
        out_specs=[pl.BlockSpec((d, tm), lambda i: (0, i)), pl.BlockSpec((tm, d), lambda i: (i, 0)),
                   pl.BlockSpec((d, tm), lambda i: (0, i))],
        out_shape=[jax.ShapeDtypeStruct((d, n), BF16), jax.ShapeDtypeStruct((n, d), BF16),
                   jax.ShapeDtypeStruct((d, n), BF16)],
        compiler_params=_params(("arbitrary",), 48),
        name="fox_proj",
    )(x, wqt, wk, wvt)


def _fox_gate_kernel(x_ref, wf_ref, bf_ref, tri_ref, sel_ref, cb_ref, carry_ref):
    @pl.when(pl.program_id(1) == 0)
    def _():
        carry_ref[...] = jnp.zeros_like(carry_ref)

    z = jnp.dot(x_ref[...].astype(BF16), wf_ref[...], preferred_element_type=F32) + bf_ref[...]
    logf = jnp.minimum(z, 0.0) - jnp.log(1.0 + jnp.exp(-jnp.abs(z)))
    tri = tri_ref[...]
    c = carry_ref[...]
    for part in _split3(logf):
        c = c + jnp.dot(tri, part, preferred_element_type=F32)
    carry_ref[...] = c[c.shape[0] - 1:, :]
    cb = None
    for j, part in enumerate(_split3(c * (-LOG2E))):
        term = jnp.dot(part, sel_ref[j], preferred_element_type=F32)
        cb = term if cb is None else cb + term
    cb_ref[...] = cb.astype(BF16)


def _fox_gate(x, wf, bf, bsz, t_len, tg=512):
    n, d = x.shape
    nt = t_len // tg
    tri = jnp.asarray(np.tril(np.ones((tg, tg), np.float32)), BF16)
    sel = np.zeros((3, LANES, d), np.float32)
    for h in range(FOX_HEADS):
        for j in range(3):
            sel[j, h, (h // 2) * LANES + 3 * (h % 2) + j] = 1.0
    return pl.pallas_call(
        _fox_gate_kernel,
        grid=(bsz, nt),
        in_specs=[pl.BlockSpec((tg, d), lambda b, i: (b * nt + i, 0)),
                  pl.BlockSpec((d, LANES), lambda b, i: (0, 0)),
                  pl.BlockSpec((1, LANES), lambda b, i: (0, 0)),
                  pl.BlockSpec((tg, tg), lambda b, i: (0, 0)),
                  pl.BlockSpec((3, LANES, d), lambda b, i: (0, 0, 0))],
        out_specs=pl.BlockSpec((tg, d), lambda b, i: (b * nt + i, 0)),
        out_shape=jax.ShapeDtypeStruct((n, d), BF16),
        scratch_shapes=[pltpu.VMEM((1, LANES), F32)],
        compiler_params=_params(("arbitrary", "arbitrary"), 32),
        name="fox_gate",
    )(x, wf, bf, tri, jnp.asarray(sel, BF16))


def _fox_attn_kernel(qt_ref, k_ref, cb_ref, vt_ref, ot_ref, s_ref, *, tq):
    qi = pl.program_id(2)
    n_heads = 2 * FOX_PAIRS_PER_STEP
    feat = lax.broadcasted_iota(jnp.int32, (LANES, tq), 0)
    rhs = []
    for h in range(n_heads):
        pair, hh = divmod(h, 2)
        qt = qt_ref[pair * LANES:(pair + 1) * LANES, :].astype(F32)
        own = (feat >= hh * FOX_HEAD_DIM) & (feat < (hh + 1) * FOX_HEAD_DIM)
        bias_rows = (feat >= 3 * hh) & (feat < 3 * hh + 3)
        rhs.append(jnp.concatenate([jnp.where(own, qt, 0.0).astype(BF16),
                                    jnp.where(bias_rows, 1.0, 0.0).astype(BF16)], axis=0))
    key_i = lax.broadcasted_iota(jnp.int32, (tq, tq), 0)
    qry_i = lax.broadcasted_iota(jnp.int32, (tq, tq), 1)
    ones_rows = jnp.ones((16, tq), BF16)

    def scores(j, slot):
        start = pl.multiple_of(j * tq, tq)
        for pair in range(FOX_PAIRS_PER_STEP):
            lanes = slice(pair * LANES, (pair + 1) * LANES)
            kext = jnp.concatenate([k_ref[pl.ds(start, tq), lanes], cb_ref[pl.ds(start, tq), lanes]], axis=1)
            for h in (2 * pair, 2 * pair + 1):
                s_ref[slot, h] = jnp.dot(kext, rhs[h], preferred_element_type=F32)

    def block(j, carry, slot, masked):
        start = pl.multiple_of(j * tq, tq)
        if not masked:
            scores(j + 1, 1 - slot)
        out = []
        for h in range(n_heads):
            m, l, acc = carry[h]
            st = s_ref[slot, h]
            if masked:
                st = jnp.where(key_i <= qry_i, st, -jnp.inf)
            m_new = jnp.maximum(m, jnp.max(st, axis=0, keepdims=True))
            p = jnp.exp2(st - m_new)
            a = jnp.exp2(m - m_new)
            vt = vt_ref[h * FOX_HEAD_DIM:(h + 1) * FOX_HEAD_DIM, pl.ds(start, tq)]
            pv = jnp.dot(jnp.concatenate([vt, ones_rows], axis=0), p.astype(BF16), preferred_element_type=F32)
            out.append((m_new, a * l + pv[FOX_HEAD_DIM:FOX_HEAD_DIM + 1], acc * a + pv[:FOX_HEAD_DIM]))
        return tuple(out)

    init = (jnp.full((1, tq), -jnp.inf, F32), jnp.zeros((1, tq), F32), jnp.zeros((FOX_HEAD_DIM, tq), F32))
    scores(0, 0)
    carry = lax.fori_loop(
        0, qi // 2, lambda i, c: block(2 * i + 1, block(2 * i, c, 0, False), 1, False), (init,) * n_heads)
    carry = lax.cond(
        qi % 2 == 0,
        lambda c: block(qi, c, 0, True),
        lambda c: block(qi, block(qi - 1, c, 0, False), 1, True),
        carry)
    for h in range(n_heads):
        _, l, acc = carry[h]
        ot_ref[h * FOX_HEAD_DIM:(h + 1) * FOX_HEAD_DIM, :] = (acc / l).astype(ot_ref.dtype)


def _fox_attn(qt, k, cb, vt, bsz, t_len, tq=256):
    d, n = qt.shape
    w = FOX_PAIRS_PER_STEP * LANES
    nq = t_len // tq
    return pl.pallas_call(
        functools.partial(_fox_attn_kernel, tq=tq),
        grid=(bsz, d // w, nq),
        in_specs=[pl.BlockSpec((w, tq), lambda b, p, i: (p, b * nq + i)),
                  pl.BlockSpec((t_len, w), lambda b, p, i: (b, p)),
                  pl.BlockSpec((t_len, w), lambda b, p, i: (b, p)),
                  pl.BlockSpec((w, t_len), lambda b, p, i: (p, b))],
        out_specs=pl.BlockSpec((w, tq), lambda b, p, i: (p, b * nq + i)),
        out_shape=jax.ShapeDtypeStruct((d, n), BF16),
        scratch_shapes=[pltpu.VMEM((2, 2 * FOX_PAIRS_PER_STEP, tq, tq), F32)],
        compiler_params=_params(("arbitrary", "arbitrary", "arbitrary"), 32),
        name="fox_attn",
    )(qt, k, cb, vt)


def _fox_layer(x, w_in, b_f, w_out, g, b, alpha, bsz, t_len):
    d = x.shape[1]
    wqt = (w_in[:, :d] * (FOX_HEAD_DIM ** -0.5 * LOG2E)).T.astype(BF16)
    wk = w_in[:, d:2 * d].astype(BF16)
    wvt = w_in[:, 2 * d:3 * d].T.astype(BF16)
    wf = jnp.zeros((d, LANES), F32).at[:, :FOX_HEADS].set(w_in[:, 3 * d:]).astype(BF16)
    bf = jnp.zeros((1, LANES), F32).at[0, :FOX_HEADS].set(b_f)
    qt, k, vt = _fox_proj(x, wqt, wk, wvt)
    cb = _fox_gate(x, wf, bf, bsz, t_len)
    ot = _fox_attn(qt, k, cb, vt, bsz, t_len)
    return _outproj_ln(ot, w_out.astype(BF16), x, g, b, alpha, "fox_out_ln", transposed=True)


def _hgrn_constants():
    c = HGRN_CHUNK
    r = np.arange(c)[:, None]
    j = np.arange(c)[None, :]
    blocks = [j <= r]
    masks = []
    levels = []
    half = c // 2
    while half >= 1:
        ref = (r // (2 * half)) * (2 * half) + half - 1
        upper = (r % (2 * half)) >= half
        if half < HGRN_SMALL_LEVEL:
            blocks.append(np.where(upper, (j > ref) & (j <= r), (j > r) & (j <= ref)))
        masks.append(((r // (2 * half)) == (j // (2 * half))) & upper & ((j % (2 * half)) < half))
        levels.append(half)
        half //= 2
    masks.append(r == j)
    wall = np.concatenate(blocks, axis=0).astype(np.float32)
    wall2 = np.concatenate([wall, wall], axis=1)
    return wall2, np.stack(masks).astype(np.float32), tuple(levels)


def _hgrn_kernel(q_ref, fl_ref, i_ref, g_ref, lb_ref, ng_ref, wall_ref, mask_ref, o_ref, s_ref, *, levels):
    c = HGRN_CHUNK

    @pl.when(pl.program_id(1) == 0)
    def _():
        s_ref[...] = jnp.zeros_like(s_ref)

    wall = wall_ref[...]
    rowi = lax.broadcasted_iota(jnp.int32, (c, HGRN_DIM), 0)
    e_pair = None
    for h in range(HGRN_HEADS):
        sl = slice(h * HGRN_DIM, (h + 1) * HGRN_DIM)
        if h % 2 == 0:
            sl2 = slice(h * HGRN_DIM, (h + 2) * HGRN_DIM)
            lb2 = lb_ref[:, sl2]
            f2 = lb2 + (1.0 - lb2) * _sigmoid(fl_ref[:, sl2])
            logf2 = jnp.log(f2)
            hi = logf2.astype(BF16)
            mid = (logf2 - hi.astype(F32)).astype(BF16)
            e_pair = jnp.dot(wall, jnp.concatenate([hi, mid], axis=0), preferred_element_type=F32)
        lane0 = (h % 2) * HGRN_DIM
        q = q_ref[:, sl].astype(F32)
        i_b = i_ref[:, sl]
        k = 1.0 - f2[:, lane0:lane0 + HGRN_DIM]
        e_all = e_pair[:, lane0:lane0 + HGRN_DIM]
        b = e_all[0:c]
        x_pre = jnp.exp(b)
        x_suf = jnp.exp(b[c - 1:c, :] - b)
        st = s_ref[h]
        o = lax.dot_general((q * x_pre).astype(BF16), st.astype(BF16), _NT, preferred_element_type=F32)
        upd = lax.dot_general(i_b, (k * x_suf).astype(BF16), _TN, preferred_element_type=F32)
        s_ref[h] = st * x_pre[c - 1:c, :] + upd
        a = jnp.zeros((c, c), F32)
        n_big = sum(half >= HGRN_SMALL_LEVEL for half in levels)
        for l, half in enumerate(levels):
            upper = (rowi & half) != 0
            if half >= HGRN_SMALL_LEVEL:
                parts = []
                for r0 in range(0, c, 2 * half):
                    ref = b[r0 + half - 1:r0 + half, :]
                    parts += [ref - b[r0:r0 + half], b[r0 + half:r0 + 2 * half] - ref]
                e_lvl = jnp.concatenate(parts, axis=0)
            else:
                e_lvl = e_all[(1 + l - n_big) * c:(2 + l - n_big) * c]
            z = (jnp.where(upper, q, k) * jnp.exp(e_lvl)).astype(BF16)
            a = a + lax.dot_general(z, z, _NT, preferred_element_type=F32) * mask_ref[l]
        a = a + lax.dot_general(q.astype(BF16), k.astype(BF16), _NT,
                                preferred_element_type=F32) * mask_ref[len(levels)]
        o = o + jnp.dot(a.astype(BF16), i_b, preferred_element_type=F32)
        o = o * lax.rsqrt(jnp.mean(o * o, axis=-1, keepdims=True) + RMS_EPS) * ng_ref[...]
        gate = g_ref[:, sl].astype(F32)
        o_ref[:, sl] = (o * (gate * _sigmoid(gate))).astype(o_ref.dtype)


def _hgrn_core(q, fl, i, g, lb, ng, bsz, t_len):
    n, d = q.shape
    c = HGRN_CHUNK
    nc = t_len // c
    wall2, masks, levels = _hgrn_constants()
    row = lambda b, j: (b * nc + j, 0)
    fixed2 = lambda b, j: (0, 0)
    return pl.pallas_call(
        functools.partial(_hgrn_kernel, levels=levels),
        grid=(bsz, nc),
        in_specs=[pl.BlockSpec((c, d), row), pl.BlockSpec((c, d), row), pl.BlockSpec((c, d), row),
                  pl.BlockSpec((c, d), row), pl.BlockSpec((1, d), fixed2), pl.BlockSpec((1, HGRN_DIM), fixed2),
                  pl.BlockSpec(wall2.shape, fixed2), pl.BlockSpec(masks.shape, lambda b, j: (0, 0, 0))],
        out_specs=pl.BlockSpec((c, d), row),
        out_shape=jax.ShapeDtypeStruct((n, d), BF16),
        scratch_shapes=[pltpu.VMEM((HGRN_HEADS, HGRN_DIM, HGRN_DIM), F32)],
        compiler_params=_params(("arbitrary", "arbitrary"), 32),
        name="hgrn_core",
    )(q, fl, i, g, lb, ng, jnp.asarray(wall2, BF16), jnp.asarray(masks, F32))


def _hgrn_layer(x, w_in, lower_bound, norm_g, w_out, g, b, alpha, bsz, t_len):
    d = x.shape[1]
    wb = w_in.astype(BF16)
    q, fl, i, gate = _proj(x, [wb[:, :d], wb[:, d:2 * d], wb[:, 2 * d:3 * d], wb[:, 3 * d:]],
                           [BF16, F32, BF16, BF16], "hgrn_proj")
    o = _hgrn_core(q, fl, i, gate, lower_bound.reshape(1, d), norm_g.reshape(1, HGRN_DIM), bsz, t_len)
    return _outproj_ln(o, w_out.astype(BF16), x, g, b, alpha, "hgrn_out_ln")


def _conv_kernel(x_ref, win_ref, cw_ref, wout_ref, g_ref, b_ref, y_ref, zbuf, *, alpha, tm):
    d = x_ref.shape[1]

    @pl.when(pl.program_id(1) == 0)
    def _():
        zbuf[0:8, :] = jnp.zeros((8, d), F32)

    x = x_ref[...]
    p = jnp.dot(x.astype(BF16), win_ref[...], preferred_element_type=F32)
    z = p[:, d:2 * d] * p[:, 2 * d:]
    zbuf[8:8 + tm, :] = z
    y = cw_ref[2:3, :] * z + cw_ref[1:2, :] * zbuf[7:7 + tm, :] + cw_ref[0:1, :] * zbuf[6:6 + tm, :]
    zbuf[0:8, :] = z[tm - 8:, :]
    mixed = jnp.dot((p[:, :d] * y).astype(BF16), wout_ref[...], preferred_element_type=F32)
    y_ref[...] = _layer_norm(alpha * x + mixed, g_ref[...], b_ref[...])


def _conv_layer(x, w_in, conv_w, w_out, g, b, alpha, bsz, t_len, tm=512):
    n, d = x.shape
    nt = t_len // tm
    row = lambda bb, i: (bb * nt + i, 0)
    fixed = lambda bb, i: (0, 0)
    return pl.pallas_call(
        functools.partial(_conv_kernel, alpha=alpha, tm=tm),
        grid=(bsz, nt),
        in_specs=[pl.BlockSpec((tm, d), row), pl.BlockSpec((d, 3 * d), fixed), pl.BlockSpec(conv_w.shape, fixed),
                  pl.BlockSpec((d, d), fixed), pl.BlockSpec((1, d), fixed), pl.BlockSpec((1, d), fixed)],
        out_specs=pl.BlockSpec((tm, d), row),
        out_shape=jax.ShapeDtypeStruct((n, d), F32),
        scratch_shapes=[pltpu.VMEM((tm + 8, d), F32)],
        compiler_params=_params(("arbitrary", "arbitrary"), 48),
        name="conv_layer",
    )(x, w_in.astype(BF16), conv_w, w_out.astype(BF16), g, b)


def _store_token_tiles(xt_ref, v, row0=0):
    rows = v.shape[0]
    for c in range(TOKEN_TILE_ROWS):
        xt_ref[pl.ds(row0 + c, rows, stride=TOKEN_TILE_ROWS), :] = v[:, c * LANES:(c + 1) * LANES]


def _load_token_tiles(xt_ref, rows, row0=0):
    return jnp.concatenate([xt_ref[pl.ds(row0 + c, rows, stride=TOKEN_TILE_ROWS), :]
                            for c in range(TOKEN_TILE_ROWS)], axis=1)


def _router_kernel(x_ref, wh_ref, wl_ref, b_ref, xt_ref, meta_ref, oh_ref, cnt_ref):
    @pl.when(pl.program_id(0) == 0)
    def _():
        cnt_ref[...] = jnp.zeros_like(cnt_ref)

    for r0 in range(0, x_ref.shape[0], MOE_TOK_TILE):
        x = x_ref[r0:r0 + MOE_TOK_TILE, :]
        meta, onehot = _route(x, wh_ref, wl_ref, b_ref)
        meta_ref[r0:r0 + MOE_TOK_TILE, :] = meta
        oh_ref[r0:r0 + MOE_TOK_TILE, :] = onehot.astype(BF16)
        _store_token_tiles(xt_ref, x, r0 * TOKEN_TILE_ROWS)
        cnt_ref[...] += jnp.sum(onehot, axis=0, keepdims=True)


def _route(x, wh_ref, wl_ref, b_ref):
    xh = x.astype(BF16)
    xl = (x - xh.astype(F32)).astype(BF16)
    wh = wh_ref[...]
    logits = (jnp.dot(xh, wh, preferred_element_type=F32) + jnp.dot(xh, wl_ref[...], preferred_element_type=F32)
              + jnp.dot(xl, wh, preferred_element_type=F32)) + b_ref[...]
    lane = lax.broadcasted_iota(jnp.int32, logits.shape, 1)
    lane_f = lane.astype(F32)
    neg = -jnp.inf
    far = float(LANES)
    is_g = (lane >= ROUTER_GROUP_LANE0) & (lane < ROUTER_GROUP_LANE0 + N_GROUPS)
    gl = jnp.where(is_g, logits, neg)
    gmax = jnp.max(gl, axis=-1, keepdims=True)
    g_sel = jnp.min(jnp.where(gl == gmax, lane_f, far), axis=-1, keepdims=True) - float(ROUTER_GROUP_LANE0)
    grp_w = 1.0 / jnp.sum(jnp.where(is_g, jnp.exp(logits - gmax), 0.0), axis=-1, keepdims=True)
    in_g = (lane < N_EXPERTS) & ((lane // EXPERTS_PER_GROUP).astype(F32) == g_sel)
    el = jnp.where(in_g, logits, neg)
    t1 = jnp.max(el, axis=-1, keepdims=True)
    i1 = jnp.min(jnp.where(el == t1, lane_f, far), axis=-1, keepdims=True)
    el2 = jnp.where(lane_f == i1, neg, el)
    t2 = jnp.max(el2, axis=-1, keepdims=True)
    i2 = jnp.min(jnp.where(el2 == t2, lane_f, far), axis=-1, keepdims=True)
    e2 = jnp.exp(t2 - t1)
    w1 = grp_w / (1.0 + e2)
    w2 = w1 * e2
    first_lo = i1 < i2
    lo = jnp.where(first_lo, i1, i2) - g_sel * EXPERTS_PER_GROUP
    hi = jnp.where(first_lo, i2, i1) - g_sel * EXPERTS_PER_GROUP
    pair = lo * (EXPERTS_PER_GROUP - 1.0) - lo * (lo - 1.0) * 0.5 + (hi - lo - 1.0)
    onehot = jnp.where(lane_f == g_sel * MOE_PAIRS + pair, 1.0, 0.0)
    meta = (jnp.where(lane == 0, jnp.where(first_lo, w1, w2), 0.0)
            + jnp.where(lane == 1, jnp.where(first_lo, w2, w1), 0.0))
    return meta, onehot


def _router(x, w_group, b_group, w_expert, b_expert, tm=2 * MOE_TOK_TILE):
    n, d = x.shape
    w = jnp.zeros((d, LANES), F32).at[:, :N_EXPERTS].set(w_expert)
    w = w.at[:, ROUTER_GROUP_LANE0:ROUTER_GROUP_LANE0 + N_GROUPS].set(w_group)
    bias = jnp.zeros((1, LANES), F32).at[0, :N_EXPERTS].set(b_expert)
    bias = bias.at[0, ROUTER_GROUP_LANE0:ROUTER_GROUP_LANE0 + N_GROUPS].set(b_group)
    wh = w.astype(BF16)
    wl = (w - wh.astype(F32)).astype(BF16)
    row = lambda i: (i, 0)
    fixed = lambda i: (0, 0)
    return pl.pallas_call(
        _router_kernel,
        grid=(n // tm,),
        in_specs=[pl.BlockSpec((tm, d), row), pl.BlockSpec((d, LANES), fixed), pl.BlockSpec((d, LANES), fixed),
                  pl.BlockSpec((1, LANES), fixed)],
        out_specs=[pl.BlockSpec((tm * TOKEN_TILE_ROWS, LANES), row), pl.BlockSpec((tm, LANES), row),
                   pl.BlockSpec((tm, LANES), row), pl.BlockSpec((8, LANES), fixed)],
        out_shape=[jax.ShapeDtypeStruct((n * TOKEN_TILE_ROWS, LANES), F32), jax.ShapeDtypeStruct((n, LANES), F32),
                   jax.ShapeDtypeStruct((n, LANES), BF16), jax.ShapeDtypeStruct((8, LANES), F32)],
        compiler_params=_params(("arbitrary",), 32),
        name="moe_router",
    )(x, wh, wl, bias)


def _moe_pos_kernel(oh_ref, base_ref, lt_ref, pos_ref, carry_ref):
    @pl.when(pl.program_id(0) == 0)
    def _():
        carry_ref[...] = jnp.zeros_like(carry_ref)

    oh = oh_ref[...]
    rank = jnp.dot(lt_ref[...], oh, preferred_element_type=F32) + carry_ref[...] + base_ref[...]
    ohf = oh.astype(F32)
    val = ohf * rank
    carry_ref[...] += jnp.sum(ohf, axis=0, keepdims=True)
    hi = jnp.floor(val * (1.0 / 256.0))
    lo = val - 256.0 * hi
    ones = jnp.ones((8, LANES), BF16)
    pos = (256.0 * lax.dot_general(ones, hi.astype(BF16), _NT, preferred_element_type=F32)
           + lax.dot_general(ones, lo.astype(BF16), _NT, preferred_element_type=F32))
    pos_ref[0] = pos[0:1].astype(jnp.int32)


def _moe_pos(onehot, base, tm=MOE_TOK_TILE):
    n = onehot.shape[0]
    lt = jnp.asarray(np.tril(np.ones((tm, tm), np.float32), -1), BF16)
    return pl.pallas_call(
        _moe_pos_kernel,
        grid=(n // tm,),
        in_specs=[pl.BlockSpec((tm, LANES), lambda i: (i, 0)), pl.BlockSpec((1, LANES), lambda i: (0, 0)),
                  pl.BlockSpec((tm, tm), lambda i: (0, 0))],
        out_specs=pl.BlockSpec((1, 1, tm), lambda i: (i, 0, 0)),
        out_shape=jax.ShapeDtypeStruct((n // tm, 1, tm), jnp.int32),
        scratch_shapes=[pltpu.VMEM((1, LANES), F32)],
        compiler_params=_params(("arbitrary",), 32),
        name="moe_pos",
    )(onehot, base, lt)


def _tile_copy(src, src_tok, dst, dst_tok, sem):
    rows = TOKEN_TILE_ROWS
    return pltpu.make_async_copy(src.at[pl.ds(pl.multiple_of(src_tok * rows, rows), rows)],
                                 dst.at[pl.ds(pl.multiple_of(dst_tok * rows, rows), rows)], sem)


def _meta_copy(src, src_tok, dst, dst_tok, sem):
    return pltpu.make_async_copy(src.at[pl.ds(src_tok, 1)], dst.at[pl.ds(dst_tok, 1)], sem)


def _moe_scatter_kernel(pos_ref, xt_ref, meta_ref, xs_in_ref, xm_in_ref, xs_ref, xm_ref, sem, *, tm):
    del xs_in_ref, xm_in_ref
    i = pl.program_id(0)
    slot = i % 2
    t0 = i * tm

    def start(r2, c):
        for par in range(2):
            r = 2 * r2 + par
            p = pos_ref[0, 0, r]
            _tile_copy(xt_ref, t0 + r, xs_ref, p, sem.at[slot, 0]).start(priority=par)
            _meta_copy(meta_ref, t0 + r, xm_ref, p, sem.at[slot, 1]).start(priority=1 - par)
        return c

    def wait(s):
        rows = tm * TOKEN_TILE_ROWS
        pltpu.make_async_copy(xt_ref.at[pl.ds(0, rows)], xs_ref.at[pl.ds(0, rows)], sem.at[s, 0]).wait()
        pltpu.make_async_copy(meta_ref.at[pl.ds(0, tm)], xm_ref.at[pl.ds(0, tm)], sem.at[s, 1]).wait()

    lax.fori_loop(0, tm // 2, start, 0, unroll=4)

    @pl.when(i > 0)
    def _():
        wait(1 - slot)

    @pl.when(i == pl.num_programs(0) - 1)
    def _():
        wait(slot)


def _moe_scatter(pos, xt, meta, xs_buf, xm_buf, tm=MOE_TOK_TILE):
    n = meta.shape[0]
    return pl.pallas_call(
        functools.partial(_moe_scatter_kernel, tm=tm),
        grid=(n // tm,),
        in_specs=[pl.BlockSpec((1, 1, tm), lambda i: (i, 0, 0), memory_space=pltpu.SMEM),
                  pl.BlockSpec(memory_space=pl.ANY), pl.BlockSpec(memory_space=pl.ANY),
                  pl.BlockSpec(memory_space=pl.ANY), pl.BlockSpec(memory_space=pl.ANY)],
        out_specs=[pl.BlockSpec(memory_space=pl.ANY), pl.BlockSpec(memory_space=pl.ANY)],
        out_shape=[jax.ShapeDtypeStruct(xs_buf.shape, xs_buf.dtype), jax.ShapeDtypeStruct(xm_buf.shape, xm_buf.dtype)],
        scratch_shapes=[pltpu.SemaphoreType.DMA((2, 2))],
        input_output_aliases={3: 0, 4: 1},
        compiler_params=_params(("arbitrary",), 32),
        name="moe_scatter",
    )(pos, xt, meta, xs_buf, xm_buf)


def _moe_experts_kernel(grp_ref, e1_ref, e2_ref, nused_ref, xs_ref, xm_ref, wg_ref, wu_ref, wd_ref, ys_in_ref, ys_ref):
    del grp_ref, ys_in_ref
    i = pl.program_id(0)
    tr = xm_ref.shape[0] // MOE_TILES_PER_STEP

    @pl.when(i * MOE_TILES_PER_STEP < nused_ref[0])
    def _():
        for t in range(MOE_TILES_PER_STEP):
            x = _load_token_tiles(xs_ref, tr, t * tr * TOKEN_TILE_ROWS).astype(BF16)
            meta = xm_ref[t * tr:(t + 1) * tr, :]
            y = None
            for k, e_ref in enumerate((e1_ref, e2_ref)):
                e = e_ref[i * MOE_TILES_PER_STEP + t]
                hg = jnp.dot(x, wg_ref[0, e], preferred_element_type=F32)
                hu = jnp.dot(x, wu_ref[0, e], preferred_element_type=F32)
                h = hg * _sigmoid(hg) * hu * meta[:, k:k + 1]
                yk = jnp.dot(h.astype(BF16), wd_ref[0, e], preferred_element_type=F32)
                y = yk if y is None else y + yk
            _store_token_tiles(ys_ref, y, t * tr * TOKEN_TILE_ROWS)


def _moe_experts(xs_buf, xm_buf, ys_buf, tile_grp, tile_e1, tile_e2, n_used, w_gate, w_up, w_down, layer,
                 tr=MOE_ROW_TILE):
    r_max = xm_buf.shape[0]
    d, f = w_gate.shape[-2:]
    assert d == TOKEN_TILE_ROWS * LANES
    tps = MOE_TILES_PER_STEP
    step_rows = tps * tr
    tile_rows = step_rows * TOKEN_TILE_ROWS
    epg = EXPERTS_PER_GROUP
    used = lambda i, grp, e1, e2, nu: (jnp.minimum(i, nu[0] // tps - 1), 0)
    wmap = lambda i, grp, e1, e2, nu: (layer * N_GROUPS + grp[i * tps], 0, 0, 0)
    grid_spec = pltpu.PrefetchScalarGridSpec(
        num_scalar_prefetch=4,
        grid=(r_max // step_rows,),
        in_specs=[pl.BlockSpec((tile_rows, LANES), used), pl.BlockSpec((step_rows, LANES), used),
                  pl.BlockSpec((1, epg, d, f), wmap), pl.BlockSpec((1, epg, d, f), wmap),
                  pl.BlockSpec((1, epg, f, d), wmap), pl.BlockSpec(memory_space=pl.ANY)],
        out_specs=pl.BlockSpec((tile_rows, LANES), used),
    )
    return pl.pallas_call(
        _moe_experts_kernel,
        grid_spec=grid_spec,
        out_shape=jax.ShapeDtypeStruct(ys_buf.shape, ys_buf.dtype),
        input_output_aliases={9: 0},
        compiler_params=_params(("arbitrary",), 48),
        name="moe_experts",
    )(tile_grp, tile_e1, tile_e2, n_used, xs_buf, xm_buf, w_gate, w_up, w_down, ys_buf)


def _moe_combine_ln_kernel(pos_ref, nxt_ref, x_ref, ys_ref, g_ref, b_ref, y_ref, buf_ref, sem, *, alpha, tm):
    i = pl.program_id(0)
    slot = i % 2
    rows = tm * TOKEN_TILE_ROWS

    def gather(p_ref, s, unroll):
        def start(r2, c):
            for par in range(2):
                r = 2 * r2 + par
                _tile_copy(ys_ref, p_ref[0, 0, r], buf_ref, s * tm + r, sem.at[s]).start(priority=par)
            return c
        lax.fori_loop(0, tm // 2, start, 0, unroll=unroll)

    def wait(s):
        dst = buf_ref.at[pl.ds(pl.multiple_of(s * rows, rows), rows)]
        pltpu.make_async_copy(ys_ref.at[pl.ds(0, rows)], dst, sem.at[s]).wait()

    @pl.when(i == 0)
    def _():
        gather(pos_ref, 0, 4)

    wait(slot)
    v = alpha * x_ref[...] + _load_token_tiles(buf_ref, tm, pl.multiple_of(slot * rows, rows))
    gather(nxt_ref, 1 - slot, True)
    y_ref[...] = _layer_norm(v, g_ref[...], b_ref[...])

    @pl.when(i == pl.num_programs(0) - 1)
    def _():
        wait(1 - slot)


def _moe_combine_ln(pos, x, ys_buf, g, b, alpha, tm=MOE_TOK_TILE):
    n, d = x.shape
    nt = n // tm
    row = lambda i: (i, 0)
    fixed = lambda i: (0, 0)
    return pl.pallas_call(
        functools.partial(_moe_combine_ln_kernel, alpha=alpha, tm=tm),
        grid=(nt,),
        in_specs=[pl.BlockSpec((1, 1, tm), lambda i: (i, 0, 0), memory_space=pltpu.SMEM),
                  pl.BlockSpec((1, 1, tm), lambda i: (jnp.minimum(i + 1, nt - 1), 0, 0), memory_space=pltpu.SMEM),
                  pl.BlockSpec((tm, d), row), pl.BlockSpec(memory_space=pl.ANY),
                  pl.BlockSpec((1, d), fixed), pl.BlockSpec((1, d), fixed)],
        out_specs=pl.BlockSpec((tm, d), row),
        out_shape=jax.ShapeDtypeStruct((n, d), F32),
        scratch_shapes=[pltpu.VMEM((2 * tm * TOKEN_TILE_ROWS, LANES), F32), pltpu.SemaphoreType.DMA((2,))],
        compiler_params=_params(("arbitrary",), 32),
        name="moe_combine_ln",
    )(pos, pos, x, ys_buf, g, b)


def _bucket_experts():
    lo, hi = [], []
    for a in range(EXPERTS_PER_GROUP):
        for c in range(a + 1, EXPERTS_PER_GROUP):
            lo.append(a)
            hi.append(c)
    grp = np.repeat(np.arange(N_GROUPS), MOE_PAIRS).astype(np.int32)
    return grp, np.tile(lo, N_GROUPS).astype(np.int32), np.tile(hi, N_GROUPS).astype(np.int32)


def _moe_layer(x, bufs, w_group, b_group, w_expert, b_expert, w_gate, w_up, w_down, layer, g, b, alpha):
    tr = MOE_ROW_TILE
    xs_buf, xm_buf, ys_buf = bufs
    xt, meta, onehot, cnt = _router(x, w_group, b_group, w_expert, b_expert)
    counts = cnt[0, :MOE_BUCKETS].astype(jnp.int32)
    padded = ((counts + (tr - 1)) // tr * tr).reshape(N_GROUPS, MOE_PAIRS)
    step_rows = MOE_TILES_PER_STEP * tr
    slack = -jnp.sum(padded, axis=1) % step_rows
    padded = padded.at[:, MOE_PAIRS - 1].add(slack).reshape(MOE_BUCKETS)
    ends = jnp.cumsum(padded)
    base = jnp.zeros((1, LANES), F32).at[0, :MOE_BUCKETS].set((ends - padded).astype(F32))
    n_used = ends[-1] // tr
    n_tiles = xm_buf.shape[0] // tr
    tile = jnp.minimum(jnp.arange(n_tiles, dtype=jnp.int32), n_used - 1)
    tile_bkt = jnp.sum((ends[None, :] <= (tile * tr)[:, None]).astype(jnp.int32), axis=1)
    tile_bkt = jnp.minimum(tile_bkt, MOE_BUCKETS - 1)
    b_grp, b_lo, b_hi = (jnp.asarray(t)[tile_bkt] for t in _bucket_experts())
    pos = _moe_pos(onehot, base)
    xs_buf, xm_buf = _moe_scatter(pos, xt, meta, xs_buf, xm_buf)
    ys_buf = _moe_experts(xs_buf, xm_buf, ys_buf, b_grp, b_lo, b_hi, n_used.reshape(1).astype(jnp.int32),
                          w_gate, w_up, w_down, layer)
    return _moe_combine_ln(pos, x, ys_buf, g, b, alpha), (xs_buf, xm_buf, ys_buf)


def kernel(x, ln_mix_g, ln_mix_b, ln_ffn_g, ln_ffn_b, fox_w_in, fox_b_f, fox_w_out, hgrn_w_in, hgrn_lb_logits, hgrn_norm_g, hgrn_w_out, conv_w_in, conv_w, conv_w_out, moe_w_group, moe_b_group, moe_w_expert, moe_b_expert, moe_w_gate, moe_w_up, moe_w_down):
    bsz, t_len, d = x.shape
    depth = ln_mix_g.shape[0]
    alpha = float((2 * depth) ** 0.25)
    assert d == FOX_HEADS * FOX_HEAD_DIM == HGRN_HEADS * HGRN_DIM
    assert t_len % 512 == 0 and (bsz * t_len) % 1024 == 0

    lb_prob = jax.nn.softmax(hgrn_lb_logits.astype(F32), axis=0)
    lower_bounds = jnp.cumsum(lb_prob, axis=0) - lb_prob[0]

    h = x.reshape(bsz * t_len, d)
    r_max = bsz * t_len + MOE_BUCKETS * MOE_ROW_TILE + N_GROUPS * MOE_TILES_PER_STEP * MOE_ROW_TILE
    assert r_max % (MOE_TILES_PER_STEP * MOE_ROW_TILE) == 0
    by_group = lambda w: w.astype(BF16).reshape((depth * N_GROUPS, EXPERTS_PER_GROUP) + w.shape[2:])
    moe_w_gate, moe_w_up, moe_w_down = by_group(moe_w_gate), by_group(moe_w_up), by_group(moe_w_down)
    bufs = (jnp.zeros((r_max * TOKEN_TILE_ROWS, LANES), F32), jnp.zeros((r_max, LANES), F32),
            jnp.zeros((r_max * TOKEN_TILE_ROWS, LANES), F32))
    for layer in range(depth):
        kind, j = layer % 3, layer // 3
        g_mix, b_mix = ln_mix_g[layer].reshape(1, d), ln_mix_b[layer].reshape(1, d)
        if kind == 0:
            h = _fox_layer(h, fox_w_in[j], fox_b_f[j], fox_w_out[j], g_mix, b_mix, alpha, bsz, t_len)
        elif kind == 1:
            h = _hgrn_layer(h, hgrn_w_in[j], lower_bounds[layer], hgrn_norm_g[j], hgrn_w_out[j],
                            g_mix, b_mix, alpha, bsz, t_len)
        else:
            h = _conv_layer(h, conv_w_in[j], conv_w[j], conv_w_out[j], g_mix, b_mix, alpha, bsz, t_len)
        h, bufs = _moe_layer(
            h, bufs, moe_w_group[layer], moe_b_group[layer], moe_w_expert[layer], moe_b_expert[layer],
            moe_w_gate, moe_w_up, moe_w_down, layer,
            ln_ffn_g[layer].reshape(1, d), ln_ffn_b[layer].reshape(1, d), alpha)
    return h.reshape(bsz, t_len, d)
```

```python
import functools

import numpy as np
import jax
import jax.numpy as jnp
from jax import lax
from jax.experimental import pallas as pl
from jax.experimental.pallas import tpu as pltpu

F32 = jnp.float32
BF16 = jnp.bfloat16

FOX_HEADS = 16
FOX_HEAD_DIM = 64
FOX_PAIRS_PER_STEP = 4
HGRN_HEADS = 8
HGRN_DIM = 128
HGRN_CHUNK = 128
HGRN_SMALL_LEVEL = 8
N_GROUPS = 4
EXPERTS_PER_GROUP = 8
N_EXPERTS = N_GROUPS * EXPERTS_PER_GROUP
LN_EPS = 1e-5
RMS_EPS = 1e-6
LOG2E = 1.4426950408889634
LANES = 128
ROUTER_GROUP_LANE0 = N_EXPERTS
MOE_PAIRS = EXPERTS_PER_GROUP * (EXPERTS_PER_GROUP - 1) // 2
MOE_BUCKETS = N_GROUPS * MOE_PAIRS
MOE_ROW_TILE = 128
MOE_TILES_PER_STEP = 4
MOE_TOK_TILE = 256
TOKEN_TILE_ROWS = 8

_NT = (((1,), (1,)), ((), ()))
_TN = (((0,), (0,)), ((), ()))


def _params(semantics, vmem_mb):
    return pltpu.CompilerParams(dimension_semantics=semantics,
                                vmem_limit_bytes=vmem_mb * 1024 * 1024)


def _layer_norm(v, g, b):
    mu = jnp.mean(v, axis=-1, keepdims=True)
    d = v - mu
    var = jnp.mean(d * d, axis=-1, keepdims=True)
    return d * lax.rsqrt(var + LN_EPS) * g + b


def _sigmoid(v):
    return 1.0 / (1.0 + jnp.exp(-v))


def _proj_kernel(x_ref, *refs, n_out):
    xb = x_ref[...].astype(BF16)
    for w_ref, o_ref in zip(refs[:n_out], refs[n_out:]):
        o_ref[...] = jnp.dot(xb, w_ref[...], preferred_element_type=F32).astype(o_ref.dtype)


def _proj(x, ws, dtypes, name, tm=512):
    n, k = x.shape
    return pl.pallas_call(
        functools.partial(_proj_kernel, n_out=len(ws)),
        grid=(n // tm,),
        in_specs=[pl.BlockSpec((tm, k), lambda i: (i, 0))]
        + [pl.BlockSpec(w.shape, lambda i: (0, 0)) for w in ws],
        out_specs=[pl.BlockSpec((tm, w.shape[1]), lambda i: (i, 0)) for w in ws],
        out_shape=[jax.ShapeDtypeStruct((n, w.shape[1]), dt) for w, dt in zip(ws, dtypes)],
        compiler_params=_params(("arbitrary",), 48),
        name=name,
    )(x, *ws)


def _outproj_ln_kernel(o_ref, w_ref, x_ref, g_ref, b_ref, *refs, alpha, transposed):
    y_ref = refs[3]
    dims = _TN if transposed else (((1,), (0,)), ((), ()))
    mixed = lax.dot_general(o_ref[...], w_ref[...], dims, preferred_element_type=F32)
    y = _layer_norm(alpha * x_ref[...] + mixed, g_ref[...], b_ref[...])
    y_ref[...] = y
    _route_rows(y, pl.program_id(0) == 0, *refs[:3], *refs[4:])


def _outproj_ln(o, w, x, g, b, router_w, alpha, name, tm=512, transposed=False):
    n, d = x.shape
    k = w.shape[0]
    row = lambda i: (i, 0)
    fixed = lambda i: (0, 0)
    o_spec = pl.BlockSpec((k, tm), lambda i: (0, i)) if transposed else pl.BlockSpec((tm, k), row)
    r_in, r_out, r_shape = _route_specs(n, d, tm, row, fixed)
    y, *routed = pl.pallas_call(
        functools.partial(_outproj_ln_kernel, alpha=alpha, transposed=transposed),
        grid=(n // tm,),
        in_specs=[o_spec, pl.BlockSpec((k, d), fixed),
                  pl.BlockSpec((tm, d), row), pl.BlockSpec((1, d), fixed), pl.BlockSpec((1, d), fixed)] + r_in,
        out_specs=[pl.BlockSpec((tm, d), row)] + r_out,
        out_shape=[jax.ShapeDtypeStruct((n, d), F32)] + r_shape,
        compiler_params=_params(("arbitrary",), 48),
        name=name,
    )(o, w, x, g, b, *router_w)
    return y, routed


def _split3(v):
    hi = v.astype(BF16)
    r1 = v - hi.astype(F32)
    mid = r1.astype(BF16)
    return hi, mid, (r1 - mid.astype(F32)).astype(BF16)


def _fox_proj_kernel(x_ref, wqt_ref, wk_ref, wvt_ref, qt_ref, k_ref, vt_ref):
    xb = x_ref[...].astype(BF16)
    qt_ref[...] = lax.dot_general(wqt_ref[...], xb, _NT, preferred_element_type=F32).astype(BF16)
    k_ref[...] = jnp.dot(xb, wk_ref[...], preferred_element_type=F32).astype(BF16)
    vt_ref[...] = lax.dot_general(wvt_ref[...], xb, _NT, preferred_element_type=F32).astype(BF16)


def _fox_proj(x, wqt, wk, wvt, tm=512):
    n, d = x.shape
    fixed = lambda i: (0, 0)
    return pl.pallas_call(
        _fox_proj_kernel,
        grid=(n // tm,),
        in_specs=[pl.BlockSpec((tm, d), lambda i: (i, 0)), pl.BlockSpec((d, d), fixed),
                  pl.BlockSpec((d, d), fixed), pl.BlockSpec((d, d), fixed)],
        out_specs=[pl.BlockSpec((d, tm), lambda i: (0, i)), pl.BlockSpec((tm, d), lambda i: (i, 0)),
                   pl.BlockSpec((d, tm), lambda i: (0, i))],
        out_shape=[jax.ShapeDtypeStruct((d, n), BF16), jax.ShapeDtypeStruct((n, d), BF16),
                   jax.ShapeDtypeStruct((d, n), BF16)],
        compiler_params=_params(("arbitrary",), 48),
        name="fox_proj",
    )(x, wqt, wk, wvt)


def _fox_gate_kernel(x_ref, wf_ref, bf_ref, tri_ref, sel_ref, cb_ref, carry_ref):
    @pl.when(pl.program_id(1) == 0)
    def _():
        carry_ref[...] = jnp.zeros_like(carry_ref)

    z = jnp.dot(x_ref[...].astype(BF16), wf_ref[...], preferred_element_type=F32) + bf_ref[...]
    logf = jnp.minimum(z, 0.0) - jnp.log(1.0 + jnp.exp(-jnp.abs(z)))
    tri = tri_ref[...]
    c = carry_ref[...]
    for part in _split3(logf):
        c = c + jnp.dot(tri, part, preferred_element_type=F32)
    carry_ref[...] = c[c.shape[0] - 1:, :]
    cb = None
    for j, part in enumerate(_split3(c * (-LOG2E))):
        term = jnp.dot(part, sel_ref[j], preferred_element_type=F32)
        cb = term if cb is None else cb + term
    cb_ref[...] = cb.astype(BF16)


def _fox_gate(x, wf, bf, bsz, t_len, tg=512):
    n, d = x.shape
    nt = t_len // tg
    tri = jnp.asarray(np.tril(np.ones((tg, tg), np.float32)), BF16)
    sel = np.zeros((3, LANES, d), np.float32)
    for h in range(FOX_HEADS):
        for j in range(3):
            sel[j, h, (h // 2) * LANES + 3 * (h % 2) + j] = 1.0
    return pl.pallas_call(
        _fox_gate_kernel,
        grid=(bsz, nt),
        in_specs=[pl.BlockSpec((tg, d), lambda b, i: (b * nt + i, 0)),
                  pl.BlockSpec((d, LANES), lambda b, i: (0, 0)),
                  pl.BlockSpec((1, LANES), lambda b, i: (0, 0)),
                  pl.BlockSpec((tg, tg), lambda b, i: (0, 0)),
                  pl.BlockSpec((3, LANES, d), lambda b, i: (0, 0, 0))],
        out_specs=pl.BlockSpec((tg, d), lambda b, i: (b * nt + i, 0)),
        out_shape=jax.ShapeDtypeStruct((n, d), BF16),
        scratch_shapes=[pltpu.VMEM((1, LANES), F32)],
        compiler_params=_params(("arbitrary", "arbitrary"), 32),
        name="fox_gate",
    )(x, wf, bf, tri, jnp.asarray(sel, BF16))


def _fox_attn_kernel(qt_ref, k_ref, cb_ref, vt_ref, ot_ref, s_ref, *, tq):
    qi = pl.program_id(2)
    n_heads = 2 * FOX_PAIRS_PER_STEP
    feat = lax.broadcasted_iota(jnp.int32, (LANES, tq), 0)
    rhs = []
    for h in range(n_heads):
        pair, hh = divmod(h, 2)
        qt = qt_ref[pair * LANES:(pair + 1) * LANES, :].astype(F32)
        own = (feat >= hh * FOX_HEAD_DIM) & (feat < (hh + 1) * FOX_HEAD_DIM)
        bias_rows = (feat >= 3 * hh) & (feat < 3 * hh + 3)
        rhs.append(jnp.concatenate([jnp.where(own, qt, 0.0).astype(BF16),
                                    jnp.where(bias_rows, 1.0, 0.0).astype(BF16)], axis=0))
    key_i = lax.broadcasted_iota(jnp.int32, (tq, tq), 0)
    qry_i = lax.broadcasted_iota(jnp.int32, (tq, tq), 1)
    ones_rows = jnp.ones((16, tq), BF16)

    def scores(j, slot):
        start = pl.multiple_of(j * tq, tq)
        for pair in range(FOX_PAIRS_PER_STEP):
            lanes = slice(pair * LANES, (pair + 1) * LANES)
            kext = jnp.concatenate([k_ref[pl.ds(start, tq), lanes], cb_ref[pl.ds(start, tq), lanes]], axis=1)
            for h in (2 * pair, 2 * pair + 1):
                s_ref[slot, h] = jnp.dot(kext, rhs[h], preferred_element_type=F32)

    def block(j, carry, slot, masked):
        start = pl.multiple_of(j * tq, tq)
        if not masked:
            scores(j + 1, 1 - slot)
        out = []
        for h in range(n_heads):
            m, l, acc = carry[h]
            st = s_ref[slot, h]
            if masked:
                st = jnp.where(key_i <= qry_i, st, -jnp.inf)
            m_new = jnp.maximum(m, jnp.max(st, axis=0, keepdims=True))
            p = jnp.exp2(st - m_new)
            a = jnp.exp2(m - m_new)
            vt = vt_ref[h * FOX_HEAD_DIM:(h + 1) * FOX_HEAD_DIM, pl.ds(start, tq)]
            pv = jnp.dot(jnp.concatenate([vt, ones_rows], axis=0), p.astype(BF16), preferred_element_type=F32)
            out.append((m_new, a * l + pv[FOX_HEAD_DIM:FOX_HEAD_DIM + 1], acc * a + pv[:FOX_HEAD_DIM]))
        return tuple(out)

    init = (jnp.full((1, tq), -jnp.inf, F32), jnp.zeros((1, tq), F32), jnp.zeros((FOX_HEAD_DIM, tq), F32))
    scores(0, 0)
    carry = lax.fori_loop(
        0, qi // 2, lambda i, c: block(2 * i + 1, block(2 * i, c, 0, False), 1, False), (init,) * n_heads)
    carry = lax.cond(
        qi % 2 == 0,
        lambda c: block(qi, c, 0, True),
        lambda c: block(qi, block(qi - 1, c, 0, False), 1, True),
        carry)
    for h in range(n_heads):
        _, l, acc = carry[h]
        ot_ref[h * FOX_HEAD_DIM:(h + 1) * FOX_HEAD_DIM, :] = (acc / l).astype(ot_ref.dtype)


def _fox_attn(qt, k, cb, vt, bsz, t_len, tq=256):
    d, n = qt.shape
    w = FOX_PAIRS_PER_STEP * LANES
    nq = t_len // tq
    return pl.pallas_call(
        functools.partial(_fox_attn_kernel, tq=tq),
        grid=(bsz, d // w, nq),
        in_specs=[pl.BlockSpec((w, tq), lambda b, p, i: (p, b * nq + i)),
                  pl.BlockSpec((t_len, w), lambda b, p, i: (b, p)),
                  pl.BlockSpec((t_len, w), lambda b, p, i: (b, p)),
                  pl.BlockSpec((w, t_len), lambda b, p, i: (p, b))],
        out_specs=pl.BlockSpec((w, tq), lambda b, p, i: (p, b * nq + i)),
        out_shape=jax.ShapeDtypeStruct((d, n), BF16),
        scratch_shapes=[pltpu.VMEM((2, 2 * FOX_PAIRS_PER_STEP, tq, tq), F32)],
        compiler_params=_params(("arbitrary", "arbitrary", "arbitrary"), 32),
        name="fox_attn",
    )(qt, k, cb, vt)


def _fox_layer(x, w_in, b_f, w_out, g, b, router_w, alpha, bsz, t_len):
    d = x.shape[1]
    wqt = (w_in[:, :d] * (FOX_HEAD_DIM ** -0.5 * LOG2E)).T.astype(BF16)
    wk = w_in[:, d:2 * d].astype(BF16)
    wvt = w_in[:, 2 * d:3 * d].T.astype(BF16)
    wf = jnp.zeros((d, LANES), F32).at[:, :FOX_HEADS].set(w_in[:, 3 * d:]).astype(BF16)
    bf = jnp.zeros((1, LANES), F32).at[0, :FOX_HEADS].set(b_f)
    qt, k, vt = _fox_proj(x, wqt, wk, wvt)
    cb = _fox_gate(x, wf, bf, bsz, t_len)
    ot = _fox_attn(qt, k, cb, vt, bsz, t_len)
    return _outproj_ln(ot, w_out.astype(BF16), x, g, b, router_w, alpha, "fox_out_ln", transposed=True)


def _hgrn_constants():
    c = HGRN_CHUNK
    r = np.arange(c)[:, None]
    j = np.arange(c)[None, :]
    blocks = [j <= r]
    masks = []
    levels = []
    half = c // 2
    while half >= 1:
        ref = (r // (2 * half)) * (2 * half) + half - 1
        upper = (r % (2 * half)) >= half
        if half < HGRN_SMALL_LEVEL:
            blocks.append(np.where(upper, (j > ref) & (j <= r), (j > r) & (j <= ref)))
        masks.append(((r // (2 * half)) == (j // (2 * half))) & upper & ((j % (2 * half)) < half))
        levels.append(half)
        half //= 2
    masks.append(r == j)
    wall = np.concatenate(blocks, axis=0).astype(np.float32)
    wall2 = np.concatenate([wall, wall], axis=1)
    return wall2, np.stack(masks).astype(np.float32), tuple(levels)


def _hgrn_kernel(q_ref, fl_ref, i_ref, g_ref, lb_ref, ng_ref, wall_ref, mask_ref, o_ref, s_ref, *, levels):
    c = HGRN_CHUNK

    @pl.when(pl.program_id(1) == 0)
    def _():
        s_ref[...] = jnp.zeros_like(s_ref)

    wall = wall_ref[...]
    rowi = lax.broadcasted_iota(jnp.int32, (c, HGRN_DIM), 0)
    e_pair = None
    for h in range(HGRN_HEADS):
        sl = slice(h * HGRN_DIM, (h + 1) * HGRN_DIM)
        if h % 2 == 0:
            sl2 = slice(h * HGRN_DIM, (h + 2) * HGRN_DIM)
            lb2 = lb_ref[:, sl2]
            f2 = lb2 + (1.0 - lb2) * _sigmoid(fl_ref[:, sl2])
            logf2 = jnp.log(f2)
            hi = logf2.astype(BF16)
            mid = (logf2 - hi.astype(F32)).astype(BF16)
            e_pair = jnp.dot(wall, jnp.concatenate([hi, mid], axis=0), preferred_element_type=F32)
        lane0 = (h % 2) * HGRN_DIM
        q = q_ref[:, sl].astype(F32)
        i_b = i_ref[:, sl]
        k = 1.0 - f2[:, lane0:lane0 + HGRN_DIM]
        e_all = e_pair[:, lane0:lane0 + HGRN_DIM]
        b = e_all[0:c]
        x_pre = jnp.exp(b)
        x_suf = jnp.exp(b[c - 1:c, :] - b)
        st = s_ref[h]
        o = lax.dot_general((q * x_pre).astype(BF16), st.astype(BF16), _NT, preferred_element_type=F32)
        upd = lax.dot_general(i_b, (k * x_suf).astype(BF16), _TN, preferred_element_type=F32)
        s_ref[h] = st * x_pre[c - 1:c, :] + upd
        a = jnp.zeros((c, c), F32)
        n_big = sum(half >= HGRN_SMALL_LEVEL for half in levels)
        for l, half in enumerate(levels):
            upper = (rowi & half) != 0
            if half >= HGRN_SMALL_LEVEL:
                parts = []
                for r0 in range(0, c, 2 * half):
                    ref = b[r0 + half - 1:r0 + half, :]
                    parts += [ref - b[r0:r0 + half], b[r0 + half:r0 + 2 * half] - ref]
                e_lvl = jnp.concatenate(parts, axis=0)
            else:
                e_lvl = e_all[(1 + l - n_big) * c:(2 + l - n_big) * c]
            z = (jnp.where(upper, q, k) * jnp.exp(e_lvl)).astype(BF16)
            a = a + lax.dot_general(z, z, _NT, preferred_element_type=F32) * mask_ref[l]
        a = a + lax.dot_general(q.astype(BF16), k.astype(BF16), _NT,
                                preferred_element_type=F32) * mask_ref[len(levels)]
        o = o + jnp.dot(a.astype(BF16), i_b, preferred_element_type=F32)
        o = o * lax.rsqrt(jnp.mean(o * o, axis=-1, keepdims=True) + RMS_EPS) * ng_ref[...]
        gate = g_ref[:, sl].astype(F32)
        o_ref[:, sl] = (o * (gate * _sigmoid(gate))).astype(o_ref.dtype)


def _hgrn_core(q, fl, i, g, lb, ng, bsz, t_len):
    n, d = q.shape
    c = HGRN_CHUNK
    nc = t_len // c
    wall2, masks, levels = _hgrn_constants()
    row = lambda b, j: (b * nc + j, 0)
    fixed2 = lambda b, j: (0, 0)
    return pl.pallas_call(
        functools.partial(_hgrn_kernel, levels=levels),
        grid=(bsz, nc),
        in_specs=[pl.BlockSpec((c, d), row), pl.BlockSpec((c, d), row), pl.BlockSpec((c, d), row),
                  pl.BlockSpec((c, d), row), pl.BlockSpec((1, d), fixed2), pl.BlockSpec((1, HGRN_DIM), fixed2),
                  pl.BlockSpec(wall2.shape, fixed2), pl.BlockSpec(masks.shape, lambda b, j: (0, 0, 0))],
        out_specs=pl.BlockSpec((c, d), row),
        out_shape=jax.ShapeDtypeStruct((n, d), BF16),
        scratch_shapes=[pltpu.VMEM((HGRN_HEADS, HGRN_DIM, HGRN_DIM), F32)],
        compiler_params=_params(("arbitrary", "arbitrary"), 32),
        name="hgrn_core",
    )(q, fl, i, g, lb, ng, jnp.asarray(wall2, BF16), jnp.asarray(masks, F32))


def _hgrn_layer(x, w_in, lower_bound, norm_g, w_out, g, b, router_w, alpha, bsz, t_len):
    d = x.shape[1]
    wb = w_in.astype(BF16)
    q, fl, i, gate = _proj(x, [wb[:, :d], wb[:, d:2 * d], wb[:, 2 * d:3 * d], wb[:, 3 * d:]],
                           [BF16, F32, BF16, BF16], "hgrn_proj")
    o = _hgrn_core(q, fl, i, gate, lower_bound.reshape(1, d), norm_g.reshape(1, HGRN_DIM), bsz, t_len)
    return _outproj_ln(o, w_out.astype(BF16), x, g, b, router_w, alpha, "hgrn_out_ln")


def _conv_kernel(x_ref, win_ref, cw_ref, wout_ref, g_ref, b_ref, *refs, alpha, tm):
    y_ref, zbuf = refs[3], refs[8]
    d = x_ref.shape[1]

    @pl.when(pl.program_id(1) == 0)
    def _():
        zbuf[0:8, :] = jnp.zeros((8, d), F32)

    x = x_ref[...]
    p = jnp.dot(x.astype(BF16), win_ref[...], preferred_element_type=F32)
    z = p[:, d:2 * d] * p[:, 2 * d:]
    zbuf[8:8 + tm, :] = z
    y = cw_ref[2:3, :] * z + cw_ref[1:2, :] * zbuf[7:7 + tm, :] + cw_ref[0:1, :] * zbuf[6:6 + tm, :]
    zbuf[0:8, :] = z[tm - 8:, :]
    mixed = jnp.dot((p[:, :d] * y).astype(BF16), wout_ref[...], preferred_element_type=F32)
    out = _layer_norm(alpha * x + mixed, g_ref[...], b_ref[...])
    y_ref[...] = out
    _route_rows(out, (pl.program_id(0) == 0) & (pl.program_id(1) == 0), *refs[:3], *refs[4:8])


def _conv_layer(x, w_in, conv_w, w_out, g, b, router_w, alpha, bsz, t_len, tm=512):
    n, d = x.shape
    nt = t_len // tm
    row = lambda bb, i: (bb * nt + i, 0)
    fixed = lambda bb, i: (0, 0)
    r_in, r_out, r_shape = _route_specs(n, d, tm, row, fixed)
    y, *routed = pl.pallas_call(
        functools.partial(_conv_kernel, alpha=alpha, tm=tm),
        grid=(bsz, nt),
        in_specs=[pl.BlockSpec((tm, d), row), pl.BlockSpec((d, 3 * d), fixed), pl.BlockSpec(conv_w.shape, fixed),
                  pl.BlockSpec((d, d), fixed), pl.BlockSpec((1, d), fixed), pl.BlockSpec((1, d), fixed)] + r_in,
        out_specs=[pl.BlockSpec((tm, d), row)] + r_out,
        out_shape=[jax.ShapeDtypeStruct((n, d), F32)] + r_shape,
        scratch_shapes=[pltpu.VMEM((tm + 8, d), F32)],
        compiler_params=_params(("arbitrary", "arbitrary"), 56),
        name="conv_layer",
    )(x, w_in.astype(BF16), conv_w, w_out.astype(BF16), g, b, *router_w)
    return y, routed


def _store_token_tiles(xt_ref, v, row0=0):
    rows = v.shape[0]
    for c in range(TOKEN_TILE_ROWS):
        xt_ref[pl.ds(row0 + c, rows, stride=TOKEN_TILE_ROWS), :] = v[:, c * LANES:(c + 1) * LANES]


def _load_token_tiles(xt_ref, rows, row0=0):
    return jnp.concatenate([xt_ref[pl.ds(row0 + c, rows, stride=TOKEN_TILE_ROWS), :]
                            for c in range(TOKEN_TILE_ROWS)], axis=1)


def _route_rows(y, first_step, wh_ref, wl_ref, b_ref, xt_ref, meta_ref, oh_ref, cnt_ref):
    @pl.when(first_step)
    def _():
        cnt_ref[...] = jnp.zeros_like(cnt_ref)

    for r0 in range(0, y.shape[0], MOE_TOK_TILE):
        x = y[r0:r0 + MOE_TOK_TILE, :]
        meta, onehot = _route(x, wh_ref, wl_ref, b_ref)
        meta_ref[r0:r0 + MOE_TOK_TILE, :] = meta
        oh_ref[r0:r0 + MOE_TOK_TILE, :] = onehot.astype(BF16)
        _store_token_tiles(xt_ref, x, r0 * TOKEN_TILE_ROWS)
        cnt_ref[...] += jnp.sum(onehot, axis=0, keepdims=True)


def _route(x, wh_ref, wl_ref, b_ref):
    xh = x.astype(BF16)
    xl = (x - xh.astype(F32)).astype(BF16)
    wh = wh_ref[...]
    logits = (jnp.dot(xh, wh, preferred_element_type=F32) + jnp.dot(xh, wl_ref[...], preferred_element_type=F32)
              + jnp.dot(xl, wh, preferred_element_type=F32)) + b_ref[...]
    lane = lax.broadcasted_iota(jnp.int32, logits.shape, 1)
    lane_f = lane.astype(F32)
    neg = -jnp.inf
    far = float(LANES)
    is_g = (lane >= ROUTER_GROUP_LANE0) & (lane < ROUTER_GROUP_LANE0 + N_GROUPS)
    gl = jnp.where(is_g, logits, neg)
    gmax = jnp.max(gl, axis=-1, keepdims=True)
    g_sel = jnp.min(jnp.where(gl == gmax, lane_f, far), axis=-1, keepdims=True) - float(ROUTER_GROUP_LANE0)
    grp_w = 1.0 / jnp.sum(jnp.where(is_g, jnp.exp(logits - gmax), 0.0), axis=-1, keepdims=True)
    in_g = (lane < N_EXPERTS) & ((lane // EXPERTS_PER_GROUP).astype(F32) == g_sel)
    el = jnp.where(in_g, logits, neg)
    t1 = jnp.max(el, axis=-1, keepdims=True)
    i1 = jnp.min(jnp.where(el == t1, lane_f, far), axis=-1, keepdims=True)
    el2 = jnp.where(lane_f == i1, neg, el)
    t2 = jnp.max(el2, axis=-1, keepdims=True)
    i2 = jnp.min(jnp.where(el2 == t2, lane_f, far), axis=-1, keepdims=True)
    e2 = jnp.exp(t2 - t1)
    w1 = grp_w / (1.0 + e2)
    w2 = w1 * e2
    first_lo = i1 < i2
    lo = jnp.where(first_lo, i1, i2) - g_sel * EXPERTS_PER_GROUP
    hi = jnp.where(first_lo, i2, i1) - g_sel * EXPERTS_PER_GROUP
    pair = lo * (EXPERTS_PER_GROUP - 1.0) - lo * (lo - 1.0) * 0.5 + (hi - lo - 1.0)
    onehot = jnp.where(lane_f == g_sel * MOE_PAIRS + pair, 1.0, 0.0)
    meta = (jnp.where(lane == 0, jnp.where(first_lo, w1, w2), 0.0)
            + jnp.where(lane == 1, jnp.where(first_lo, w2, w1), 0.0))
    return meta, onehot


def _router_weights(w_group, b_group, w_expert, b_expert):
    d = w_group.shape[0]
    w = jnp.zeros((d, LANES), F32).at[:, :N_EXPERTS].set(w_expert)
    w = w.at[:, ROUTER_GROUP_LANE0:ROUTER_GROUP_LANE0 + N_GROUPS].set(w_group)
    bias = jnp.zeros((1, LANES), F32).at[0, :N_EXPERTS].set(b_expert)
    bias = bias.at[0, ROUTER_GROUP_LANE0:ROUTER_GROUP_LANE0 + N_GROUPS].set(b_group)
    wh = w.astype(BF16)
    return wh, (w - wh.astype(F32)).astype(BF16), bias


def _route_specs(n, d, tm, row, fixed):
    assert tm % MOE_TOK_TILE == 0 and d == TOKEN_TILE_ROWS * LANES
    in_specs = [pl.BlockSpec((d, LANES), fixed), pl.BlockSpec((d, LANES), fixed), pl.BlockSpec((1, LANES), fixed)]
    out_specs = [pl.BlockSpec((tm * TOKEN_TILE_ROWS, LANES), row), pl.BlockSpec((tm, LANES), row),
                 pl.BlockSpec((tm, LANES), row), pl.BlockSpec((8, LANES), fixed)]
    out_shape = [jax.ShapeDtypeStruct((n * TOKEN_TILE_ROWS, LANES), F32), jax.ShapeDtypeStruct((n, LANES), F32),
                 jax.ShapeDtypeStruct((n, LANES), BF16), jax.ShapeDtypeStruct((8, LANES), F32)]
    return in_specs, out_specs, out_shape


def _moe_pos_kernel(oh_ref, base_ref, lt_ref, pos_ref, carry_ref):
    @pl.when(pl.program_id(0) == 0)
    def _():
        carry_ref[...] = jnp.zeros_like(carry_ref)

    oh = oh_ref[...]
    rank = jnp.dot(lt_ref[...], oh, preferred_element_type=F32) + carry_ref[...] + base_ref[...]
    ohf = oh.astype(F32)
    val = ohf * rank
    carry_ref[...] += jnp.sum(ohf, axis=0, keepdims=True)
    hi = jnp.floor(val * (1.0 / 256.0))
    lo = val - 256.0 * hi
    ones = jnp.ones((8, LANES), BF16)
    pos = (256.0 * lax.dot_general(ones, hi.astype(BF16), _NT, preferred_element_type=F32)
           + lax.dot_general(ones, lo.astype(BF16), _NT, preferred_element_type=F32))
    pos_ref[0] = pos[0:1].astype(jnp.int32)


def _moe_pos(onehot, base, tm=MOE_TOK_TILE):
    n = onehot.shape[0]
    lt = jnp.asarray(np.tril(np.ones((tm, tm), np.float32), -1), BF16)
    return pl.pallas_call(
        _moe_pos_kernel,
        grid=(n // tm,),
        in_specs=[pl.BlockSpec((tm, LANES), lambda i: (i, 0)), pl.BlockSpec((1, LANES), lambda i: (0, 0)),
                  pl.BlockSpec((tm, tm), lambda i: (0, 0))],
        out_specs=pl.BlockSpec((1, 1, tm), lambda i: (i, 0, 0)),
        out_shape=jax.ShapeDtypeStruct((n // tm, 1, tm), jnp.int32),
        scratch_shapes=[pltpu.VMEM((1, LANES), F32)],
        compiler_params=_params(("arbitrary",), 32),
        name="moe_pos",
    )(onehot, base, lt)


def _tile_copy(src, src_tok, dst, dst_tok, sem):
    rows = TOKEN_TILE_ROWS
    return pltpu.make_async_copy(src.at[pl.ds(pl.multiple_of(src_tok * rows, rows), rows)],
                                 dst.at[pl.ds(pl.multiple_of(dst_tok * rows, rows), rows)], sem)


def _meta_copy(src, src_tok, dst, dst_tok, sem):
    return pltpu.make_async_copy(src.at[pl.ds(src_tok, 1)], dst.at[pl.ds(dst_tok, 1)], sem)


def _moe_scatter_kernel(pos_ref, xt_ref, meta_ref, xs_in_ref, xm_in_ref, xs_ref, xm_ref, sem, *, tm):
    del xs_in_ref, xm_in_ref

    def start(r2, c):
        for par in range(2):
            r = 2 * r2 + par
            p = pos_ref[0, 0, r]
            _tile_copy(xt_ref, r, xs_ref, p, sem.at[0]).start(priority=par)
            _meta_copy(meta_ref, r, xm_ref, p, sem.at[1]).start(priority=1 - par)
        return c

    lax.fori_loop(0, tm // 2, start, 0, unroll=4)
    pltpu.make_async_copy(xt_ref, xs_ref.at[pl.ds(0, tm * TOKEN_TILE_ROWS)], sem.at[0]).wait()
    pltpu.make_async_copy(meta_ref, xm_ref.at[pl.ds(0, tm)], sem.at[1]).wait()


def _moe_scatter(pos, xt, meta, xs_buf, xm_buf, tm=MOE_TOK_TILE):
    n = meta.shape[0]
    return pl.pallas_call(
        functools.partial(_moe_scatter_kernel, tm=tm),
        grid=(n // tm,),
        in_specs=[pl.BlockSpec((1, 1, tm), lambda i: (i, 0, 0), memory_space=pltpu.SMEM),
                  pl.BlockSpec((tm * TOKEN_TILE_ROWS, LANES), lambda i: (i, 0)),
                  pl.BlockSpec((tm, LANES), lambda i: (i, 0)),
                  pl.BlockSpec(memory_space=pl.ANY), pl.BlockSpec(memory_space=pl.ANY)],
        out_specs=[pl.BlockSpec(memory_space=pl.ANY), pl.BlockSpec(memory_space=pl.ANY)],
        out_shape=[jax.ShapeDtypeStruct(xs_buf.shape, xs_buf.dtype), jax.ShapeDtypeStruct(xm_buf.shape, xm_buf.dtype)],
        scratch_shapes=[pltpu.SemaphoreType.DMA((2,))],
        input_output_aliases={3: 0, 4: 1},
        compiler_params=_params(("arbitrary",), 32),
        name="moe_scatter",
    )(pos, xt, meta, xs_buf, xm_buf)


def _moe_experts_kernel(grp_ref, e1_ref, e2_ref, nused_ref, xs_ref, xm_ref, wg_ref, wu_ref, wd_ref, ys_in_ref, ys_ref):
    del grp_ref, ys_in_ref
    i = pl.program_id(0)
    tr = xm_ref.shape[0] // MOE_TILES_PER_STEP

    @pl.when(i * MOE_TILES_PER_STEP < nused_ref[0])
    def _():
        for t in range(MOE_TILES_PER_STEP):
            x = _load_token_tiles(xs_ref, tr, t * tr * TOKEN_TILE_ROWS).astype(BF16)
            meta = xm_ref[t * tr:(t + 1) * tr, :]
            y = None
            for k, e_ref in enumerate((e1_ref, e2_ref)):
                e = e_ref[i * MOE_TILES_PER_STEP + t]
                hg = jnp.dot(x, wg_ref[0, e], preferred_element_type=F32)
                hu = jnp.dot(x, wu_ref[0, e], preferred_element_type=F32)
                h = hg * _sigmoid(hg) * hu * meta[:, k:k + 1]
                yk = jnp.dot(h.astype(BF16), wd_ref[0, e], preferred_element_type=F32)
                y = yk if y is None else y + yk
            _store_token_tiles(ys_ref, y, t * tr * TOKEN_TILE_ROWS)


def _moe_experts(xs_buf, xm_buf, ys_buf, tile_grp, tile_e1, tile_e2, n_used, w_gate, w_up, w_down, layer,
                 tr=MOE_ROW_TILE):
    r_max = xm_buf.shape[0]
    d, f = w_gate.shape[-2:]
    assert d == TOKEN_TILE_ROWS * LANES
    tps = MOE_TILES_PER_STEP
    step_rows = tps * tr
    tile_rows = step_rows * TOKEN_TILE_ROWS
    epg = EXPERTS_PER_GROUP
    used = lambda i, grp, e1, e2, nu: (jnp.minimum(i, nu[0] // tps - 1), 0)
    wmap = lambda i, grp, e1, e2, nu: (layer * N_GROUPS + grp[i * tps], 0, 0, 0)
    grid_spec = pltpu.PrefetchScalarGridSpec(
        num_scalar_prefetch=4,
        grid=(r_max // step_rows,),
        in_specs=[pl.BlockSpec((tile_rows, LANES), used), pl.BlockSpec((step_rows, LANES), used),
                  pl.BlockSpec((1, epg, d, f), wmap), pl.BlockSpec((1, epg, d, f), wmap),
                  pl.BlockSpec((1, epg, f, d), wmap), pl.BlockSpec(memory_space=pl.ANY)],
        out_specs=pl.BlockSpec((tile_rows, LANES), used),
    )
    return pl.pallas_call(
        _moe_experts_kernel,
        grid_spec=grid_spec,
        out_shape=jax.ShapeDtypeStruct(ys_buf.shape, ys_buf.dtype),
        input_output_aliases={9: 0},
        compiler_params=_params(("arbitrary",), 48),
        name="moe_experts",
    )(tile_grp, tile_e1, tile_e2, n_used, xs_buf, xm_buf, w_gate, w_up, w_down, ys_buf)


def _moe_combine_ln_kernel(pos_ref, nxt_ref, x_ref, ys_ref, g_ref, b_ref, y_ref, buf_ref, sem, *, alpha, tm):
    i = pl.program_id(0)
    slot = i % 2

    def gather(p_ref, s):
        def start(r2, c):
            for par in range(2):
                r = 2 * r2 + par
                _tile_copy(ys_ref, p_ref[0, 0, r], buf_ref, s * tm + r, sem.at[s]).start(priority=par)
            return c
        lax.fori_loop(0, tm // 2, start, 0, unroll=4)

    @pl.when(i == 0)
    def _():
        gather(pos_ref, 0)

    @pl.when(i + 1 < pl.num_programs(0))
    def _():
        gather(nxt_ref, 1 - slot)

    base = pl.multiple_of(slot * (tm * TOKEN_TILE_ROWS), tm * TOKEN_TILE_ROWS)
    slot_rows = pl.ds(base, tm * TOKEN_TILE_ROWS)
    pltpu.make_async_copy(ys_ref.at[pl.ds(0, tm * TOKEN_TILE_ROWS)], buf_ref.at[slot_rows], sem.at[slot]).wait()
    ffn = _load_token_tiles(buf_ref, tm, base)
    y_ref[...] = _layer_norm(alpha * x_ref[...] + ffn, g_ref[...], b_ref[...])


def _moe_combine_ln(pos, x, ys_buf, g, b, alpha, tm=MOE_TOK_TILE):
    n, d = x.shape
    nt = n // tm
    row = lambda i: (i, 0)
    fixed = lambda i: (0, 0)
    return pl.pallas_call(
        functools.partial(_moe_combine_ln_kernel, alpha=alpha, tm=tm),
        grid=(nt,),
        in_specs=[pl.BlockSpec((1, 1, tm), lambda i: (i, 0, 0), memory_space=pltpu.SMEM),
                  pl.BlockSpec((1, 1, tm), lambda i: (jnp.minimum(i + 1, nt - 1), 0, 0), memory_space=pltpu.SMEM),
                  pl.BlockSpec((tm, d), row), pl.BlockSpec(memory_space=pl.ANY),
                  pl.BlockSpec((1, d), fixed), pl.BlockSpec((1, d), fixed)],
        out_specs=pl.BlockSpec((tm, d), row),
        out_shape=jax.ShapeDtypeStruct((n, d), F32),
        scratch_shapes=[pltpu.VMEM((2 * tm * TOKEN_TILE_ROWS, LANES), F32), pltpu.SemaphoreType.DMA((2,))],
        compiler_params=_params(("arbitrary",), 32),
        name="moe_combine_ln",
    )(pos, pos, x, ys_buf, g, b)


def _bucket_experts():
    lo, hi = [], []
    for a in range(EXPERTS_PER_GROUP):
        for c in range(a + 1, EXPERTS_PER_GROUP):
            lo.append(a)
            hi.append(c)
    grp = np.repeat(np.arange(N_GROUPS), MOE_PAIRS).astype(np.int32)
    return grp, np.tile(lo, N_GROUPS).astype(np.int32), np.tile(hi, N_GROUPS).astype(np.int32)


def _moe_layer(x, routed, bufs, w_gate, w_up, w_down, layer, g, b, alpha):
    tr = MOE_ROW_TILE
    xs_buf, xm_buf, ys_buf = bufs
    xt, meta, onehot, cnt = routed
    counts = cnt[0, :MOE_BUCKETS].astype(jnp.int32)
    padded = ((counts + (tr - 1)) // tr * tr).reshape(N_GROUPS, MOE_PAIRS)
    step_rows = MOE_TILES_PER_STEP * tr
    slack = -jnp.sum(padded, axis=1) % step_rows
    padded = padded.at[:, MOE_PAIRS - 1].add(slack).reshape(MOE_BUCKETS)
    ends = jnp.cumsum(padded)
    base = jnp.zeros((1, LANES), F32).at[0, :MOE_BUCKETS].set((ends - padded).astype(F32))
    n_used = ends[-1] // tr
    n_tiles = xm_buf.shape[0] // tr
    tile = jnp.minimum(jnp.arange(n_tiles, dtype=jnp.int32), n_used - 1)
    tile_bkt = jnp.sum((ends[None, :] <= (tile * tr)[:, None]).astype(jnp.int32), axis=1)
    tile_bkt = jnp.minimum(tile_bkt, MOE_BUCKETS - 1)
    b_grp, b_lo, b_hi = (jnp.asarray(t)[tile_bkt] for t in _bucket_experts())
    pos = _moe_pos(onehot, base)
    xs_buf, xm_buf = _moe_scatter(pos, xt, meta, xs_buf, xm_buf)
    ys_buf = _moe_experts(xs_buf, xm_buf, ys_buf, b_grp, b_lo, b_hi, n_used.reshape(1).astype(jnp.int32),
                          w_gate, w_up, w_down, layer)
    return _moe_combine_ln(pos, x, ys_buf, g, b, alpha), (xs_buf, xm_buf, ys_buf)


def kernel(x, ln_mix_g, ln_mix_b, ln_ffn_g, ln_ffn_b, fox_w_in, fox_b_f, fox_w_out, hgrn_w_in, hgrn_lb_logits, hgrn_norm_g, hgrn_w_out, conv_w_in, conv_w, conv_w_out, moe_w_group, moe_b_group, moe_w_expert, moe_b_expert, moe_w_gate, moe_w_up, moe_w_down):
    bsz, t_len, d = x.shape
    depth = ln_mix_g.shape[0]
    alpha = float((2 * depth) ** 0.25)
    assert d == FOX_HEADS * FOX_HEAD_DIM == HGRN_HEADS * HGRN_DIM
    assert t_len % 512 == 0 and (bsz * t_len) % 1024 == 0

    lb_prob = jax.nn.softmax(hgrn_lb_logits.astype(F32), axis=0)
    lower_bounds = jnp.cumsum(lb_prob, axis=0) - lb_prob[0]

    h = x.reshape(bsz * t_len, d)
    r_max = bsz * t_len + MOE_BUCKETS * MOE_ROW_TILE + N_GROUPS * MOE_TILES_PER_STEP * MOE_ROW_TILE
    assert r_max % (MOE_TILES_PER_STEP * MOE_ROW_TILE) == 0
    by_group = lambda w: w.astype(BF16).reshape((depth * N_GROUPS, EXPERTS_PER_GROUP) + w.shape[2:])
    moe_w_gate, moe_w_up, moe_w_down = by_group(moe_w_gate), by_group(moe_w_up), by_group(moe_w_down)
    bufs = (jnp.zeros((r_max * TOKEN_TILE_ROWS, LANES), F32), jnp.zeros((r_max, LANES), F32),
            jnp.zeros((r_max * TOKEN_TILE_ROWS, LANES), F32))
    for layer in range(depth):
        kind, j = layer % 3, layer // 3
        g_mix, b_mix = ln_mix_g[layer].reshape(1, d), ln_mix_b[layer].reshape(1, d)
        router_w = _router_weights(moe_w_group[layer], moe_b_group[layer], moe_w_expert[layer], moe_b_expert[layer])
        if kind == 0:
            h, routed = _fox_layer(h, fox_w_in[j], fox_b_f[j], fox_w_out[j], g_mix, b_mix, router_w,
                                   alpha, bsz, t_len)
        elif kind == 1:
            h, routed = _hgrn_layer(h, hgrn_w_in[j], lower_bounds[layer], hgrn_norm_g[j], hgrn_w_out[j],
                                    g_mix, b_mix, router_w, alpha, bsz, t_len)
        else:
            h, routed = _conv_layer(h, conv_w_in[j], conv_w[j], conv_w_out[j], g_mix, b_mix, router_w,
                                    alpha, bsz, t_len)
        h, bufs = _moe_layer(h, routed, bufs, moe_w_gate, moe_w_up, moe_w_down, layer,
                             ln_ffn_g[layer].reshape(1, d), ln_ffn_b[layer].reshape(1, d), alpha)
    return h.reshape(bsz, t_len, d)
```

```python
import functools

import numpy as np
import jax
import jax.numpy as jnp
from jax import lax
from jax.experimental import pallas as pl
from jax.experimental.pallas import tpu as pltpu

F32 = jnp.float32
BF16 = jnp.bfloat16

FOX_HEADS = 16
FOX_HEAD_DIM = 64
FOX_PAIRS_PER_STEP = 4
HGRN_HEADS = 8
HGRN_DIM = 128
HGRN_CHUNK = 128
HGRN_SMALL_LEVEL = 8
N_GROUPS = 4
EXPERTS_PER_GROUP = 8
N_EXPERTS = N_GROUPS * EXPERTS_PER_GROUP
LN_EPS = 1e-5
RMS_EPS = 1e-6
LOG2E = 1.4426950408889634
LANES = 128
ROUTER_GROUP_LANE0 = N_EXPERTS
MOE_PAIRS = EXPERTS_PER_GROUP * (EXPERTS_PER_GROUP - 1) // 2
MOE_BUCKETS = N_GROUPS * MOE_PAIRS
MOE_ROW_TILE = 128
MOE_TILES_PER_STEP = 4
MOE_TOK_TILE = 256
TOKEN_TILE_ROWS = 8

_NT = (((1,), (1,)), ((), ()))
_TN = (((0,), (0,)), ((), ()))


def _params(semantics, vmem_mb):
    return pltpu.CompilerParams(dimension_semantics=semantics,
                                vmem_limit_bytes=vmem_mb * 1024 * 1024)


def _layer_norm(v, g, b):
    mu = jnp.mean(v, axis=-1, keepdims=True)
    d = v - mu
    var = jnp.mean(d * d, axis=-1, keepdims=True)
    return d * lax.rsqrt(var + LN_EPS) * g + b


def _sigmoid(v):
    return 1.0 / (1.0 + jnp.exp(-v))


def _proj_kernel(x_ref, *refs, n_out):
    xb = x_ref[...].astype(BF16)
    for w_ref, o_ref in zip(refs[:n_out], refs[n_out:]):
        o_ref[...] = jnp.dot(xb, w_ref[...], preferred_element_type=F32).astype(o_ref.dtype)


def _proj(x, ws, dtypes, name, tm=512):
    n, k = x.shape
    return pl.pallas_call(
        functools.partial(_proj_kernel, n_out=len(ws)),
        grid=(n // tm,),
        in_specs=[pl.BlockSpec((tm, k), lambda i: (i, 0))]
        + [pl.BlockSpec(w.shape, lambda i: (0, 0)) for w in ws],
        out_specs=[pl.BlockSpec((tm, w.shape[1]), lambda i: (i, 0)) for w in ws],
        out_shape=[jax.ShapeDtypeStruct((n, w.shape[1]), dt) for w, dt in zip(ws, dtypes)],
        compiler_params=_params(("arbitrary",), 48),
        name=name,
    )(x, *ws)


def _outproj_ln_kernel(o_ref, w_ref, x_ref, g_ref, b_ref, *refs, alpha, transposed):
    y_ref = refs[3]
    dims = _TN if transposed else (((1,), (0,)), ((), ()))
    mixed = lax.dot_general(o_ref[...], w_ref[...], dims, preferred_element_type=F32)
    y = _layer_norm(alpha * x_ref[...] + mixed, g_ref[...], b_ref[...])
    y_ref[...] = y
    _route_rows(y, pl.program_id(0) == 0, *refs[:3], *refs[4:])


def _outproj_ln(o, w, x, g, b, router_w, alpha, name, tm=512, transposed=False):
    n, d = x.shape
    k = w.shape[0]
    row = lambda i: (i, 0)
    fixed = lambda i: (0, 0)
    o_spec = pl.BlockSpec((k, tm), lambda i: (0, i)) if transposed else pl.BlockSpec((tm, k), row)
    r_in, r_out, r_shape = _route_specs(n, d, tm, row, fixed)
    y, *routed = pl.pallas_call(
        functools.partial(_outproj_ln_kernel, alpha=alpha, transposed=transposed),
        grid=(n // tm,),
        in_specs=[o_spec, pl.BlockSpec((k, d), fixed),
                  pl.BlockSpec((tm, d), row), pl.BlockSpec((1, d), fixed), pl.BlockSpec((1, d), fixed)] + r_in,
        out_specs=[pl.BlockSpec((tm, d), row)] + r_out,
        out_shape=[jax.ShapeDtypeStruct((n, d), F32)] + r_shape,
        compiler_params=_params(("arbitrary",), 48),
        name=name,
    )(o, w, x, g, b, *router_w)
    return y, routed


def _split3(v):
    hi = v.astype(BF16)
    r1 = v - hi.astype(F32)
    mid = r1.astype(BF16)
    return hi, mid, (r1 - mid.astype(F32)).astype(BF16)


def _fox_proj_kernel(x_ref, wqt_ref, wk_ref, wvt_ref, qt_ref, k_ref, vt_ref):
    xb = x_ref[...].astype(BF16)
    qt_ref[...] = lax.dot_general(wqt_ref[...], xb, _NT, preferred_element_type=F32).astype(BF16)
    k_ref[...] = jnp.dot(xb, wk_ref[...], preferred_element_type=F32).astype(BF16)
    vt_ref[...] = lax.dot_general(wvt_ref[...], xb, _NT, preferred_element_type=F32).astype(BF16)


def _fox_proj(x, wqt, wk, wvt, tm=512):
    n, d = x.shape
    fixed = lambda i: (0, 0)
    return pl.pallas_call(
        _fox_proj_kernel,
        grid=(n // tm,),
        in_specs=[pl.BlockSpec((tm, d), lambda i: (i, 0)), pl.BlockSpec((d, d), fixed),
                  pl.BlockSpec((d, d), fixed), pl.BlockSpec((d, d), fixed)],
        out_specs=[pl.BlockSpec((d, tm), lambda i: (0, i)), pl.BlockSpec((tm, d), lambda i: (i, 0)),
                   pl.BlockSpec((d, tm), lambda i: (0, i))],
        out_shape=[jax.ShapeDtypeStruct((d, n), BF16), jax.ShapeDtypeStruct((n, d), BF16),
                   jax.ShapeDtypeStruct((d, n), BF16)],
        compiler_params=_params(("arbitrary",), 48),
        name="fox_proj",
    )(x, wqt, wk, wvt)


def _fox_gate_kernel(x_ref, wf_ref, bf_ref, tri_ref, sel_ref, cb_ref, carry_ref):
    @pl.when(pl.program_id(1) == 0)
    def _():
        carry_ref[...] = jnp.zeros_like(carry_ref)

    z = jnp.dot(x_ref[...].astype(BF16), wf_ref[...], preferred_element_type=F32) + bf_ref[...]
    logf = jnp.minimum(z, 0.0) - jnp.log(1.0 + jnp.exp(-jnp.abs(z)))
    c3 = jnp.dot(tri_ref[...], jnp.concatenate(_split3(logf), axis=1), preferred_element_type=F32)
    c = carry_ref[...] + c3[:, :LANES] + c3[:, LANES:2 * LANES] + c3[:, 2 * LANES:]
    carry_ref[...] = c[c.shape[0] - 1:, :]
    neg3 = jnp.concatenate(_split3(c * (-LOG2E)), axis=1)
    cb_ref[...] = jnp.dot(neg3, sel_ref[...], preferred_element_type=F32).astype(BF16)


def _fox_gate(x, wf, bf, bsz, t_len, tg=512):
    n, d = x.shape
    nt = t_len // tg
    tri = jnp.asarray(np.tril(np.ones((tg, tg), np.float32)), BF16)
    sel = np.zeros((3 * LANES, d), np.float32)
    for h in range(FOX_HEADS):
        for j in range(3):
            sel[j * LANES + h, (h // 2) * LANES + 3 * (h % 2) + j] = 1.0
    return pl.pallas_call(
        _fox_gate_kernel,
        grid=(bsz, nt),
        in_specs=[pl.BlockSpec((tg, d), lambda b, i: (b * nt + i, 0)),
                  pl.BlockSpec((d, LANES), lambda b, i: (0, 0)),
                  pl.BlockSpec((1, LANES), lambda b, i: (0, 0)),
                  pl.BlockSpec((tg, tg), lambda b, i: (0, 0)),
                  pl.BlockSpec((3 * LANES, d), lambda b, i: (0, 0))],
        out_specs=pl.BlockSpec((tg, d), lambda b, i: (b * nt + i, 0)),
        out_shape=jax.ShapeDtypeStruct((n, d), BF16),
        scratch_shapes=[pltpu.VMEM((1, LANES), F32)],
        compiler_params=_params(("arbitrary", "arbitrary"), 32),
        name="fox_gate",
    )(x, wf, bf, tri, jnp.asarray(sel, BF16))


def _fox_attn_kernel(qt_ref, k_ref, cb_ref, vt_ref, ot_ref, s_ref, *, tq):
    qi = pl.program_id(2)
    n_heads = 2 * FOX_PAIRS_PER_STEP
    feat = lax.broadcasted_iota(jnp.int32, (LANES, tq), 0)
    rhs = []
    for h in range(n_heads):
        pair, hh = divmod(h, 2)
        qt = qt_ref[pair * LANES:(pair + 1) * LANES, :].astype(F32)
        own = (feat >= hh * FOX_HEAD_DIM) & (feat < (hh + 1) * FOX_HEAD_DIM)
        bias_rows = (feat >= 3 * hh) & (feat < 3 * hh + 3)
        rhs.append(jnp.concatenate([jnp.where(own, qt, 0.0).astype(BF16),
                                    jnp.where(bias_rows, 1.0, 0.0).astype(BF16)], axis=0))
    key_i = lax.broadcasted_iota(jnp.int32, (tq, tq), 0)
    qry_i = lax.broadcasted_iota(jnp.int32, (tq, tq), 1)
    ones_rows = jnp.ones((16, tq), BF16)

    def scores(j, slot):
        start = pl.multiple_of(j * tq, tq)
        for pair in range(FOX_PAIRS_PER_STEP):
            lanes = slice(pair * LANES, (pair + 1) * LANES)
            kext = jnp.concatenate([k_ref[pl.ds(start, tq), lanes], cb_ref[pl.ds(start, tq), lanes]], axis=1)
            for h in (2 * pair, 2 * pair + 1):
                s_ref[slot, h] = jnp.dot(kext, rhs[h], preferred_element_type=F32)

    def block(j, carry, slot, masked):
        start = pl.multiple_of(j * tq, tq)
        if not masked:
            scores(j + 1, 1 - slot)
        out = []
        for h in range(n_heads):
            m, l, acc = carry[h]
            st = s_ref[slot, h]
            if masked:
                st = jnp.where(key_i <= qry_i, st, -jnp.inf)
            m_new = jnp.maximum(m, jnp.max(st, axis=0, keepdims=True))
            p = jnp.exp2(st - m_new)
            a = jnp.exp2(m - m_new)
            vt = vt_ref[h * FOX_HEAD_DIM:(h + 1) * FOX_HEAD_DIM, pl.ds(start, tq)]
            pv = jnp.dot(jnp.concatenate([vt, ones_rows], axis=0), p.astype(BF16), preferred_element_type=F32)
            out.append((m_new, a * l + pv[FOX_HEAD_DIM:FOX_HEAD_DIM + 1], acc * a + pv[:FOX_HEAD_DIM]))
        return tuple(out)

    init = (jnp.full((1, tq), -jnp.inf, F32), jnp.zeros((1, tq), F32), jnp.zeros((FOX_HEAD_DIM, tq), F32))
    scores(0, 0)
    carry = lax.fori_loop(
        0, qi // 2, lambda i, c: block(2 * i + 1, block(2 * i, c, 0, False), 1, False), (init,) * n_heads)
    carry = lax.cond(
        qi % 2 == 0,
        lambda c: block(qi, c, 0, True),
        lambda c: block(qi, block(qi - 1, c, 0, False), 1, True),
        carry)
    for h in range(n_heads):
        _, l, acc = carry[h]
        ot_ref[h * FOX_HEAD_DIM:(h + 1) * FOX_HEAD_DIM, :] = (acc / l).astype(ot_ref.dtype)


def _fox_attn(qt, k, cb, vt, bsz, t_len, tq=256):
    d, n = qt.shape
    w = FOX_PAIRS_PER_STEP * LANES
    nq = t_len // tq
    return pl.pallas_call(
        functools.partial(_fox_attn_kernel, tq=tq),
        grid=(bsz, d // w, nq),
        in_specs=[pl.BlockSpec((w, tq), lambda b, p, i: (p, b * nq + i)),
                  pl.BlockSpec((t_len, w), lambda b, p, i: (b, p)),
                  pl.BlockSpec((t_len, w), lambda b, p, i: (b, p)),
                  pl.BlockSpec((w, t_len), lambda b, p, i: (p, b))],
        out_specs=pl.BlockSpec((w, tq), lambda b, p, i: (p, b * nq + i)),
        out_shape=jax.ShapeDtypeStruct((d, n), BF16),
        scratch_shapes=[pltpu.VMEM((2, 2 * FOX_PAIRS_PER_STEP, tq, tq), F32)],
        compiler_params=_params(("arbitrary", "arbitrary", "arbitrary"), 32),
        name="fox_attn",
    )(qt, k, cb, vt)


def _fox_layer(x, w_in, b_f, w_out, g, b, router_w, alpha, bsz, t_len):
    d = x.shape[1]
    wqt = (w_in[:, :d] * (FOX_HEAD_DIM ** -0.5 * LOG2E)).T.astype(BF16)
    wk = w_in[:, d:2 * d].astype(BF16)
    wvt = w_in[:, 2 * d:3 * d].T.astype(BF16)
    wf = jnp.zeros((d, LANES), F32).at[:, :FOX_HEADS].set(w_in[:, 3 * d:]).astype(BF16)
    bf = jnp.zeros((1, LANES), F32).at[0, :FOX_HEADS].set(b_f)
    qt, k, vt = _fox_proj(x, wqt, wk, wvt)
    cb = _fox_gate(x, wf, bf, bsz, t_len)
    ot = _fox_attn(qt, k, cb, vt, bsz, t_len)
    return _outproj_ln(ot, w_out.astype(BF16), x, g, b, router_w, alpha, "fox_out_ln", transposed=True)


def _hgrn_constants():
    c = HGRN_CHUNK
    r = np.arange(c)[:, None]
    j = np.arange(c)[None, :]
    blocks = [j <= r]
    masks = []
    levels = []
    half = c // 2
    while half >= 1:
        ref = (r // (2 * half)) * (2 * half) + half - 1
        upper = (r % (2 * half)) >= half
        if half < HGRN_SMALL_LEVEL:
            blocks.append(np.where(upper, (j > ref) & (j <= r), (j > r) & (j <= ref)))
        masks.append(((r // (2 * half)) == (j // (2 * half))) & upper & ((j % (2 * half)) < half))
        levels.append(half)
        half //= 2
    masks.append(r == j)
    wall = np.concatenate(blocks, axis=0).astype(np.float32)
    wall2 = np.concatenate([wall, wall], axis=1)
    return wall2, np.stack(masks).astype(np.float32), tuple(levels)


def _hgrn_kernel(q_ref, fl_ref, i_ref, g_ref, lb_ref, ng_ref, wall_ref, mask_ref, o_ref, s_ref, *, levels):
    c = HGRN_CHUNK

    @pl.when(pl.program_id(1) == 0)
    def _():
        s_ref[...] = jnp.zeros_like(s_ref)

    wall = wall_ref[...]
    rowi = lax.broadcasted_iota(jnp.int32, (c, HGRN_DIM), 0)
    e_pair = None
    for h in range(HGRN_HEADS):
        sl = slice(h * HGRN_DIM, (h + 1) * HGRN_DIM)
        if h % 2 == 0:
            sl2 = slice(h * HGRN_DIM, (h + 2) * HGRN_DIM)
            lb2 = lb_ref[:, sl2]
            f2 = lb2 + (1.0 - lb2) * _sigmoid(fl_ref[:, sl2])
            logf2 = jnp.log(f2)
            hi = logf2.astype(BF16)
            mid = (logf2 - hi.astype(F32)).astype(BF16)
            e_pair = jnp.dot(wall, jnp.concatenate([hi, mid], axis=0), preferred_element_type=F32)
        lane0 = (h % 2) * HGRN_DIM
        q = q_ref[:, sl].astype(F32)
        i_b = i_ref[:, sl]
        k = 1.0 - f2[:, lane0:lane0 + HGRN_DIM]
        e_all = e_pair[:, lane0:lane0 + HGRN_DIM]
        b = e_all[0:c]
        x_pre = jnp.exp(b)
        x_suf = jnp.exp(b[c - 1:c, :] - b)
        st = s_ref[h]
        o = lax.dot_general((q * x_pre).astype(BF16), st.astype(BF16), _NT, preferred_element_type=F32)
        upd = lax.dot_general(i_b, (k * x_suf).astype(BF16), _TN, preferred_element_type=F32)
        s_ref[h] = st * x_pre[c - 1:c, :] + upd
        a = jnp.zeros((c, c), F32)
        n_big = sum(half >= HGRN_SMALL_LEVEL for half in levels)
        for l, half in enumerate(levels):
            upper = (rowi & half) != 0
            if half >= HGRN_SMALL_LEVEL:
                parts = []
                for r0 in range(0, c, 2 * half):
                    ref = b[r0 + half - 1:r0 + half, :]
                    parts += [ref - b[r0:r0 + half], b[r0 + half:r0 + 2 * half] - ref]
                e_lvl = jnp.concatenate(parts, axis=0)
            else:
                e_lvl = e_all[(1 + l - n_big) * c:(2 + l - n_big) * c]
            z = (jnp.where(upper, q, k) * jnp.exp(e_lvl)).astype(BF16)
            a = a + lax.dot_general(z, z, _NT, preferred_element_type=F32) * mask_ref[l]
        a = a + lax.dot_general(q.astype(BF16), k.astype(BF16), _NT,
                                preferred_element_type=F32) * mask_ref[len(levels)]
        o = o + jnp.dot(a.astype(BF16), i_b, preferred_element_type=F32)
        o = o * lax.rsqrt(jnp.mean(o * o, axis=-1, keepdims=True) + RMS_EPS) * ng_ref[...]
        gate = g_ref[:, sl].astype(F32)
        o_ref[:, sl] = (o * (gate * _sigmoid(gate))).astype(o_ref.dtype)


def _hgrn_core(q, fl, i, g, lb, ng, bsz, t_len):
    n, d = q.shape
    c = HGRN_CHUNK
    nc = t_len // c
    wall2, masks, levels = _hgrn_constants()
    row = lambda b, j: (b * nc + j, 0)
    fixed2 = lambda b, j: (0, 0)
    return pl.pallas_call(
        functools.partial(_hgrn_kernel, levels=levels),
        grid=(bsz, nc),
        in_specs=[pl.BlockSpec((c, d), row), pl.BlockSpec((c, d), row), pl.BlockSpec((c, d), row),
                  pl.BlockSpec((c, d), row), pl.BlockSpec((1, d), fixed2), pl.BlockSpec((1, HGRN_DIM), fixed2),
                  pl.BlockSpec(wall2.shape, fixed2), pl.BlockSpec(masks.shape, lambda b, j: (0, 0, 0))],
        out_specs=pl.BlockSpec((c, d), row),
        out_shape=jax.ShapeDtypeStruct((n, d), BF16),
        scratch_shapes=[pltpu.VMEM((HGRN_HEADS, HGRN_DIM, HGRN_DIM), F32)],
        compiler_params=_params(("arbitrary", "arbitrary"), 32),
        name="hgrn_core",
    )(q, fl, i, g, lb, ng, jnp.asarray(wall2, BF16), jnp.asarray(masks, F32))


def _hgrn_layer(x, w_in, lower_bound, norm_g, w_out, g, b, router_w, alpha, bsz, t_len):
    d = x.shape[1]
    wb = w_in.astype(BF16)
    q, fl, i, gate = _proj(x, [wb[:, :d], wb[:, d:2 * d], wb[:, 2 * d:3 * d], wb[:, 3 * d:]],
                           [BF16, F32, BF16, BF16], "hgrn_proj")
    o = _hgrn_core(q, fl, i, gate, lower_bound.reshape(1, d), norm_g.reshape(1, HGRN_DIM), bsz, t_len)
    return _outproj_ln(o, w_out.astype(BF16), x, g, b, router_w, alpha, "hgrn_out_ln")


def _conv_kernel(x_ref, win_ref, cw_ref, wout_ref, g_ref, b_ref, *refs, alpha, tm):
    y_ref, zbuf = refs[3], refs[8]
    d = x_ref.shape[1]

    @pl.when(pl.program_id(1) == 0)
    def _():
        zbuf[0:8, :] = jnp.zeros((8, d), F32)

    x = x_ref[...]
    p = jnp.dot(x.astype(BF16), win_ref[...], preferred_element_type=F32)
    z = p[:, d:2 * d] * p[:, 2 * d:]
    zbuf[8:8 + tm, :] = z
    y = cw_ref[2:3, :] * z + cw_ref[1:2, :] * zbuf[7:7 + tm, :] + cw_ref[0:1, :] * zbuf[6:6 + tm, :]
    zbuf[0:8, :] = z[tm - 8:, :]
    mixed = jnp.dot((p[:, :d] * y).astype(BF16), wout_ref[...], preferred_element_type=F32)
    out = _layer_norm(alpha * x + mixed, g_ref[...], b_ref[...])
    y_ref[...] = out
    _route_rows(out, (pl.program_id(0) == 0) & (pl.program_id(1) == 0), *refs[:3], *refs[4:8])


def _conv_layer(x, w_in, conv_w, w_out, g, b, router_w, alpha, bsz, t_len, tm=512):
    n, d = x.shape
    nt = t_len // tm
    row = lambda bb, i: (bb * nt + i, 0)
    fixed = lambda bb, i: (0, 0)
    r_in, r_out, r_shape = _route_specs(n, d, tm, row, fixed)
    y, *routed = pl.pallas_call(
        functools.partial(_conv_kernel, alpha=alpha, tm=tm),
        grid=(bsz, nt),
        in_specs=[pl.BlockSpec((tm, d), row), pl.BlockSpec((d, 3 * d), fixed), pl.BlockSpec(conv_w.shape, fixed),
                  pl.BlockSpec((d, d), fixed), pl.BlockSpec((1, d), fixed), pl.BlockSpec((1, d), fixed)] + r_in,
        out_specs=[pl.BlockSpec((tm, d), row)] + r_out,
        out_shape=[jax.ShapeDtypeStruct((n, d), F32)] + r_shape,
        scratch_shapes=[pltpu.VMEM((tm + 8, d), F32)],
        compiler_params=_params(("arbitrary", "arbitrary"), 56),
        name="conv_layer",
    )(x, w_in.astype(BF16), conv_w, w_out.astype(BF16), g, b, *router_w)
    return y, routed


def _store_token_tiles(xt_ref, v, row0=0):
    rows = v.shape[0]
    for c in range(TOKEN_TILE_ROWS):
        xt_ref[pl.ds(row0 + c, rows, stride=TOKEN_TILE_ROWS), :] = v[:, c * LANES:(c + 1) * LANES]


def _load_token_tiles(xt_ref, rows, row0=0):
    return jnp.concatenate([xt_ref[pl.ds(row0 + c, rows, stride=TOKEN_TILE_ROWS), :]
                            for c in range(TOKEN_TILE_ROWS)], axis=1)


def _route_rows(y, first_step, wh_ref, wl_ref, b_ref, xt_ref, meta_ref, oh_ref, cnt_ref):
    @pl.when(first_step)
    def _():
        cnt_ref[...] = jnp.zeros_like(cnt_ref)

    for r0 in range(0, y.shape[0], MOE_TOK_TILE):
        x = y[r0:r0 + MOE_TOK_TILE, :]
        meta, onehot = _route(x, wh_ref, wl_ref, b_ref)
        meta_ref[r0:r0 + MOE_TOK_TILE, :] = meta
        oh_ref[r0:r0 + MOE_TOK_TILE, :] = onehot.astype(BF16)
        _store_token_tiles(xt_ref, x, r0 * TOKEN_TILE_ROWS)
        cnt_ref[...] += jnp.sum(onehot, axis=0, keepdims=True)


def _route(x, wh_ref, wl_ref, b_ref):
    xh = x.astype(BF16)
    xl = (x - xh.astype(F32)).astype(BF16)
    wh = wh_ref[...]
    logits = (jnp.dot(xh, wh, preferred_element_type=F32) + jnp.dot(xh, wl_ref[...], preferred_element_type=F32)
              + jnp.dot(xl, wh, preferred_element_type=F32)) + b_ref[...]
    lane = lax.broadcasted_iota(jnp.int32, logits.shape, 1)
    lane_f = lane.astype(F32)
    neg = -jnp.inf
    far = float(LANES)
    is_g = (lane >= ROUTER_GROUP_LANE0) & (lane < ROUTER_GROUP_LANE0 + N_GROUPS)
    gl = jnp.where(is_g, logits, neg)
    gmax = jnp.max(gl, axis=-1, keepdims=True)
    g_sel = jnp.min(jnp.where(gl == gmax, lane_f, far), axis=-1, keepdims=True) - float(ROUTER_GROUP_LANE0)
    grp_w = 1.0 / jnp.sum(jnp.where(is_g, jnp.exp(logits - gmax), 0.0), axis=-1, keepdims=True)
    in_g = (lane < N_EXPERTS) & ((lane // EXPERTS_PER_GROUP).astype(F32) == g_sel)
    el = jnp.where(in_g, logits, neg)
    t1 = jnp.max(el, axis=-1, keepdims=True)
    i1 = jnp.min(jnp.where(el == t1, lane_f, far), axis=-1, keepdims=True)
    el2 = jnp.where(lane_f == i1, neg, el)
    t2 = jnp.max(el2, axis=-1, keepdims=True)
    i2 = jnp.min(jnp.where(el2 == t2, lane_f, far), axis=-1, keepdims=True)
    e2 = jnp.exp(t2 - t1)
    w1 = grp_w / (1.0 + e2)
    w2 = w1 * e2
    first_lo = i1 < i2
    lo = jnp.where(first_lo, i1, i2) - g_sel * EXPERTS_PER_GROUP
    hi = jnp.where(first_lo, i2, i1) - g_sel * EXPERTS_PER_GROUP
    pair = lo * (EXPERTS_PER_GROUP - 1.0) - lo * (lo - 1.0) * 0.5 + (hi - lo - 1.0)
    onehot = jnp.where(lane_f == g_sel * MOE_PAIRS + pair, 1.0, 0.0)
    meta = (jnp.where(lane == 0, jnp.where(first_lo, w1, w2), 0.0)
            + jnp.where(lane == 1, jnp.where(first_lo, w2, w1), 0.0))
    return meta, onehot


def _router_weights(w_group, b_group, w_expert, b_expert):
    d = w_group.shape[0]
    w = jnp.zeros((d, LANES), F32).at[:, :N_EXPERTS].set(w_expert)
    w = w.at[:, ROUTER_GROUP_LANE0:ROUTER_GROUP_LANE0 + N_GROUPS].set(w_group)
    bias = jnp.zeros((1, LANES), F32).at[0, :N_EXPERTS].set(b_expert)
    bias = bias.at[0, ROUTER_GROUP_LANE0:ROUTER_GROUP_LANE0 + N_GROUPS].set(b_group)
    wh = w.astype(BF16)
    return wh, (w - wh.astype(F32)).astype(BF16), bias


def _route_specs(n, d, tm, row, fixed):
    assert tm % MOE_TOK_TILE == 0 and d == TOKEN_TILE_ROWS * LANES
    in_specs = [pl.BlockSpec((d, LANES), fixed), pl.BlockSpec((d, LANES), fixed), pl.BlockSpec((1, LANES), fixed)]
    out_specs = [pl.BlockSpec((tm * TOKEN_TILE_ROWS, LANES), row), pl.BlockSpec((tm, LANES), row),
                 pl.BlockSpec((tm, LANES), row), pl.BlockSpec((8, LANES), fixed)]
    out_shape = [jax.ShapeDtypeStruct((n * TOKEN_TILE_ROWS, LANES), F32), jax.ShapeDtypeStruct((n, LANES), F32),
                 jax.ShapeDtypeStruct((n, LANES), BF16), jax.ShapeDtypeStruct((8, LANES), F32)]
    return in_specs, out_specs, out_shape


def _moe_pos_kernel(oh_ref, base_ref, lt_ref, pos_ref, carry_ref):
    @pl.when(pl.program_id(0) == 0)
    def _():
        carry_ref[...] = jnp.zeros_like(carry_ref)

    oh = oh_ref[...]
    rank = jnp.dot(lt_ref[...], oh, preferred_element_type=F32) + carry_ref[...] + base_ref[...]
    ohf = oh.astype(F32)
    val = ohf * rank
    carry_ref[...] += jnp.sum(ohf, axis=0, keepdims=True)
    hi = jnp.floor(val * (1.0 / 256.0))
    lo = val - 256.0 * hi
    ones = jnp.ones((8, LANES), BF16)
    pos = (256.0 * lax.dot_general(ones, hi.astype(BF16), _NT, preferred_element_type=F32)
           + lax.dot_general(ones, lo.astype(BF16), _NT, preferred_element_type=F32))
    pos_ref[0] = pos[0:1].astype(jnp.int32)


def _moe_pos(onehot, base, tm=MOE_TOK_TILE):
    n = onehot.shape[0]
    lt = jnp.asarray(np.tril(np.ones((tm, tm), np.float32), -1), BF16)
    return pl.pallas_call(
        _moe_pos_kernel,
        grid=(n // tm,),
        in_specs=[pl.BlockSpec((tm, LANES), lambda i: (i, 0)), pl.BlockSpec((1, LANES), lambda i: (0, 0)),
                  pl.BlockSpec((tm, tm), lambda i: (0, 0))],
        out_specs=pl.BlockSpec((1, 1, tm), lambda i: (i, 0, 0)),
        out_shape=jax.ShapeDtypeStruct((n // tm, 1, tm), jnp.int32),
        scratch_shapes=[pltpu.VMEM((1, LANES), F32)],
        compiler_params=_params(("arbitrary",), 32),
        name="moe_pos",
    )(onehot, base, lt)


def _tile_copy(src, src_tok, dst, dst_tok, sem):
    rows = TOKEN_TILE_ROWS
    return pltpu.make_async_copy(src.at[pl.ds(pl.multiple_of(src_tok * rows, rows), rows)],
                                 dst.at[pl.ds(pl.multiple_of(dst_tok * rows, rows), rows)], sem)


def _meta_copy(src, src_tok, dst, dst_tok, sem):
    return pltpu.make_async_copy(src.at[pl.ds(src_tok, 1)], dst.at[pl.ds(dst_tok, 1)], sem)


def _moe_scatter_kernel(pos_ref, xt_ref, meta_ref, xs_in_ref, xm_in_ref, xs_ref, xm_ref, sem, *, tm):
    del xs_in_ref, xm_in_ref

    def start(r2, c):
        for par in range(2):
            r = 2 * r2 + par
            p = pos_ref[0, 0, r]
            _tile_copy(xt_ref, r, xs_ref, p, sem.at[0]).start(priority=par)
            _meta_copy(meta_ref, r, xm_ref, p, sem.at[1]).start(priority=1 - par)
        return c

    lax.fori_loop(0, tm // 2, start, 0, unroll=4)
    pltpu.make_async_copy(xt_ref, xs_ref.at[pl.ds(0, tm * TOKEN_TILE_ROWS)], sem.at[0]).wait()
    pltpu.make_async_copy(meta_ref, xm_ref.at[pl.ds(0, tm)], sem.at[1]).wait()


def _moe_scatter(pos, xt, meta, xs_buf, xm_buf, tm=2 * MOE_TOK_TILE):
    n = meta.shape[0]
    pos = pos.reshape(n // tm, 1, tm)
    return pl.pallas_call(
        functools.partial(_moe_scatter_kernel, tm=tm),
        grid=(n // tm,),
        in_specs=[pl.BlockSpec((1, 1, tm), lambda i: (i, 0, 0), memory_space=pltpu.SMEM),
                  pl.BlockSpec((tm * TOKEN_TILE_ROWS, LANES), lambda i: (i, 0)),
                  pl.BlockSpec((tm, LANES), lambda i: (i, 0)),
                  pl.BlockSpec(memory_space=pl.ANY), pl.BlockSpec(memory_space=pl.ANY)],
        out_specs=[pl.BlockSpec(memory_space=pl.ANY), pl.BlockSpec(memory_space=pl.ANY)],
        out_shape=[jax.ShapeDtypeStruct(xs_buf.shape, xs_buf.dtype), jax.ShapeDtypeStruct(xm_buf.shape, xm_buf.dtype)],
        scratch_shapes=[pltpu.SemaphoreType.DMA((2,))],
        input_output_aliases={3: 0, 4: 1},
        compiler_params=_params(("arbitrary",), 32),
        name="moe_scatter",
    )(pos, xt, meta, xs_buf, xm_buf)


def _moe_experts_kernel(grp_ref, e1_ref, e2_ref, nused_ref, xs_ref, xm_ref, wg_ref, wu_ref, wd_ref, ys_in_ref, ys_ref):
    del grp_ref, ys_in_ref
    i = pl.program_id(0)
    tr = xm_ref.shape[0] // MOE_TILES_PER_STEP

    @pl.when(i * MOE_TILES_PER_STEP < nused_ref[0])
    def _():
        for t in range(MOE_TILES_PER_STEP):
            x = _load_token_tiles(xs_ref, tr, t * tr * TOKEN_TILE_ROWS).astype(BF16)
            meta = xm_ref[t * tr:(t + 1) * tr, :]
            y = None
            for k, e_ref in enumerate((e1_ref, e2_ref)):
                e = e_ref[i * MOE_TILES_PER_STEP + t]
                hg = jnp.dot(x, wg_ref[0, e], preferred_element_type=F32)
                hu = jnp.dot(x, wu_ref[0, e], preferred_element_type=F32)
                h = hg * _sigmoid(hg) * hu * meta[:, k:k + 1]
                yk = jnp.dot(h.astype(BF16), wd_ref[0, e], preferred_element_type=F32)
                y = yk if y is None else y + yk
            _store_token_tiles(ys_ref, y, t * tr * TOKEN_TILE_ROWS)


def _moe_experts(xs_buf, xm_buf, ys_buf, tile_grp, tile_e1, tile_e2, n_used, w_gate, w_up, w_down, layer,
                 tr=MOE_ROW_TILE):
    r_max = xm_buf.shape[0]
    d, f = w_gate.shape[-2:]
    assert d == TOKEN_TILE_ROWS * LANES
    tps = MOE_TILES_PER_STEP
    step_rows = tps * tr
    tile_rows = step_rows * TOKEN_TILE_ROWS
    epg = EXPERTS_PER_GROUP
    used = lambda i, grp, e1, e2, nu: (jnp.minimum(i, nu[0] // tps - 1), 0)
    wmap = lambda i, grp, e1, e2, nu: (layer * N_GROUPS + grp[i * tps], 0, 0, 0)
    grid_spec = pltpu.PrefetchScalarGridSpec(
        num_scalar_prefetch=4,
        grid=(r_max // step_rows,),
        in_specs=[pl.BlockSpec((tile_rows, LANES), used), pl.BlockSpec((step_rows, LANES), used),
                  pl.BlockSpec((1, epg, d, f), wmap), pl.BlockSpec((1, epg, d, f), wmap),
                  pl.BlockSpec((1, epg, f, d), wmap), pl.BlockSpec(memory_space=pl.ANY)],
        out_specs=pl.BlockSpec((tile_rows, LANES), used),
    )
    return pl.pallas_call(
        _moe_experts_kernel,
        grid_spec=grid_spec,
        out_shape=jax.ShapeDtypeStruct(ys_buf.shape, ys_buf.dtype),
        input_output_aliases={9: 0},
        compiler_params=_params(("arbitrary",), 48),
        name="moe_experts",
    )(tile_grp, tile_e1, tile_e2, n_used, xs_buf, xm_buf, w_gate, w_up, w_down, ys_buf)


def _moe_combine_ln_kernel(pos_ref, nxt_ref, x_ref, ys_ref, g_ref, b_ref, y_ref, buf_ref, sem, *, alpha, tm):
    i = pl.program_id(0)
    slot = i % 2

    def gather(p_ref, s):
        def start(r2, c):
            for par in range(2):
                r = 2 * r2 + par
                _tile_copy(ys_ref, p_ref[0, 0, r], buf_ref, s * tm + r, sem.at[s]).start(priority=par)
            return c
        lax.fori_loop(0, tm // 2, start, 0, unroll=4)

    @pl.when(i == 0)
    def _():
        gather(pos_ref, 0)

    @pl.when(i + 1 < pl.num_programs(0))
    def _():
        gather(nxt_ref, 1 - slot)

    base = pl.multiple_of(slot * (tm * TOKEN_TILE_ROWS), tm * TOKEN_TILE_ROWS)
    slot_rows = pl.ds(base, tm * TOKEN_TILE_ROWS)
    pltpu.make_async_copy(ys_ref.at[pl.ds(0, tm * TOKEN_TILE_ROWS)], buf_ref.at[slot_rows], sem.at[slot]).wait()
    ffn = _load_token_tiles(buf_ref, tm, base)
    y_ref[...] = _layer_norm(alpha * x_ref[...] + ffn, g_ref[...], b_ref[...])


def _moe_combine_ln(pos, x, ys_buf, g, b, alpha, tm=MOE_TOK_TILE):
    n, d = x.shape
    nt = n // tm
    row = lambda i: (i, 0)
    fixed = lambda i: (0, 0)
    return pl.pallas_call(
        functools.partial(_moe_combine_ln_kernel, alpha=alpha, tm=tm),
        grid=(nt,),
        in_specs=[pl.BlockSpec((1, 1, tm), lambda i: (i, 0, 0), memory_space=pltpu.SMEM),
                  pl.BlockSpec((1, 1, tm), lambda i: (jnp.minimum(i + 1, nt - 1), 0, 0), memory_space=pltpu.SMEM),
                  pl.BlockSpec((tm, d), row), pl.BlockSpec(memory_space=pl.ANY),
                  pl.BlockSpec((1, d), fixed), pl.BlockSpec((1, d), fixed)],
        out_specs=pl.BlockSpec((tm, d), row),
        out_shape=jax.ShapeDtypeStruct((n, d), F32),
        scratch_shapes=[pltpu.VMEM((2 * tm * TOKEN_TILE_ROWS, LANES), F32), pltpu.SemaphoreType.DMA((2,))],
        compiler_params=_params(("arbitrary",), 32),
        name="moe_combine_ln",
    )(pos, pos, x, ys_buf, g, b)


def _bucket_experts():
    lo, hi = [], []
    for a in range(EXPERTS_PER_GROUP):
        for c in range(a + 1, EXPERTS_PER_GROUP):
            lo.append(a)
            hi.append(c)
    grp = np.repeat(np.arange(N_GROUPS), MOE_PAIRS).astype(np.int32)
    return grp, np.tile(lo, N_GROUPS).astype(np.int32), np.tile(hi, N_GROUPS).astype(np.int32)


def _moe_layer(x, routed, bufs, w_gate, w_up, w_down, layer, g, b, alpha):
    tr = MOE_ROW_TILE
    xs_buf, xm_buf, ys_buf = bufs
    xt, meta, onehot, cnt = routed
    counts = cnt[0, :MOE_BUCKETS].astype(jnp.int32)
    padded = ((counts + (tr - 1)) // tr * tr).reshape(N_GROUPS, MOE_PAIRS)
    step_rows = MOE_TILES_PER_STEP * tr
    slack = -jnp.sum(padded, axis=1) % step_rows
    padded = padded.at[:, MOE_PAIRS - 1].add(slack).reshape(MOE_BUCKETS)
    ends = jnp.cumsum(padded)
    base = jnp.zeros((1, LANES), F32).at[0, :MOE_BUCKETS].set((ends - padded).astype(F32))
    n_used = ends[-1] // tr
    n_tiles = xm_buf.shape[0] // tr
    tile = jnp.minimum(jnp.arange(n_tiles, dtype=jnp.int32), n_used - 1)
    tile_bkt = jnp.sum((ends[None, :] <= (tile * tr)[:, None]).astype(jnp.int32), axis=1)
    tile_bkt = jnp.minimum(tile_bkt, MOE_BUCKETS - 1)
    b_grp, b_lo, b_hi = (jnp.asarray(t)[tile_bkt] for t in _bucket_experts())
    pos = _moe_pos(onehot, base)
    xs_buf, xm_buf = _moe_scatter(pos, xt, meta, xs_buf, xm_buf)
    ys_buf = _moe_experts(xs_buf, xm_buf, ys_buf, b_grp, b_lo, b_hi, n_used.reshape(1).astype(jnp.int32),
                          w_gate, w_up, w_down, layer)
    return _moe_combine_ln(pos, x, ys_buf, g, b, alpha), (xs_buf, xm_buf, ys_buf)


def kernel(x, ln_mix_g, ln_mix_b, ln_ffn_g, ln_ffn_b, fox_w_in, fox_b_f, fox_w_out, hgrn_w_in, hgrn_lb_logits, hgrn_norm_g, hgrn_w_out, conv_w_in, conv_w, conv_w_out, moe_w_group, moe_b_group, moe_w_expert, moe_b_expert, moe_w_gate, moe_w_up, moe_w_down):
    bsz, t_len, d = x.shape
    depth = ln_mix_g.shape[0]
    alpha = float((2 * depth) ** 0.25)
    assert d == FOX_HEADS * FOX_HEAD_DIM == HGRN_HEADS * HGRN_DIM
    assert t_len % 512 == 0 and (bsz * t_len) % 1024 == 0

    lb_prob = jax.nn.softmax(hgrn_lb_logits.astype(F32), axis=0)
    lower_bounds = jnp.cumsum(lb_prob, axis=0) - lb_prob[0]

    h = x.reshape(bsz * t_len, d)
    r_max = bsz * t_len + MOE_BUCKETS * MOE_ROW_TILE + N_GROUPS * MOE_TILES_PER_STEP * MOE_ROW_TILE
    assert r_max % (MOE_TILES_PER_STEP * MOE_ROW_TILE) == 0
    by_group = lambda w: w.astype(BF16).reshape((depth * N_GROUPS, EXPERTS_PER_GROUP) + w.shape[2:])
    moe_w_gate, moe_w_up, moe_w_down = by_group(moe_w_gate), by_group(moe_w_up), by_group(moe_w_down)
    bufs = (jnp.zeros((r_max * TOKEN_TILE_ROWS, LANES), F32), jnp.zeros((r_max, LANES), F32),
            jnp.zeros((r_max * TOKEN_TILE_ROWS, LANES), F32))
    for layer in range(depth):
        kind, j = layer % 3, layer // 3
        g_mix, b_mix = ln_mix_g[layer].reshape(1, d), ln_mix_b[layer].reshape(1, d)
        router_w = _router_weights(moe_w_group[layer], moe_b_group[layer], moe_w_expert[layer], moe_b_expert[layer])
        if kind == 0:
            h, routed = _fox_layer(h, fox_w_in[j], fox_b_f[j], fox_w_out[j], g_mix, b_mix, router_w,
                                   alpha, bsz, t_len)
        elif kind == 1:
            h, routed = _hgrn_layer(h, hgrn_w_in[j], lower_bounds[layer], hgrn_norm_g[j], hgrn_w_out[j],
                                    g_mix, b_mix, router_w, alpha, bsz, t_len)
        else:
            h, routed = _conv_layer(h, conv_w_in[j], conv_w[j], conv_w_out[j], g_mix, b_mix, router_w,
                                    alpha, bsz, t_len)
        h, bufs = _moe_layer(h, routed, bufs, moe_w_gate, moe_w_up, moe_w_down, layer,
                             ln_ffn_g[layer].reshape(1, d), ln_ffn_b[layer].reshape(1, d), alpha)
    return h.reshape(bsz, t_len, d)
```

```python
import functools

import numpy as np
import jax
import jax.numpy as jnp
from jax import lax
from jax.experimental import pallas as pl
from jax.experimental.pallas import tpu as pltpu

F32 = jnp.float32
BF16 = jnp.bfloat16

FOX_HEADS = 16
FOX_HEAD_DIM = 64
FOX_PAIRS_PER_STEP = 4
HGRN_HEADS = 8
HGRN_DIM = 128
HGRN_CHUNK = 128
HGRN_SEQS_PER_STEP = 2
HGRN_SMALL_LEVEL = 8
N_GROUPS = 4
EXPERTS_PER_GROUP = 8
N_EXPERTS = N_GROUPS * EXPERTS_PER_GROUP
LN_EPS = 1e-5
RMS_EPS = 1e-6
LOG2E = 1.4426950408889634
LANES = 128
ROUTER_GROUP_LANE0 = N_EXPERTS
MOE_PAIRS = EXPERTS_PER_GROUP * (EXPERTS_PER_GROUP - 1) // 2
MOE_BUCKETS = N_GROUPS * MOE_PAIRS
MOE_ROW_TILE = 128
MOE_TILES_PER_STEP = 4
MOE_TOK_TILE = 256
TOKEN_TILE_ROWS = 8

_NT = (((1,), (1,)), ((), ()))
_TN = (((0,), (0,)), ((), ()))


def _params(semantics, vmem_mb):
    return pltpu.CompilerParams(dimension_semantics=semantics,
                                vmem_limit_bytes=vmem_mb * 1024 * 1024)


def _layer_norm(v, g, b):
    mu = jnp.mean(v, axis=-1, keepdims=True)
    d = v - mu
    var = jnp.mean(d * d, axis=-1, keepdims=True)
    return d * lax.rsqrt(var + LN_EPS) * g + b


def _sigmoid(v):
    return 1.0 / (1.0 + jnp.exp(-v))


def _proj_kernel(x_ref, *refs, n_out):
    xb = x_ref[...].astype(BF16)
    for w_ref, o_ref in zip(refs[:n_out], refs[n_out:]):
        o_ref[...] = jnp.dot(xb, w_ref[...], preferred_element_type=F32).astype(o_ref.dtype)


def _proj(x, ws, dtypes, name, tm=512):
    n, k = x.shape
    return pl.pallas_call(
        functools.partial(_proj_kernel, n_out=len(ws)),
        grid=(n // tm,),
        in_specs=[pl.BlockSpec((tm, k), lambda i: (i, 0))]
        + [pl.BlockSpec(w.shape, lambda i: (0, 0)) for w in ws],
        out_specs=[pl.BlockSpec((tm, w.shape[1]), lambda i: (i, 0)) for w in ws],
        out_shape=[jax.ShapeDtypeStruct((n, w.shape[1]), dt) for w, dt in zip(ws, dtypes)],
        compiler_params=_params(("arbitrary",), 48),
        name=name,
    )(x, *ws)


def _outproj_ln_kernel(o_ref, w_ref, x_ref, g_ref, b_ref, *refs, alpha, transposed):
    y_ref = refs[3]
    dims = _TN if transposed else (((1,), (0,)), ((), ()))
    mixed = lax.dot_general(o_ref[...], w_ref[...], dims, preferred_element_type=F32)
    y = _layer_norm(alpha * x_ref[...] + mixed, g_ref[...], b_ref[...])
    y_ref[...] = y
    _route_rows(y, pl.program_id(0) == 0, *refs[:3], *refs[4:])


def _outproj_ln(o, w, x, g, b, router_w, alpha, name, tm=512, transposed=False):
    n, d = x.shape
    k = w.shape[0]
    row = lambda i: (i, 0)
    fixed = lambda i: (0, 0)
    o_spec = pl.BlockSpec((k, tm), lambda i: (0, i)) if transposed else pl.BlockSpec((tm, k), row)
    r_in, r_out, r_shape = _route_specs(n, d, tm, row, fixed)
    y, *routed = pl.pallas_call(
        functools.partial(_outproj_ln_kernel, alpha=alpha, transposed=transposed),
        grid=(n // tm,),
        in_specs=[o_spec, pl.BlockSpec((k, d), fixed),
                  pl.BlockSpec((tm, d), row), pl.BlockSpec((1, d), fixed), pl.BlockSpec((1, d), fixed)] + r_in,
        out_specs=[pl.BlockSpec((tm, d), row)] + r_out,
        out_shape=[jax.ShapeDtypeStruct((n, d), F32)] + r_shape,
        compiler_params=_params(("arbitrary",), 48),
        name=name,
    )(o, w, x, g, b, *router_w)
    return y, routed


def _split3(v):
    hi = v.astype(BF16)
    r1 = v - hi.astype(F32)
    mid = r1.astype(BF16)
    return hi, mid, (r1 - mid.astype(F32)).astype(BF16)


def _fox_proj_kernel(x_ref, wqt_ref, wk_ref, wvt_ref, qt_ref, k_ref, vt_ref):
    xb = x_ref[...].astype(BF16)
    qt_ref[...] = lax.dot_general(wqt_ref[...], xb, _NT, preferred_element_type=F32).astype(BF16)
    k_ref[...] = jnp.dot(xb, wk_ref[...], preferred_element_type=F32).astype(BF16)
    vt_ref[...] = lax.dot_general(wvt_ref[...], xb, _NT, preferred_element_type=F32).astype(BF16)


def _fox_proj(x, wqt, wk, wvt, tm=512):
    n, d = x.shape
    fixed = lambda i: (0, 0)
    return pl.pallas_call(
        _fox_proj_kernel,
        grid=(n // tm,),
        in_specs=[pl.BlockSpec((tm, d), lambda i: (i, 0)), pl.BlockSpec((d, d), fixed),
                  pl.BlockSpec((d, d), fixed), pl.BlockSpec((d, d), fixed)],
        out_specs=[pl.BlockSpec((d, tm), lambda i: (0, i)), pl.BlockSpec((tm, d), lambda i: (i, 0)),
                   pl.BlockSpec((d, tm), lambda i: (0, i))],
        out_shape=[jax.ShapeDtypeStruct((d, n), BF16), jax.ShapeDtypeStruct((n, d), BF16),
                   jax.ShapeDtypeStruct((d, n), BF16)],
        compiler_params=_params(("arbitrary",), 48),
        name="fox_proj",
    )(x, wqt, wk, wvt)


def _fox_gate_kernel(x_ref, wf_ref, bf_ref, tri_ref, sel_ref, cb_ref, carry_ref):
    @pl.when(pl.program_id(1) == 0)
    def _():
        carry_ref[...] = jnp.zeros_like(carry_ref)

    z = jnp.dot(x_ref[...].astype(BF16), wf_ref[...], preferred_element_type=F32) + bf_ref[...]
    logf = jnp.minimum(z, 0.0) - jnp.log(1.0 + jnp.exp(-jnp.abs(z)))
    c3 = jnp.dot(tri_ref[...], jnp.concatenate(_split3(logf), axis=1), preferred_element_type=F32)
    c = carry_ref[...] + c3[:, :LANES] + c3[:, LANES:2 * LANES] + c3[:, 2 * LANES:]
    carry_ref[...] = c[c.shape[0] - 1:, :]
    neg3 = jnp.concatenate(_split3(c * (-LOG2E)), axis=1)
    cb_ref[...] = jnp.dot(neg3, sel_ref[...], preferred_element_type=F32).astype(BF16)


def _fox_gate(x, wf, bf, bsz, t_len, tg=512):
    n, d = x.shape
    nt = t_len // tg
    tri = jnp.asarray(np.tril(np.ones((tg, tg), np.float32)), BF16)
    sel = np.zeros((3 * LANES, d), np.float32)
    for h in range(FOX_HEADS):
        for j in range(3):
            sel[j * LANES + h, (h // 2) * LANES + 3 * (h % 2) + j] = 1.0
    return pl.pallas_call(
        _fox_gate_kernel,
        grid=(bsz, nt),
        in_specs=[pl.BlockSpec((tg, d), lambda b, i: (b * nt + i, 0)),
                  pl.BlockSpec((d, LANES), lambda b, i: (0, 0)),
                  pl.BlockSpec((1, LANES), lambda b, i: (0, 0)),
                  pl.BlockSpec((tg, tg), lambda b, i: (0, 0)),
                  pl.BlockSpec((3 * LANES, d), lambda b, i: (0, 0))],
        out_specs=pl.BlockSpec((tg, d), lambda b, i: (b * nt + i, 0)),
        out_shape=jax.ShapeDtypeStruct((n, d), BF16),
        scratch_shapes=[pltpu.VMEM((1, LANES), F32)],
        compiler_params=_params(("arbitrary", "arbitrary"), 32),
        name="fox_gate",
    )(x, wf, bf, tri, jnp.asarray(sel, BF16))


def _fox_attn_kernel(qt_ref, k_ref, cb_ref, vt_ref, ot_ref, s_ref, *, tq):
    qi = pl.program_id(2)
    n_heads = 2 * FOX_PAIRS_PER_STEP
    feat = lax.broadcasted_iota(jnp.int32, (LANES, tq), 0)
    rhs = []
    for h in range(n_heads):
        pair, hh = divmod(h, 2)
        qt = qt_ref[pair * LANES:(pair + 1) * LANES, :].astype(F32)
        own = (feat >= hh * FOX_HEAD_DIM) & (feat < (hh + 1) * FOX_HEAD_DIM)
        bias_rows = (feat >= 3 * hh) & (feat < 3 * hh + 3)
        rhs.append(jnp.concatenate([jnp.where(own, qt, 0.0).astype(BF16),
                                    jnp.where(bias_rows, 1.0, 0.0).astype(BF16)], axis=0))
    key_i = lax.broadcasted_iota(jnp.int32, (tq, tq), 0)
    qry_i = lax.broadcasted_iota(jnp.int32, (tq, tq), 1)
    ones_rows = jnp.ones((16, tq), BF16)

    def scores(j, slot):
        start = pl.multiple_of(j * tq, tq)
        for pair in range(FOX_PAIRS_PER_STEP):
            lanes = slice(pair * LANES, (pair + 1) * LANES)
            kext = jnp.concatenate([k_ref[pl.ds(start, tq), lanes], cb_ref[pl.ds(start, tq), lanes]], axis=1)
            for h in (2 * pair, 2 * pair + 1):
                s_ref[slot, h] = jnp.dot(kext, rhs[h], preferred_element_type=F32)

    def block(j, carry, slot, masked):
        start = pl.multiple_of(j * tq, tq)
        if not masked:
            scores(j + 1, 1 - slot)
        out = []
        for h in range(n_heads):
            m, l, acc = carry[h]
            st = s_ref[slot, h]
            if masked:
                st = jnp.where(key_i <= qry_i, st, -jnp.inf)
            m_new = jnp.maximum(m, jnp.max(st, axis=0, keepdims=True))
            p = jnp.exp2(st - m_new)
            a = jnp.exp2(m - m_new)
            vt = vt_ref[h * FOX_HEAD_DIM:(h + 1) * FOX_HEAD_DIM, pl.ds(start, tq)]
            pv = jnp.dot(jnp.concatenate([vt, ones_rows], axis=0), p.astype(BF16), preferred_element_type=F32)
            out.append((m_new, a * l + pv[FOX_HEAD_DIM:FOX_HEAD_DIM + 1], acc * a + pv[:FOX_HEAD_DIM]))
        return tuple(out)

    init = (jnp.full((1, tq), -jnp.inf, F32), jnp.zeros((1, tq), F32), jnp.zeros((FOX_HEAD_DIM, tq), F32))
    scores(0, 0)
    carry = lax.fori_loop(
        0, qi // 2, lambda i, c: block(2 * i + 1, block(2 * i, c, 0, False), 1, False), (init,) * n_heads)
    carry = lax.cond(
        qi % 2 == 0,
        lambda c: block(qi, c, 0, True),
        lambda c: block(qi, block(qi - 1, c, 0, False), 1, True),
        carry)
    for h in range(n_heads):
        _, l, acc = carry[h]
        ot_ref[h * FOX_HEAD_DIM:(h + 1) * FOX_HEAD_DIM, :] = (acc / l).astype(ot_ref.dtype)


def _fox_attn(qt, k, cb, vt, bsz, t_len, tq=256):
    d, n = qt.shape
    w = FOX_PAIRS_PER_STEP * LANES
    nq = t_len // tq
    return pl.pallas_call(
        functools.partial(_fox_attn_kernel, tq=tq),
        grid=(bsz, d // w, nq),
        in_specs=[pl.BlockSpec((w, tq), lambda b, p, i: (p, b * nq + i)),
                  pl.BlockSpec((t_len, w), lambda b, p, i: (b, p)),
                  pl.BlockSpec((t_len, w), lambda b, p, i: (b, p)),
                  pl.BlockSpec((w, t_len), lambda b, p, i: (p, b))],
        out_specs=pl.BlockSpec((w, tq), lambda b, p, i: (p, b * nq + i)),
        out_shape=jax.ShapeDtypeStruct((d, n), BF16),
        scratch_shapes=[pltpu.VMEM((2, 2 * FOX_PAIRS_PER_STEP, tq, tq), F32)],
        compiler_params=_params(("arbitrary", "arbitrary", "arbitrary"), 32),
        name="fox_attn",
    )(qt, k, cb, vt)


def _fox_layer(x, w_in, b_f, w_out, g, b, router_w, alpha, bsz, t_len):
    d = x.shape[1]
    wqt = (w_in[:, :d] * (FOX_HEAD_DIM ** -0.5 * LOG2E)).T.astype(BF16)
    wk = w_in[:, d:2 * d].astype(BF16)
    wvt = w_in[:, 2 * d:3 * d].T.astype(BF16)
    wf = jnp.zeros((d, LANES), F32).at[:, :FOX_HEADS].set(w_in[:, 3 * d:]).astype(BF16)
    bf = jnp.zeros((1, LANES), F32).at[0, :FOX_HEADS].set(b_f)
    qt, k, vt = _fox_proj(x, wqt, wk, wvt)
    cb = _fox_gate(x, wf, bf, bsz, t_len)
    ot = _fox_attn(qt, k, cb, vt, bsz, t_len)
    return _outproj_ln(ot, w_out.astype(BF16), x, g, b, router_w, alpha, "fox_out_ln", transposed=True)


def _hgrn_constants():
    c = HGRN_CHUNK
    r = np.arange(c)[:, None]
    j = np.arange(c)[None, :]
    blocks = [j <= r]
    masks = []
    levels = []
    half = c // 2
    while half >= 1:
        ref = (r // (2 * half)) * (2 * half) + half - 1
        upper = (r % (2 * half)) >= half
        if half < HGRN_SMALL_LEVEL:
            blocks.append(np.where(upper, (j > ref) & (j <= r), (j > r) & (j <= ref)))
        masks.append(((r // (2 * half)) == (j // (2 * half))) & upper & ((j % (2 * half)) < half))
        levels.append(half)
        half //= 2
    masks.append(r == j)
    wall = np.concatenate(blocks, axis=0).astype(np.float32)
    wall2 = np.concatenate([wall, wall], axis=1)
    return wall2, np.stack(masks).astype(np.float32), tuple(levels)


def _hgrn_kernel(q_ref, fl_ref, i_ref, g_ref, lb_ref, ng_ref, wall_ref, mask_ref, o_ref, s_ref, *, levels):
    c = HGRN_CHUNK

    @pl.when(pl.program_id(1) == 0)
    def _():
        s_ref[...] = jnp.zeros_like(s_ref)

    wall = wall_ref[...]
    rowi = lax.broadcasted_iota(jnp.int32, (c, HGRN_DIM), 0)
    e_pair = None
    for sh in range(q_ref.shape[1] * HGRN_HEADS):
        seq, h = divmod(sh, HGRN_HEADS)
        q_v, fl_v, i_v, g_v, o_v = (r.at[0, seq] for r in (q_ref, fl_ref, i_ref, g_ref, o_ref))
        sl = slice(h * HGRN_DIM, (h + 1) * HGRN_DIM)
        if h % 2 == 0:
            sl2 = slice(h * HGRN_DIM, (h + 2) * HGRN_DIM)
            lb2 = lb_ref[:, sl2]
            f2 = lb2 + (1.0 - lb2) * _sigmoid(fl_v[:, sl2])
            logf2 = jnp.log(f2)
            hi = logf2.astype(BF16)
            mid = (logf2 - hi.astype(F32)).astype(BF16)
            e_pair = jnp.dot(wall, jnp.concatenate([hi, mid], axis=0), preferred_element_type=F32)
        lane0 = (h % 2) * HGRN_DIM
        q = q_v[:, sl].astype(F32)
        i_b = i_v[:, sl]
        k = 1.0 - f2[:, lane0:lane0 + HGRN_DIM]
        e_all = e_pair[:, lane0:lane0 + HGRN_DIM]
        b = e_all[0:c]
        x_pre = jnp.exp(b)
        x_suf = jnp.exp(b[c - 1:c, :] - b)
        st = s_ref[sh]
        o = lax.dot_general((q * x_pre).astype(BF16), st.astype(BF16), _NT, preferred_element_type=F32)
        upd = lax.dot_general(i_b, (k * x_suf).astype(BF16), _TN, preferred_element_type=F32)
        s_ref[sh] = st * x_pre[c - 1:c, :] + upd
        a = jnp.zeros((c, c), F32)
        n_big = sum(half >= HGRN_SMALL_LEVEL for half in levels)
        for l, half in enumerate(levels):
            upper = (rowi & half) != 0
            if half >= HGRN_SMALL_LEVEL:
                parts = []
                for r0 in range(0, c, 2 * half):
                    ref = b[r0 + half - 1:r0 + half, :]
                    parts += [ref - b[r0:r0 + half], b[r0 + half:r0 + 2 * half] - ref]
                e_lvl = jnp.concatenate(parts, axis=0)
            else:
                e_lvl = e_all[(1 + l - n_big) * c:(2 + l - n_big) * c]
            z = (jnp.where(upper, q, k) * jnp.exp(e_lvl)).astype(BF16)
            a = a + lax.dot_general(z, z, _NT, preferred_element_type=F32) * mask_ref[l]
        a = a + lax.dot_general(q.astype(BF16), k.astype(BF16), _NT,
                                preferred_element_type=F32) * mask_ref[len(levels)]
        o = o + jnp.dot(a.astype(BF16), i_b, preferred_element_type=F32)
        o = o * lax.rsqrt(jnp.mean(o * o, axis=-1, keepdims=True) + RMS_EPS) * ng_ref[...]
        gate = g_v[:, sl].astype(F32)
        o_v[:, sl] = (o * (gate * _sigmoid(gate))).astype(o_ref.dtype)


def _hgrn_core(q, fl, i, g, lb, ng, bsz, t_len):
    n, d = q.shape
    c = HGRN_CHUNK
    nc = t_len // c
    sps = HGRN_SEQS_PER_STEP
    wall2, masks, levels = _hgrn_constants()
    by_seq = lambda a: a.reshape(bsz // sps, sps, t_len, d)
    row = pl.BlockSpec((1, sps, c, d), lambda b, j: (b, 0, j, 0))
    fixed2 = lambda b, j: (0, 0)
    out = pl.pallas_call(
        functools.partial(_hgrn_kernel, levels=levels),
        grid=(bsz // sps, nc),
        in_specs=[row, row, row, row, pl.BlockSpec((1, d), fixed2), pl.BlockSpec((1, HGRN_DIM), fixed2),
                  pl.BlockSpec(wall2.shape, fixed2), pl.BlockSpec(masks.shape, lambda b, j: (0, 0, 0))],
        out_specs=row,
        out_shape=jax.ShapeDtypeStruct((bsz // sps, sps, t_len, d), BF16),
        scratch_shapes=[pltpu.VMEM((sps * HGRN_HEADS, HGRN_DIM, HGRN_DIM), F32)],
        compiler_params=_params(("arbitrary", "arbitrary"), 32),
        name="hgrn_core",
    )(by_seq(q), by_seq(fl), by_seq(i), by_seq(g), lb, ng, jnp.asarray(wall2, BF16), jnp.asarray(masks, F32))
    return out.reshape(n, d)


def _hgrn_layer(x, w_in, lower_bound, norm_g, w_out, g, b, router_w, alpha, bsz, t_len):
    d = x.shape[1]
    wb = w_in.astype(BF16)
    q, fl, i, gate = _proj(x, [wb[:, :d], wb[:, d:2 * d], wb[:, 2 * d:3 * d], wb[:, 3 * d:]],
                           [BF16, F32, BF16, BF16], "hgrn_proj")
    o = _hgrn_core(q, fl, i, gate, lower_bound.reshape(1, d), norm_g.reshape(1, HGRN_DIM), bsz, t_len)
    return _outproj_ln(o, w_out.astype(BF16), x, g, b, router_w, alpha, "hgrn_out_ln")


def _conv_kernel(x_ref, win_ref, cw_ref, wout_ref, g_ref, b_ref, *refs, alpha, tm):
    y_ref, zbuf = refs[3], refs[8]
    d = x_ref.shape[1]

    @pl.when(pl.program_id(1) == 0)
    def _():
        zbuf[0:8, :] = jnp.zeros((8, d), F32)

    x = x_ref[...]
    p = jnp.dot(x.astype(BF16), win_ref[...], preferred_element_type=F32)
    z = p[:, d:2 * d] * p[:, 2 * d:]
    zbuf[8:8 + tm, :] = z
    y = cw_ref[2:3, :] * z + cw_ref[1:2, :] * zbuf[7:7 + tm, :] + cw_ref[0:1, :] * zbuf[6:6 + tm, :]
    zbuf[0:8, :] = z[tm - 8:, :]
    mixed = jnp.dot((p[:, :d] * y).astype(BF16), wout_ref[...], preferred_element_type=F32)
    out = _layer_norm(alpha * x + mixed, g_ref[...], b_ref[...])
    y_ref[...] = out
    _route_rows(out, (pl.program_id(0) == 0) & (pl.program_id(1) == 0), *refs[:3], *refs[4:8])


def _conv_layer(x, w_in, conv_w, w_out, g, b, router_w, alpha, bsz, t_len, tm=512):
    n, d = x.shape
    nt = t_len // tm
    row = lambda bb, i: (bb * nt + i, 0)
    fixed = lambda bb, i: (0, 0)
    r_in, r_out, r_shape = _route_specs(n, d, tm, row, fixed)
    y, *routed = pl.pallas_call(
        functools.partial(_conv_kernel, alpha=alpha, tm=tm),
        grid=(bsz, nt),
        in_specs=[pl.BlockSpec((tm, d), row), pl.BlockSpec((d, 3 * d), fixed), pl.BlockSpec(conv_w.shape, fixed),
                  pl.BlockSpec((d, d), fixed), pl.BlockSpec((1, d), fixed), pl.BlockSpec((1, d), fixed)] + r_in,
        out_specs=[pl.BlockSpec((tm, d), row)] + r_out,
        out_shape=[jax.ShapeDtypeStruct((n, d), F32)] + r_shape,
        scratch_shapes=[pltpu.VMEM((tm + 8, d), F32)],
        compiler_params=_params(("arbitrary", "arbitrary"), 56),
        name="conv_layer",
    )(x, w_in.astype(BF16), conv_w, w_out.astype(BF16), g, b, *router_w)
    return y, routed


def _store_token_tiles(xt_ref, v, row0=0):
    rows = v.shape[0]
    for c in range(TOKEN_TILE_ROWS):
        xt_ref[pl.ds(row0 + c, rows, stride=TOKEN_TILE_ROWS), :] = v[:, c * LANES:(c + 1) * LANES]


def _load_token_tiles(xt_ref, rows, row0=0):
    return jnp.concatenate([xt_ref[pl.ds(row0 + c, rows, stride=TOKEN_TILE_ROWS), :]
                            for c in range(TOKEN_TILE_ROWS)], axis=1)


def _route_rows(y, first_step, wh_ref, wl_ref, b_ref, xt_ref, meta_ref, oh_ref, cnt_ref):
    @pl.when(first_step)
    def _():
        cnt_ref[...] = jnp.zeros_like(cnt_ref)

    for r0 in range(0, y.shape[0], MOE_TOK_TILE):
        x = y[r0:r0 + MOE_TOK_TILE, :]
        meta, onehot = _route(x, wh_ref, wl_ref, b_ref)
        meta_ref[r0:r0 + MOE_TOK_TILE, :] = meta
        oh_ref[r0:r0 + MOE_TOK_TILE, :] = onehot.astype(BF16)
        _store_token_tiles(xt_ref, x, r0 * TOKEN_TILE_ROWS)
        cnt_ref[...] += jnp.sum(onehot, axis=0, keepdims=True)


def _route(x, wh_ref, wl_ref, b_ref):
    xh = x.astype(BF16)
    xl = (x - xh.astype(F32)).astype(BF16)
    wh = wh_ref[...]
    logits = (jnp.dot(xh, wh, preferred_element_type=F32) + jnp.dot(xh, wl_ref[...], preferred_element_type=F32)
              + jnp.dot(xl, wh, preferred_element_type=F32)) + b_ref[...]
    lane = lax.broadcasted_iota(jnp.int32, logits.shape, 1)
    lane_f = lane.astype(F32)
    neg = -jnp.inf
    far = float(LANES)
    is_g = (lane >= ROUTER_GROUP_LANE0) & (lane < ROUTER_GROUP_LANE0 + N_GROUPS)
    gl = jnp.where(is_g, logits, neg)
    gmax = jnp.max(gl, axis=-1, keepdims=True)
    g_sel = jnp.min(jnp.where(gl == gmax, lane_f, far), axis=-1, keepdims=True) - float(ROUTER_GROUP_LANE0)
    grp_w = 1.0 / jnp.sum(jnp.where(is_g, jnp.exp(logits - gmax), 0.0), axis=-1, keepdims=True)
    in_g = (lane < N_EXPERTS) & ((lane // EXPERTS_PER_GROUP).astype(F32) == g_sel)
    el = jnp.where(in_g, logits, neg)
    t1 = jnp.max(el, axis=-1, keepdims=True)
    i1 = jnp.min(jnp.where(el == t1, lane_f, far), axis=-1, keepdims=True)
    el2 = jnp.where(lane_f == i1, neg, el)
    t2 = jnp.max(el2, axis=-1, keepdims=True)
    i2 = jnp.min(jnp.where(el2 == t2, lane_f, far), axis=-1, keepdims=True)
    e2 = jnp.exp(t2 - t1)
    w1 = grp_w / (1.0 + e2)
    w2 = w1 * e2
    first_lo = i1 < i2
    lo = jnp.where(first_lo, i1, i2) - g_sel * EXPERTS_PER_GROUP
    hi = jnp.where(first_lo, i2, i1) - g_sel * EXPERTS_PER_GROUP
    pair = lo * (EXPERTS_PER_GROUP - 1.0) - lo * (lo - 1.0) * 0.5 + (hi - lo - 1.0)
    onehot = jnp.where(lane_f == g_sel * MOE_PAIRS + pair, 1.0, 0.0)
    meta = (jnp.where(lane == 0, jnp.where(first_lo, w1, w2), 0.0)
            + jnp.where(lane == 1, jnp.where(first_lo, w2, w1), 0.0))
    return meta, onehot


def _router_weights(w_group, b_group, w_expert, b_expert):
    d = w_group.shape[0]
    w = jnp.zeros((d, LANES), F32).at[:, :N_EXPERTS].set(w_expert)
    w = w.at[:, ROUTER_GROUP_LANE0:ROUTER_GROUP_LANE0 + N_GROUPS].set(w_group)
    bias = jnp.zeros((1, LANES), F32).at[0, :N_EXPERTS].set(b_expert)
    bias = bias.at[0, ROUTER_GROUP_LANE0:ROUTER_GROUP_LANE0 + N_GROUPS].set(b_group)
    wh = w.astype(BF16)
    return wh, (w - wh.astype(F32)).astype(BF16), bias


def _route_specs(n, d, tm, row, fixed):
    assert tm % MOE_TOK_TILE == 0 and d == TOKEN_TILE_ROWS * LANES
    in_specs = [pl.BlockSpec((d, LANES), fixed), pl.BlockSpec((d, LANES), fixed), pl.BlockSpec((1, LANES), fixed)]
    out_specs = [pl.BlockSpec((tm * TOKEN_TILE_ROWS, LANES), row), pl.BlockSpec((tm, LANES), row),
                 pl.BlockSpec((tm, LANES), row), pl.BlockSpec((8, LANES), fixed)]
    out_shape = [jax.ShapeDtypeStruct((n * TOKEN_TILE_ROWS, LANES), F32), jax.ShapeDtypeStruct((n, LANES), F32),
                 jax.ShapeDtypeStruct((n, LANES), BF16), jax.ShapeDtypeStruct((8, LANES), F32)]
    return in_specs, out_specs, out_shape


def _moe_pos_kernel(oh_ref, base_ref, lt_ref, pos_ref, carry_ref):
    @pl.when(pl.program_id(0) == 0)
    def _():
        carry_ref[...] = jnp.zeros_like(carry_ref)

    oh = oh_ref[...]
    rank = jnp.dot(lt_ref[...], oh, preferred_element_type=F32) + carry_ref[...] + base_ref[...]
    ohf = oh.astype(F32)
    val = ohf * rank
    carry_ref[...] += jnp.sum(ohf, axis=0, keepdims=True)
    hi = jnp.floor(val * (1.0 / 256.0))
    lo = val - 256.0 * hi
    ones = jnp.ones((8, LANES), BF16)
    pos = (256.0 * lax.dot_general(ones, hi.astype(BF16), _NT, preferred_element_type=F32)
           + lax.dot_general(ones, lo.astype(BF16), _NT, preferred_element_type=F32))
    pos_ref[0] = pos[0:1].astype(jnp.int32)


def _moe_pos(onehot, base, tm=MOE_TOK_TILE):
    n = onehot.shape[0]
    lt = jnp.asarray(np.tril(np.ones((tm, tm), np.float32), -1), BF16)
    return pl.pallas_call(
        _moe_pos_kernel,
        grid=(n // tm,),
        in_specs=[pl.BlockSpec((tm, LANES), lambda i: (i, 0)), pl.BlockSpec((1, LANES), lambda i: (0, 0)),
                  pl.BlockSpec((tm, tm), lambda i: (0, 0))],
        out_specs=pl.BlockSpec((1, 1, tm), lambda i: (i, 0, 0)),
        out_shape=jax.ShapeDtypeStruct((n // tm, 1, tm), jnp.int32),
        scratch_shapes=[pltpu.VMEM((1, LANES), F32)],
        compiler_params=_params(("arbitrary",), 32),
        name="moe_pos",
    )(onehot, base, lt)


def _tile_copy(src, src_tok, dst, dst_tok, sem):
    rows = TOKEN_TILE_ROWS
    return pltpu.make_async_copy(src.at[pl.ds(pl.multiple_of(src_tok * rows, rows), rows)],
                                 dst.at[pl.ds(pl.multiple_of(dst_tok * rows, rows), rows)], sem)


def _meta_copy(src, src_tok, dst, dst_tok, sem):
    return pltpu.make_async_copy(src.at[pl.ds(src_tok, 1)], dst.at[pl.ds(dst_tok, 1)], sem)


def _moe_scatter_kernel(pos_ref, xt_ref, meta_ref, xs_in_ref, xm_in_ref, xs_ref, xm_ref, sem, *, tm):
    del xs_in_ref, xm_in_ref

    def start(r2, c):
        for par in range(2):
            r = 2 * r2 + par
            p = pos_ref[0, 0, r]
            _tile_copy(xt_ref, r, xs_ref, p, sem.at[0]).start(priority=par)
            _meta_copy(meta_ref, r, xm_ref, p, sem.at[1]).start(priority=1 - par)
        return c

    lax.fori_loop(0, tm // 2, start, 0, unroll=4)
    pltpu.make_async_copy(xt_ref, xs_ref.at[pl.ds(0, tm * TOKEN_TILE_ROWS)], sem.at[0]).wait()
    pltpu.make_async_copy(meta_ref, xm_ref.at[pl.ds(0, tm)], sem.at[1]).wait()


def _moe_scatter(pos, xt, meta, xs_buf, xm_buf, tm=2 * MOE_TOK_TILE):
    n = meta.shape[0]
    pos = pos.reshape(n // tm, 1, tm)
    return pl.pallas_call(
        functools.partial(_moe_scatter_kernel, tm=tm),
        grid=(n // tm,),
        in_specs=[pl.BlockSpec((1, 1, tm), lambda i: (i, 0, 0), memory_space=pltpu.SMEM),
                  pl.BlockSpec((tm * TOKEN_TILE_ROWS, LANES), lambda i: (i, 0)),
                  pl.BlockSpec((tm, LANES), lambda i: (i, 0)),
                  pl.BlockSpec(memory_space=pl.ANY), pl.BlockSpec(memory_space=pl.ANY)],
        out_specs=[pl.BlockSpec(memory_space=pl.ANY), pl.BlockSpec(memory_space=pl.ANY)],
        out_shape=[jax.ShapeDtypeStruct(xs_buf.shape, xs_buf.dtype), jax.ShapeDtypeStruct(xm_buf.shape, xm_buf.dtype)],
        scratch_shapes=[pltpu.SemaphoreType.DMA((2,))],
        input_output_aliases={3: 0, 4: 1},
        compiler_params=_params(("arbitrary",), 32),
        name="moe_scatter",
    )(pos, xt, meta, xs_buf, xm_buf)


def _moe_experts_kernel(grp_ref, e1_ref, e2_ref, nused_ref, xs_ref, xm_ref, wg_ref, wu_ref, wd_ref, ys_in_ref, ys_ref):
    del grp_ref, ys_in_ref
    i = pl.program_id(0)
    tr = xm_ref.shape[0] // MOE_TILES_PER_STEP

    @pl.when(i * MOE_TILES_PER_STEP < nused_ref[0])
    def _():
        for t in range(MOE_TILES_PER_STEP):
            x = _load_token_tiles(xs_ref, tr, t * tr * TOKEN_TILE_ROWS).astype(BF16)
            meta = xm_ref[t * tr:(t + 1) * tr, :]
            y = None
            for k, e_ref in enumerate((e1_ref, e2_ref)):
                e = e_ref[i * MOE_TILES_PER_STEP + t]
                hg = jnp.dot(x, wg_ref[0, e], preferred_element_type=F32)
                hu = jnp.dot(x, wu_ref[0, e], preferred_element_type=F32)
                h = hg * _sigmoid(hg) * hu * meta[:, k:k + 1]
                yk = jnp.dot(h.astype(BF16), wd_ref[0, e], preferred_element_type=F32)
                y = yk if y is None else y + yk
            _store_token_tiles(ys_ref, y, t * tr * TOKEN_TILE_ROWS)


def _moe_experts(xs_buf, xm_buf, ys_buf, tile_grp, tile_e1, tile_e2, n_used, w_gate, w_up, w_down, layer,
                 tr=MOE_ROW_TILE):
    r_max = xm_buf.shape[0]
    d, f = w_gate.shape[-2:]
    assert d == TOKEN_TILE_ROWS * LANES
    tps = MOE_TILES_PER_STEP
    step_rows = tps * tr
    tile_rows = step_rows * TOKEN_TILE_ROWS
    epg = EXPERTS_PER_GROUP
    used = lambda i, grp, e1, e2, nu: (jnp.minimum(i, nu[0] // tps - 1), 0)
    wmap = lambda i, grp, e1, e2, nu: (layer * N_GROUPS + grp[i * tps], 0, 0, 0)
    grid_spec = pltpu.PrefetchScalarGridSpec(
        num_scalar_prefetch=4,
        grid=(r_max // step_rows,),
        in_specs=[pl.BlockSpec((tile_rows, LANES), used), pl.BlockSpec((step_rows, LANES), used),
                  pl.BlockSpec((1, epg, d, f), wmap), pl.BlockSpec((1, epg, d, f), wmap),
                  pl.BlockSpec((1, epg, f, d), wmap), pl.BlockSpec(memory_space=pl.ANY)],
        out_specs=pl.BlockSpec((tile_rows, LANES), used),
    )
    return pl.pallas_call(
        _moe_experts_kernel,
        grid_spec=grid_spec,
        out_shape=jax.ShapeDtypeStruct(ys_buf.shape, ys_buf.dtype),
        input_output_aliases={9: 0},
        compiler_params=_params(("arbitrary",), 48),
        name="moe_experts",
    )(tile_grp, tile_e1, tile_e2, n_used, xs_buf, xm_buf, w_gate, w_up, w_down, ys_buf)


def _moe_combine_ln_kernel(pos_ref, nxt_ref, x_ref, ys_ref, g_ref, b_ref, y_ref, buf_ref, sem, *, alpha, tm):
    i = pl.program_id(0)
    slot = i % 2

    def gather(p_ref, s):
        def start(r2, c):
            for par in range(2):
                r = 2 * r2 + par
                _tile_copy(ys_ref, p_ref[0, 0, r], buf_ref, s * tm + r, sem.at[s]).start(priority=par)
            return c
        lax.fori_loop(0, tm // 2, start, 0, unroll=4)

    @pl.when(i == 0)
    def _():
        gather(pos_ref, 0)

    @pl.when(i + 1 < pl.num_programs(0))
    def _():
        gather(nxt_ref, 1 - slot)

    base = pl.multiple_of(slot * (tm * TOKEN_TILE_ROWS), tm * TOKEN_TILE_ROWS)
    slot_rows = pl.ds(base, tm * TOKEN_TILE_ROWS)
    pltpu.make_async_copy(ys_ref.at[pl.ds(0, tm * TOKEN_TILE_ROWS)], buf_ref.at[slot_rows], sem.at[slot]).wait()
    ffn = _load_token_tiles(buf_ref, tm, base)
    y_ref[...] = _layer_norm(alpha * x_ref[...] + ffn, g_ref[...], b_ref[...])


def _moe_combine_ln(pos, x, ys_buf, g, b, alpha, tm=MOE_TOK_TILE):
    n, d = x.shape
    nt = n // tm
    row = lambda i: (i, 0)
    fixed = lambda i: (0, 0)
    return pl.pallas_call(
        functools.partial(_moe_combine_ln_kernel, alpha=alpha, tm=tm),
        grid=(nt,),
        in_specs=[pl.BlockSpec((1, 1, tm), lambda i: (i, 0, 0), memory_space=pltpu.SMEM),
                  pl.BlockSpec((1, 1, tm), lambda i: (jnp.minimum(i + 1, nt - 1), 0, 0), memory_space=pltpu.SMEM),
                  pl.BlockSpec((tm, d), row), pl.BlockSpec(memory_space=pl.ANY),
                  pl.BlockSpec((1, d), fixed), pl.BlockSpec((1, d), fixed)],
        out_specs=pl.BlockSpec((tm, d), row),
        out_shape=jax.ShapeDtypeStruct((n, d), F32),
        scratch_shapes=[pltpu.VMEM((2 * tm * TOKEN_TILE_ROWS, LANES), F32), pltpu.SemaphoreType.DMA((2,))],
        compiler_params=_params(("arbitrary",), 32),
        name="moe_combine_ln",
    )(pos, pos, x, ys_buf, g, b)


def _bucket_experts():
    lo, hi = [], []
    for a in range(EXPERTS_PER_GROUP):
        for c in range(a + 1, EXPERTS_PER_GROUP):
            lo.append(a)
            hi.append(c)
    grp = np.repeat(np.arange(N_GROUPS), MOE_PAIRS).astype(np.int32)
    return grp, np.tile(lo, N_GROUPS).astype(np.int32), np.tile(hi, N_GROUPS).astype(np.int32)


def _moe_layer(x, routed, bufs, w_gate, w_up, w_down, layer, g, b, alpha):
    tr = MOE_ROW_TILE
    xs_buf, xm_buf, ys_buf = bufs
    xt, meta, onehot, cnt = routed
    counts = cnt[0, :MOE_BUCKETS].astype(jnp.int32)
    padded = ((counts + (tr - 1)) // tr * tr).reshape(N_GROUPS, MOE_PAIRS)
    step_rows = MOE_TILES_PER_STEP * tr
    slack = -jnp.sum(padded, axis=1) % step_rows
    padded = padded.at[:, MOE_PAIRS - 1].add(slack).reshape(MOE_BUCKETS)
    ends = jnp.cumsum(padded)
    base = jnp.zeros((1, LANES), F32).at[0, :MOE_BUCKETS].set((ends - padded).astype(F32))
    n_used = ends[-1] // tr
    n_tiles = xm_buf.shape[0] // tr
    tile = jnp.minimum(jnp.arange(n_tiles, dtype=jnp.int32), n_used - 1)
    tile_bkt = jnp.sum((ends[None, :] <= (tile * tr)[:, None]).astype(jnp.int32), axis=1)
    tile_bkt = jnp.minimum(tile_bkt, MOE_BUCKETS - 1)
    b_grp, b_lo, b_hi = (jnp.asarray(t)[tile_bkt] for t in _bucket_experts())
    pos = _moe_pos(onehot, base)
    xs_buf, xm_buf = _moe_scatter(pos, xt, meta, xs_buf, xm_buf)
    ys_buf = _moe_experts(xs_buf, xm_buf, ys_buf, b_grp, b_lo, b_hi, n_used.reshape(1).astype(jnp.int32),
                          w_gate, w_up, w_down, layer)
    return _moe_combine_ln(pos, x, ys_buf, g, b, alpha), (xs_buf, xm_buf, ys_buf)


def kernel(x, ln_mix_g, ln_mix_b, ln_ffn_g, ln_ffn_b, fox_w_in, fox_b_f, fox_w_out, hgrn_w_in, hgrn_lb_logits, hgrn_norm_g, hgrn_w_out, conv_w_in, conv_w, conv_w_out, moe_w_group, moe_b_group, moe_w_expert, moe_b_expert, moe_w_gate, moe_w_up, moe_w_down):
    bsz, t_len, d = x.shape
    depth = ln_mix_g.shape[0]
    alpha = float((2 * depth) ** 0.25)
    assert d == FOX_HEADS * FOX_HEAD_DIM == HGRN_HEADS * HGRN_DIM
    assert t_len % 512 == 0 and (bsz * t_len) % 1024 == 0

    lb_prob = jax.nn.softmax(hgrn_lb_logits.astype(F32), axis=0)
    lower_bounds = jnp.cumsum(lb_prob, axis=0) - lb_prob[0]

    h = x.reshape(bsz * t_len, d)
    r_max = bsz * t_len + MOE_BUCKETS * MOE_ROW_TILE + N_GROUPS * MOE_TILES_PER_STEP * MOE_ROW_TILE
    assert r_max % (MOE_TILES_PER_STEP * MOE_ROW_TILE) == 0
    by_group = lambda w: w.astype(BF16).reshape((depth * N_GROUPS, EXPERTS_PER_GROUP) + w.shape[2:])
    moe_w_gate, moe_w_up, moe_w_down = by_group(moe_w_gate), by_group(moe_w_up), by_group(moe_w_down)
    bufs = (jnp.zeros((r_max * TOKEN_TILE_ROWS, LANES), F32), jnp.zeros((r_max, LANES), F32),
            jnp.zeros((r_max * TOKEN_TILE_ROWS, LANES), F32))
    for layer in range(depth):
        kind, j = layer % 3, layer // 3
        g_mix, b_mix = ln_mix_g[layer].reshape(1, d), ln_mix_b[layer].reshape(1, d)
        router_w = _router_weights(moe_w_group[layer], moe_b_group[layer], moe_w_expert[layer], moe_b_expert[layer])
        if kind == 0:
            h, routed = _fox_layer(h, fox_w_in[j], fox_b_f[j], fox_w_out[j], g_mix, b_mix, router_w,
                                   alpha, bsz, t_len)
        elif kind == 1:
            h, routed = _hgrn_layer(h, hgrn_w_in[j], lower_bounds[layer], hgrn_norm_g[j], hgrn_w_out[j],
                                    g_mix, b_mix, router_w, alpha, bsz, t_len)
        else:
            h, routed = _conv_layer(h, conv_w_in[j], conv_w[j], conv_w_out[j], g_mix, b_mix, router_w,
                                    alpha, bsz, t_len)
        h, bufs = _moe_layer(h, routed, bufs, moe_w_gate, moe_w_up, moe_w_down, layer,
                             ln_ffn_g[layer].reshape(1, d), ln_ffn_b[layer].reshape(1, d), alpha)
    return h.reshape(bsz, t_len, d)
```

```python
import functools

import numpy as np
import jax
import jax.numpy as jnp
from jax import lax
from jax.experimental import pallas as pl
from jax.experimental.pallas import tpu as pltpu

F32 = jnp.float32
BF16 = jnp.bfloat16

FOX_HEADS = 16
FOX_HEAD_DIM = 64
FOX_PAIRS_PER_STEP = 4
HGRN_HEADS = 8
HGRN_DIM = 128
HGRN_CHUNK = 128
HGRN_SEQS_PER_STEP = 4
HGRN_SMALL_LEVEL = 8
N_GROUPS = 4
EXPERTS_PER_GROUP = 8
N_EXPERTS = N_GROUPS * EXPERTS_PER_GROUP
LN_EPS = 1e-5
RMS_EPS = 1e-6
LOG2E = 1.4426950408889634
LANES = 128
ROUTER_GROUP_LANE0 = N_EXPERTS
MOE_PAIRS = EXPERTS_PER_GROUP * (EXPERTS_PER_GROUP - 1) // 2
MOE_BUCKETS = N_GROUPS * MOE_PAIRS
MOE_ROW_TILE = 128
MOE_TILES_PER_STEP = 4
MOE_TOK_TILE = 256
TOKEN_TILE_ROWS = 8

_NT = (((1,), (1,)), ((), ()))
_TN = (((0,), (0,)), ((), ()))


def _params(semantics, vmem_mb):
    return pltpu.CompilerParams(dimension_semantics=semantics,
                                vmem_limit_bytes=vmem_mb * 1024 * 1024)


def _layer_norm(v, g, b):
    mu = jnp.mean(v, axis=-1, keepdims=True)
    d = v - mu
    var = jnp.mean(d * d, axis=-1, keepdims=True)
    return d * lax.rsqrt(var + LN_EPS) * g + b


def _sigmoid(v):
    return 1.0 / (1.0 + jnp.exp(-v))


def _proj_kernel(x_ref, *refs, n_out):
    xb = x_ref[...].astype(BF16)
    for w_ref, o_ref in zip(refs[:n_out], refs[n_out:]):
        o_ref[...] = jnp.dot(xb, w_ref[...], preferred_element_type=F32).astype(o_ref.dtype)


def _proj(x, ws, dtypes, name, tm=512):
    n, k = x.shape
    return pl.pallas_call(
        functools.partial(_proj_kernel, n_out=len(ws)),
        grid=(n // tm,),
        in_specs=[pl.BlockSpec((tm, k), lambda i: (i, 0))]
        + [pl.BlockSpec(w.shape, lambda i: (0, 0)) for w in ws],
        out_specs=[pl.BlockSpec((tm, w.shape[1]), lambda i: (i, 0)) for w in ws],
        out_shape=[jax.ShapeDtypeStruct((n, w.shape[1]), dt) for w, dt in zip(ws, dtypes)],
        compiler_params=_params(("arbitrary",), 48),
        name=name,
    )(x, *ws)


def _outproj_ln_kernel(o_ref, w_ref, x_ref, g_ref, b_ref, *refs, alpha, transposed):
    y_ref = refs[3]
    dims = _TN if transposed else (((1,), (0,)), ((), ()))
    mixed = lax.dot_general(o_ref[...], w_ref[...], dims, preferred_element_type=F32)
    y = _layer_norm(alpha * x_ref[...] + mixed, g_ref[...], b_ref[...])
    y_ref[...] = y
    _route_rows(y, pl.program_id(0) == 0, *refs[:3], *refs[4:])


def _outproj_ln(o, w, x, g, b, router_w, alpha, name, tm=512, transposed=False):
    n, d = x.shape
    k = w.shape[0]
    row = lambda i: (i, 0)
    fixed = lambda i: (0, 0)
    o_spec = pl.BlockSpec((k, tm), lambda i: (0, i)) if transposed else pl.BlockSpec((tm, k), row)
    r_in, r_out, r_shape = _route_specs(n, d, tm, row, fixed)
    y, *routed = pl.pallas_call(
        functools.partial(_outproj_ln_kernel, alpha=alpha, transposed=transposed),
        grid=(n // tm,),
        in_specs=[o_spec, pl.BlockSpec((k, d), fixed),
                  pl.BlockSpec((tm, d), row), pl.BlockSpec((1, d), fixed), pl.BlockSpec((1, d), fixed)] + r_in,
        out_specs=[pl.BlockSpec((tm, d), row)] + r_out,
        out_shape=[jax.ShapeDtypeStruct((n, d), F32)] + r_shape,
        compiler_params=_params(("arbitrary",), 48),
        name=name,
    )(o, w, x, g, b, *router_w)
    return y, routed


def _split3(v):
    hi = v.astype(BF16)
    r1 = v - hi.astype(F32)
    mid = r1.astype(BF16)
    return hi, mid, (r1 - mid.astype(F32)).astype(BF16)


def _fox_proj_kernel(x_ref, wqt_ref, wk_ref, wvt_ref, qt_ref, k_ref, vt_ref):
    xb = x_ref[...].astype(BF16)
    qt_ref[...] = lax.dot_general(wqt_ref[...], xb, _NT, preferred_element_type=F32).astype(BF16)
    k_ref[...] = jnp.dot(xb, wk_ref[...], preferred_element_type=F32).astype(BF16)
    vt_ref[...] = lax.dot_general(wvt_ref[...], xb, _NT, preferred_element_type=F32).astype(BF16)


def _fox_proj(x, wqt, wk, wvt, tm=512):
    n, d = x.shape
    fixed = lambda i: (0, 0)
    return pl.pallas_call(
        _fox_proj_kernel,
        grid=(n // tm,),
        in_specs=[pl.BlockSpec((tm, d), lambda i: (i, 0)), pl.BlockSpec((d, d), fixed),
                  pl.BlockSpec((d, d), fixed), pl.BlockSpec((d, d), fixed)],
        out_specs=[pl.BlockSpec((d, tm), lambda i: (0, i)), pl.BlockSpec((tm, d), lambda i: (i, 0)),
                   pl.BlockSpec((d, tm), lambda i: (0, i))],
        out_shape=[jax.ShapeDtypeStruct((d, n), BF16), jax.ShapeDtypeStruct((n, d), BF16),
                   jax.ShapeDtypeStruct((d, n), BF16)],
        compiler_params=_params(("arbitrary",), 48),
        name="fox_proj",
    )(x, wqt, wk, wvt)


def _fox_gate_kernel(x_ref, wf_ref, bf_ref, tri_ref, sel_ref, cb_ref, carry_ref):
    @pl.when(pl.program_id(1) == 0)
    def _():
        carry_ref[...] = jnp.zeros_like(carry_ref)

    z = jnp.dot(x_ref[...].astype(BF16), wf_ref[...], preferred_element_type=F32) + bf_ref[...]
    logf = jnp.minimum(z, 0.0) - jnp.log(1.0 + jnp.exp(-jnp.abs(z)))
    c3 = jnp.dot(tri_ref[...], jnp.concatenate(_split3(logf), axis=1), preferred_element_type=F32)
    c = carry_ref[...] + c3[:, :LANES] + c3[:, LANES:2 * LANES] + c3[:, 2 * LANES:]
    carry_ref[...] = c[c.shape[0] - 1:, :]
    neg3 = jnp.concatenate(_split3(c * (-LOG2E)), axis=1)
    cb_ref[...] = jnp.dot(neg3, sel_ref[...], preferred_element_type=F32).astype(BF16)


def _fox_gate(x, wf, bf, bsz, t_len, tg=512):
    n, d = x.shape
    nt = t_len // tg
    tri = jnp.asarray(np.tril(np.ones((tg, tg), np.float32)), BF16)
    sel = np.zeros((3 * LANES, d), np.float32)
    for h in range(FOX_HEADS):
        for j in range(3):
            sel[j * LANES + h, (h // 2) * LANES + 3 * (h % 2) + j] = 1.0
    return pl.pallas_call(
        _fox_gate_kernel,
        grid=(bsz, nt),
        in_specs=[pl.BlockSpec((tg, d), lambda b, i: (b * nt + i, 0)),
                  pl.BlockSpec((d, LANES), lambda b, i: (0, 0)),
                  pl.BlockSpec((1, LANES), lambda b, i: (0, 0)),
                  pl.BlockSpec((tg, tg), lambda b, i: (0, 0)),
                  pl.BlockSpec((3 * LANES, d), lambda b, i: (0, 0))],
        out_specs=pl.BlockSpec((tg, d), lambda b, i: (b * nt + i, 0)),
        out_shape=jax.ShapeDtypeStruct((n, d), BF16),
        scratch_shapes=[pltpu.VMEM((1, LANES), F32)],
        compiler_params=_params(("arbitrary", "arbitrary"), 32),
        name="fox_gate",
    )(x, wf, bf, tri, jnp.asarray(sel, BF16))


def _fox_attn_kernel(qt_ref, k_ref, cb_ref, vt_ref, ot_ref, s_ref, *, tq):
    qi = pl.program_id(2)
    n_heads = 2 * FOX_PAIRS_PER_STEP
    feat = lax.broadcasted_iota(jnp.int32, (LANES, tq), 0)
    rhs = []
    for h in range(n_heads):
        pair, hh = divmod(h, 2)
        qt = qt_ref[pair * LANES:(pair + 1) * LANES, :].astype(F32)
        own = (feat >= hh * FOX_HEAD_DIM) & (feat < (hh + 1) * FOX_HEAD_DIM)
        bias_rows = (feat >= 3 * hh) & (feat < 3 * hh + 3)
        rhs.append(jnp.concatenate([jnp.where(own, qt, 0.0).astype(BF16),
                                    jnp.where(bias_rows, 1.0, 0.0).astype(BF16)], axis=0))
    key_i = lax.broadcasted_iota(jnp.int32, (tq, tq), 0)
    qry_i = lax.broadcasted_iota(jnp.int32, (tq, tq), 1)
    ones_rows = jnp.ones((16, tq), BF16)

    def scores(j, slot):
        start = pl.multiple_of(j * tq, tq)
        for pair in range(FOX_PAIRS_PER_STEP):
            lanes = slice(pair * LANES, (pair + 1) * LANES)
            kext = jnp.concatenate([k_ref[pl.ds(start, tq), lanes], cb_ref[pl.ds(start, tq), lanes]], axis=1)
            for h in (2 * pair, 2 * pair + 1):
                s_ref[slot, h] = jnp.dot(kext, rhs[h], preferred_element_type=F32)

    def block(j, carry, slot, masked):
        start = pl.multiple_of(j * tq, tq)
        if not masked:
            scores(j + 1, 1 - slot)
        out = []
        for h in range(n_heads):
            m, l, acc = carry[h]
            st = s_ref[slot, h]
            if masked:
                st = jnp.where(key_i <= qry_i, st, -jnp.inf)
            m_new = jnp.maximum(m, jnp.max(st, axis=0, keepdims=True))
            p = jnp.exp2(st - m_new)
            a = jnp.exp2(m - m_new)
            vt = vt_ref[h * FOX_HEAD_DIM:(h + 1) * FOX_HEAD_DIM, pl.ds(start, tq)]
            pv = jnp.dot(jnp.concatenate([vt, ones_rows], axis=0), p.astype(BF16), preferred_element_type=F32)
            out.append((m_new, a * l + pv[FOX_HEAD_DIM:FOX_HEAD_DIM + 1], acc * a + pv[:FOX_HEAD_DIM]))
        return tuple(out)

    init = (jnp.full((1, tq), -jnp.inf, F32), jnp.zeros((1, tq), F32), jnp.zeros((FOX_HEAD_DIM, tq), F32))
    scores(0, 0)
    carry = lax.fori_loop(
        0, qi // 2, lambda i, c: block(2 * i + 1, block(2 * i, c, 0, False), 1, False), (init,) * n_heads)
    carry = lax.cond(
        qi % 2 == 0,
        lambda c: block(qi, c, 0, True),
        lambda c: block(qi, block(qi - 1, c, 0, False), 1, True),
        carry)
    for h in range(n_heads):
        _, l, acc = carry[h]
        ot_ref[h * FOX_HEAD_DIM:(h + 1) * FOX_HEAD_DIM, :] = (acc / l).astype(ot_ref.dtype)


def _fox_attn(qt, k, cb, vt, bsz, t_len, tq=256):
    d, n = qt.shape
    w = FOX_PAIRS_PER_STEP * LANES
    nq = t_len // tq
    return pl.pallas_call(
        functools.partial(_fox_attn_kernel, tq=tq),
        grid=(bsz, d // w, nq),
        in_specs=[pl.BlockSpec((w, tq), lambda b, p, i: (p, b * nq + i)),
                  pl.BlockSpec((t_len, w), lambda b, p, i: (b, p)),
                  pl.BlockSpec((t_len, w), lambda b, p, i: (b, p)),
                  pl.BlockSpec((w, t_len), lambda b, p, i: (p, b))],
        out_specs=pl.BlockSpec((w, tq), lambda b, p, i: (p, b * nq + i)),
        out_shape=jax.ShapeDtypeStruct((d, n), BF16),
        scratch_shapes=[pltpu.VMEM((2, 2 * FOX_PAIRS_PER_STEP, tq, tq), F32)],
        compiler_params=_params(("arbitrary", "arbitrary", "arbitrary"), 32),
        name="fox_attn",
    )(qt, k, cb, vt)


def _fox_layer(x, w_in, b_f, w_out, g, b, router_w, alpha, bsz, t_len):
    d = x.shape[1]
    wqt = (w_in[:, :d] * (FOX_HEAD_DIM ** -0.5 * LOG2E)).T.astype(BF16)
    wk = w_in[:, d:2 * d].astype(BF16)
    wvt = w_in[:, 2 * d:3 * d].T.astype(BF16)
    wf = jnp.zeros((d, LANES), F32).at[:, :FOX_HEADS].set(w_in[:, 3 * d:]).astype(BF16)
    bf = jnp.zeros((1, LANES), F32).at[0, :FOX_HEADS].set(b_f)
    qt, k, vt = _fox_proj(x, wqt, wk, wvt)
    cb = _fox_gate(x, wf, bf, bsz, t_len)
    ot = _fox_attn(qt, k, cb, vt, bsz, t_len)
    return _outproj_ln(ot, w_out.astype(BF16), x, g, b, router_w, alpha, "fox_out_ln", transposed=True)


def _hgrn_constants():
    c = HGRN_CHUNK
    r = np.arange(c)[:, None]
    j = np.arange(c)[None, :]
    blocks = [j <= r]
    masks = []
    levels = []
    half = c // 2
    while half >= 1:
        ref = (r // (2 * half)) * (2 * half) + half - 1
        upper = (r % (2 * half)) >= half
        if half < HGRN_SMALL_LEVEL:
            blocks.append(np.where(upper, (j > ref) & (j <= r), (j > r) & (j <= ref)))
        masks.append(((r // (2 * half)) == (j // (2 * half))) & upper & ((j % (2 * half)) < half))
        levels.append(half)
        half //= 2
    masks.append(r == j)
    wall = np.concatenate(blocks, axis=0).astype(np.float32)
    wall2 = np.concatenate([wall, wall], axis=1)
    return wall2, np.stack(masks).astype(np.float32), tuple(levels)


def _hgrn_kernel(q_ref, fl_ref, i_ref, g_ref, lb_ref, ng_ref, wall_ref, mask_ref, o_ref, s_ref, *, levels):
    c = HGRN_CHUNK

    @pl.when(pl.program_id(1) == 0)
    def _():
        s_ref[...] = jnp.zeros_like(s_ref)

    wall = wall_ref[...]
    rowi = lax.broadcasted_iota(jnp.int32, (c, HGRN_DIM), 0)
    e_pair = None
    for sh in range(q_ref.shape[1] * HGRN_HEADS):
        seq, h = divmod(sh, HGRN_HEADS)
        q_v, fl_v, i_v, g_v, o_v = (r.at[0, seq] for r in (q_ref, fl_ref, i_ref, g_ref, o_ref))
        sl = slice(h * HGRN_DIM, (h + 1) * HGRN_DIM)
        if h % 2 == 0:
            sl2 = slice(h * HGRN_DIM, (h + 2) * HGRN_DIM)
            lb2 = lb_ref[:, sl2]
            f2 = lb2 + (1.0 - lb2) * _sigmoid(fl_v[:, sl2])
            logf2 = jnp.log(f2)
            hi = logf2.astype(BF16)
            mid = (logf2 - hi.astype(F32)).astype(BF16)
            e_pair = jnp.dot(wall, jnp.concatenate([hi, mid], axis=0), preferred_element_type=F32)
        lane0 = (h % 2) * HGRN_DIM
        q = q_v[:, sl].astype(F32)
        i_b = i_v[:, sl]
        k = 1.0 - f2[:, lane0:lane0 + HGRN_DIM]
        e_all = e_pair[:, lane0:lane0 + HGRN_DIM]
        b = e_all[0:c]
        x_pre = jnp.exp(b)
        x_suf = jnp.exp(b[c - 1:c, :] - b)
        st = s_ref[sh]
        o = lax.dot_general((q * x_pre).astype(BF16), st.astype(BF16), _NT, preferred_element_type=F32)
        upd = lax.dot_general(i_b, (k * x_suf).astype(BF16), _TN, preferred_element_type=F32)
        s_ref[sh] = st * x_pre[c - 1:c, :] + upd
        a = jnp.zeros((c, c), F32)
        n_big = sum(half >= HGRN_SMALL_LEVEL for half in levels)
        for l, half in enumerate(levels):
            upper = (rowi & half) != 0
            if half >= HGRN_SMALL_LEVEL:
                parts = []
                for r0 in range(0, c, 2 * half):
                    ref = b[r0 + half - 1:r0 + half, :]
                    parts += [ref - b[r0:r0 + half], b[r0 + half:r0 + 2 * half] - ref]
                e_lvl = jnp.concatenate(parts, axis=0)
            else:
                e_lvl = e_all[(1 + l - n_big) * c:(2 + l - n_big) * c]
            z = (jnp.where(upper, q, k) * jnp.exp(e_lvl)).astype(BF16)
            a = a + lax.dot_general(z, z, _NT, preferred_element_type=F32) * mask_ref[l]
        a = a + lax.dot_general(q.astype(BF16), k.astype(BF16), _NT,
                                preferred_element_type=F32) * mask_ref[len(levels)]
        o = o + jnp.dot(a.astype(BF16), i_b, preferred_element_type=F32)
        o = o * lax.rsqrt(jnp.mean(o * o, axis=-1, keepdims=True) + RMS_EPS) * ng_ref[...]
        gate = g_v[:, sl].astype(F32)
        o_v[:, sl] = (o * (gate * _sigmoid(gate))).astype(o_ref.dtype)


def _hgrn_core(q, fl, i, g, lb, ng, bsz, t_len):
    n, d = q.shape
    c = HGRN_CHUNK
    nc = t_len // c
    sps = HGRN_SEQS_PER_STEP
    wall2, masks, levels = _hgrn_constants()
    by_seq = lambda a: a.reshape(bsz // sps, sps, t_len, d)
    row = pl.BlockSpec((1, sps, c, d), lambda b, j: (b, 0, j, 0))
    fixed2 = lambda b, j: (0, 0)
    out = pl.pallas_call(
        functools.partial(_hgrn_kernel, levels=levels),
        grid=(bsz // sps, nc),
        in_specs=[row, row, row, row, pl.BlockSpec((1, d), fixed2), pl.BlockSpec((1, HGRN_DIM), fixed2),
                  pl.BlockSpec(wall2.shape, fixed2), pl.BlockSpec(masks.shape, lambda b, j: (0, 0, 0))],
        out_specs=row,
        out_shape=jax.ShapeDtypeStruct((bsz // sps, sps, t_len, d), BF16),
        scratch_shapes=[pltpu.VMEM((sps * HGRN_HEADS, HGRN_DIM, HGRN_DIM), F32)],
        compiler_params=_params(("arbitrary", "arbitrary"), 32),
        name="hgrn_core",
    )(by_seq(q), by_seq(fl), by_seq(i), by_seq(g), lb, ng, jnp.asarray(wall2, BF16), jnp.asarray(masks, F32))
    return out.reshape(n, d)


def _hgrn_layer(x, w_in, lower_bound, norm_g, w_out, g, b, router_w, alpha, bsz, t_len):
    d = x.shape[1]
    wb = w_in.astype(BF16)
    q, fl, i, gate = _proj(x, [wb[:, :d], wb[:, d:2 * d], wb[:, 2 * d:3 * d], wb[:, 3 * d:]],
                           [BF16, F32, BF16, BF16], "hgrn_proj")
    o = _hgrn_core(q, fl, i, gate, lower_bound.reshape(1, d), norm_g.reshape(1, HGRN_DIM), bsz, t_len)
    return _outproj_ln(o, w_out.astype(BF16), x, g, b, router_w, alpha, "hgrn_out_ln")


def _conv_kernel(x_ref, win_ref, cw_ref, wout_ref, g_ref, b_ref, *refs, alpha, tm):
    y_ref, zbuf = refs[3], refs[8]
    d = x_ref.shape[1]

    @pl.when(pl.program_id(1) == 0)
    def _():
        zbuf[0:8, :] = jnp.zeros((8, d), F32)

    x = x_ref[...]
    p = jnp.dot(x.astype(BF16), win_ref[...], preferred_element_type=F32)
    z = p[:, d:2 * d] * p[:, 2 * d:]
    zbuf[8:8 + tm, :] = z
    y = cw_ref[2:3, :] * z + cw_ref[1:2, :] * zbuf[7:7 + tm, :] + cw_ref[0:1, :] * zbuf[6:6 + tm, :]
    zbuf[0:8, :] = z[tm - 8:, :]
    mixed = jnp.dot((p[:, :d] * y).astype(BF16), wout_ref[...], preferred_element_type=F32)
    out = _layer_norm(alpha * x + mixed, g_ref[...], b_ref[...])
    y_ref[...] = out
    _route_rows(out, (pl.program_id(0) == 0) & (pl.program_id(1) == 0), *refs[:3], *refs[4:8])


def _conv_layer(x, w_in, conv_w, w_out, g, b, router_w, alpha, bsz, t_len, tm=512):
    n, d = x.shape
    nt = t_len // tm
    row = lambda bb, i: (bb * nt + i, 0)
    fixed = lambda bb, i: (0, 0)
    r_in, r_out, r_shape = _route_specs(n, d, tm, row, fixed)
    y, *routed = pl.pallas_call(
        functools.partial(_conv_kernel, alpha=alpha, tm=tm),
        grid=(bsz, nt),
        in_specs=[pl.BlockSpec((tm, d), row), pl.BlockSpec((d, 3 * d), fixed), pl.BlockSpec(conv_w.shape, fixed),
                  pl.BlockSpec((d, d), fixed), pl.BlockSpec((1, d), fixed), pl.BlockSpec((1, d), fixed)] + r_in,
        out_specs=[pl.BlockSpec((tm, d), row)] + r_out,
        out_shape=[jax.ShapeDtypeStruct((n, d), F32)] + r_shape,
        scratch_shapes=[pltpu.VMEM((tm + 8, d), F32)],
        compiler_params=_params(("arbitrary", "arbitrary"), 56),
        name="conv_layer",
    )(x, w_in.astype(BF16), conv_w, w_out.astype(BF16), g, b, *router_w)
    return y, routed


def _store_token_tiles(xt_ref, v, row0=0):
    rows = v.shape[0]
    for c in range(TOKEN_TILE_ROWS):
        xt_ref[pl.ds(row0 + c, rows, stride=TOKEN_TILE_ROWS), :] = v[:, c * LANES:(c + 1) * LANES]


def _load_token_tiles(xt_ref, rows, row0=0):
    return jnp.concatenate([xt_ref[pl.ds(row0 + c, rows, stride=TOKEN_TILE_ROWS), :]
                            for c in range(TOKEN_TILE_ROWS)], axis=1)


def _route_rows(y, first_step, wh_ref, wl_ref, b_ref, xt_ref, meta_ref, oh_ref, cnt_ref):
    @pl.when(first_step)
    def _():
        cnt_ref[...] = jnp.zeros_like(cnt_ref)

    for r0 in range(0, y.shape[0], MOE_TOK_TILE):
        x = y[r0:r0 + MOE_TOK_TILE, :]
        meta, onehot = _route(x, wh_ref, wl_ref, b_ref)
        meta_ref[r0:r0 + MOE_TOK_TILE, :] = meta
        oh_ref[r0:r0 + MOE_TOK_TILE, :] = onehot.astype(BF16)
        _store_token_tiles(xt_ref, x, r0 * TOKEN_TILE_ROWS)
        cnt_ref[...] += jnp.sum(onehot, axis=0, keepdims=True)


def _route(x, wh_ref, wl_ref, b_ref):
    xh = x.astype(BF16)
    xl = (x - xh.astype(F32)).astype(BF16)
    wh = wh_ref[...]
    logits = (jnp.dot(xh, wh, preferred_element_type=F32) + jnp.dot(xh, wl_ref[...], preferred_element_type=F32)
              + jnp.dot(xl, wh, preferred_element_type=F32)) + b_ref[...]
    lane = lax.broadcasted_iota(jnp.int32, logits.shape, 1)
    lane_f = lane.astype(F32)
    neg = -jnp.inf
    far = float(LANES)
    is_g = (lane >= ROUTER_GROUP_LANE0) & (lane < ROUTER_GROUP_LANE0 + N_GROUPS)
    gl = jnp.where(is_g, logits, neg)
    gmax = jnp.max(gl, axis=-1, keepdims=True)
    g_sel = jnp.min(jnp.where(gl == gmax, lane_f, far), axis=-1, keepdims=True) - float(ROUTER_GROUP_LANE0)
    grp_w = 1.0 / jnp.sum(jnp.where(is_g, jnp.exp(logits - gmax), 0.0), axis=-1, keepdims=True)
    in_g = (lane < N_EXPERTS) & ((lane // EXPERTS_PER_GROUP).astype(F32) == g_sel)
    el = jnp.where(in_g, logits, neg)
    t1 = jnp.max(el, axis=-1, keepdims=True)
    i1 = jnp.min(jnp.where(el == t1, lane_f, far), axis=-1, keepdims=True)
    el2 = jnp.where(lane_f == i1, neg, el)
    t2 = jnp.max(el2, axis=-1, keepdims=True)
    i2 = jnp.min(jnp.where(el2 == t2, lane_f, far), axis=-1, keepdims=True)
    e2 = jnp.exp(t2 - t1)
    w1 = grp_w / (1.0 + e2)
    w2 = w1 * e2
    first_lo = i1 < i2
    lo = jnp.where(first_lo, i1, i2) - g_sel * EXPERTS_PER_GROUP
    hi = jnp.where(first_lo, i2, i1) - g_sel * EXPERTS_PER_GROUP
    pair = lo * (EXPERTS_PER_GROUP - 1.0) - lo * (lo - 1.0) * 0.5 + (hi - lo - 1.0)
    onehot = jnp.where(lane_f == g_sel * MOE_PAIRS + pair, 1.0, 0.0)
    meta = (jnp.where(lane == 0, jnp.where(first_lo, w1, w2), 0.0)
            + jnp.where(lane == 1, jnp.where(first_lo, w2, w1), 0.0))
    return meta, onehot


def _router_weights(w_group, b_group, w_expert, b_expert):
    d = w_group.shape[0]
    w = jnp.zeros((d, LANES), F32).at[:, :N_EXPERTS].set(w_expert)
    w = w.at[:, ROUTER_GROUP_LANE0:ROUTER_GROUP_LANE0 + N_GROUPS].set(w_group)
    bias = jnp.zeros((1, LANES), F32).at[0, :N_EXPERTS].set(b_expert)
    bias = bias.at[0, ROUTER_GROUP_LANE0:ROUTER_GROUP_LANE0 + N_GROUPS].set(b_group)
    wh = w.astype(BF16)
    return wh, (w - wh.astype(F32)).astype(BF16), bias


def _route_specs(n, d, tm, row, fixed):
    assert tm % MOE_TOK_TILE == 0 and d == TOKEN_TILE_ROWS * LANES
    in_specs = [pl.BlockSpec((d, LANES), fixed), pl.BlockSpec((d, LANES), fixed), pl.BlockSpec((1, LANES), fixed)]
    out_specs = [pl.BlockSpec((tm * TOKEN_TILE_ROWS, LANES), row), pl.BlockSpec((tm, LANES), row),
                 pl.BlockSpec((tm, LANES), row), pl.BlockSpec((8, LANES), fixed)]
    out_shape = [jax.ShapeDtypeStruct((n * TOKEN_TILE_ROWS, LANES), F32), jax.ShapeDtypeStruct((n, LANES), F32),
                 jax.ShapeDtypeStruct((n, LANES), BF16), jax.ShapeDtypeStruct((8, LANES), F32)]
    return in_specs, out_specs, out_shape


def _moe_pos_kernel(oh_ref, base_ref, lt_ref, pos_ref, carry_ref):
    @pl.when(pl.program_id(0) == 0)
    def _():
        carry_ref[...] = jnp.zeros_like(carry_ref)

    oh = oh_ref[...]
    rank = jnp.dot(lt_ref[...], oh, preferred_element_type=F32) + carry_ref[...] + base_ref[...]
    ohf = oh.astype(F32)
    val = ohf * rank
    carry_ref[...] += jnp.sum(ohf, axis=0, keepdims=True)
    hi = jnp.floor(val * (1.0 / 256.0))
    lo = val - 256.0 * hi
    ones = jnp.ones((8, LANES), BF16)
    pos = (256.0 * lax.dot_general(ones, hi.astype(BF16), _NT, preferred_element_type=F32)
           + lax.dot_general(ones, lo.astype(BF16), _NT, preferred_element_type=F32))
    pos_ref[0] = pos[0:1].astype(jnp.int32)


def _moe_pos(onehot, base, tm=MOE_TOK_TILE):
    n = onehot.shape[0]
    lt = jnp.asarray(np.tril(np.ones((tm, tm), np.float32), -1), BF16)
    return pl.pallas_call(
        _moe_pos_kernel,
        grid=(n // tm,),
        in_specs=[pl.BlockSpec((tm, LANES), lambda i: (i, 0)), pl.BlockSpec((1, LANES), lambda i: (0, 0)),
                  pl.BlockSpec((tm, tm), lambda i: (0, 0))],
        out_specs=pl.BlockSpec((1, 1, tm), lambda i: (i, 0, 0)),
        out_shape=jax.ShapeDtypeStruct((n // tm, 1, tm), jnp.int32),
        scratch_shapes=[pltpu.VMEM((1, LANES), F32)],
        compiler_params=_params(("arbitrary",), 32),
        name="moe_pos",
    )(onehot, base, lt)


def _tile_copy(src, src_tok, dst, dst_tok, sem):
    rows = TOKEN_TILE_ROWS
    return pltpu.make_async_copy(src.at[pl.ds(pl.multiple_of(src_tok * rows, rows), rows)],
                                 dst.at[pl.ds(pl.multiple_of(dst_tok * rows, rows), rows)], sem)


def _meta_copy(src, src_tok, dst, dst_tok, sem):
    return pltpu.make_async_copy(src.at[pl.ds(src_tok, 1)], dst.at[pl.ds(dst_tok, 1)], sem)


def _moe_scatter_kernel(pos_ref, xt_ref, meta_ref, xs_in_ref, xm_in_ref, xs_ref, xm_ref, sem, *, tm):
    del xs_in_ref, xm_in_ref

    def start(r2, c):
        for par in range(2):
            r = 2 * r2 + par
            p = pos_ref[0, 0, r]
            _tile_copy(xt_ref, r, xs_ref, p, sem.at[0]).start(priority=par)
            _meta_copy(meta_ref, r, xm_ref, p, sem.at[1]).start(priority=1 - par)
        return c

    lax.fori_loop(0, tm // 2, start, 0, unroll=4)
    pltpu.make_async_copy(xt_ref, xs_ref.at[pl.ds(0, tm * TOKEN_TILE_ROWS)], sem.at[0]).wait()
    pltpu.make_async_copy(meta_ref, xm_ref.at[pl.ds(0, tm)], sem.at[1]).wait()


def _moe_scatter(pos, xt, meta, xs_buf, xm_buf, tm=2 * MOE_TOK_TILE):
    n = meta.shape[0]
    pos = pos.reshape(n // tm, 1, tm)
    return pl.pallas_call(
        functools.partial(_moe_scatter_kernel, tm=tm),
        grid=(n // tm,),
        in_specs=[pl.BlockSpec((1, 1, tm), lambda i: (i, 0, 0), memory_space=pltpu.SMEM),
                  pl.BlockSpec((tm * TOKEN_TILE_ROWS, LANES), lambda i: (i, 0)),
                  pl.BlockSpec((tm, LANES), lambda i: (i, 0)),
                  pl.BlockSpec(memory_space=pl.ANY), pl.BlockSpec(memory_space=pl.ANY)],
        out_specs=[pl.BlockSpec(memory_space=pl.ANY), pl.BlockSpec(memory_space=pl.ANY)],
        out_shape=[jax.ShapeDtypeStruct(xs_buf.shape, xs_buf.dtype), jax.ShapeDtypeStruct(xm_buf.shape, xm_buf.dtype)],
        scratch_shapes=[pltpu.SemaphoreType.DMA((2,))],
        input_output_aliases={3: 0, 4: 1},
        compiler_params=_params(("arbitrary",), 32),
        name="moe_scatter",
    )(pos, xt, meta, xs_buf, xm_buf)


def _moe_experts_kernel(grp_ref, e1_ref, e2_ref, nused_ref, xs_ref, xm_ref, wg_ref, wu_ref, wd_ref, ys_in_ref, ys_ref):
    del grp_ref, ys_in_ref
    i = pl.program_id(0)
    tr = xm_ref.shape[0] // MOE_TILES_PER_STEP

    @pl.when(i * MOE_TILES_PER_STEP < nused_ref[0])
    def _():
        for t in range(MOE_TILES_PER_STEP):
            x = _load_token_tiles(xs_ref, tr, t * tr * TOKEN_TILE_ROWS).astype(BF16)
            meta = xm_ref[t * tr:(t + 1) * tr, :]
            y = None
            for k, e_ref in enumerate((e1_ref, e2_ref)):
                e = e_ref[i * MOE_TILES_PER_STEP + t]
                hg = jnp.dot(x, wg_ref[0, e], preferred_element_type=F32)
                hu = jnp.dot(x, wu_ref[0, e], preferred_element_type=F32)
                h = hg * _sigmoid(hg) * hu * meta[:, k:k + 1]
                yk = jnp.dot(h.astype(BF16), wd_ref[0, e], preferred_element_type=F32)
                y = yk if y is None else y + yk
            _store_token_tiles(ys_ref, y, t * tr * TOKEN_TILE_ROWS)


def _moe_experts(xs_buf, xm_buf, ys_buf, tile_grp, tile_e1, tile_e2, n_used, w_gate, w_up, w_down, layer,
                 tr=MOE_ROW_TILE):
    r_max = xm_buf.shape[0]
    d, f = w_gate.shape[-2:]
    assert d == TOKEN_TILE_ROWS * LANES
    tps = MOE_TILES_PER_STEP
    step_rows = tps * tr
    tile_rows = step_rows * TOKEN_TILE_ROWS
    epg = EXPERTS_PER_GROUP
    used = lambda i, grp, e1, e2, nu: (jnp.minimum(i, nu[0] // tps - 1), 0)
    wmap = lambda i, grp, e1, e2, nu: (layer * N_GROUPS + grp[i * tps], 0, 0, 0)
    grid_spec = pltpu.PrefetchScalarGridSpec(
        num_scalar_prefetch=4,
        grid=(r_max // step_rows,),
        in_specs=[pl.BlockSpec((tile_rows, LANES), used), pl.BlockSpec((step_rows, LANES), used),
                  pl.BlockSpec((1, epg, d, f), wmap), pl.BlockSpec((1, epg, d, f), wmap),
                  pl.BlockSpec((1, epg, f, d), wmap), pl.BlockSpec(memory_space=pl.ANY)],
        out_specs=pl.BlockSpec((tile_rows, LANES), used),
    )
    return pl.pallas_call(
        _moe_experts_kernel,
        grid_spec=grid_spec,
        out_shape=jax.ShapeDtypeStruct(ys_buf.shape, ys_buf.dtype),
        input_output_aliases={9: 0},
        compiler_params=_params(("arbitrary",), 48),
        name="moe_experts",
    )(tile_grp, tile_e1, tile_e2, n_used, xs_buf, xm_buf, w_gate, w_up, w_down, ys_buf)


def _moe_combine_ln_kernel(pos_ref, nxt_ref, x_ref, ys_ref, g_ref, b_ref, y_ref, buf_ref, sem, *, alpha, tm):
    i = pl.program_id(0)
    slot = i % 2

    def gather(p_ref, s):
        def start(r2, c):
            for par in range(2):
                r = 2 * r2 + par
                _tile_copy(ys_ref, p_ref[0, 0, r], buf_ref, s * tm + r, sem.at[s]).start(priority=par)
            return c
        lax.fori_loop(0, tm // 2, start, 0, unroll=4)

    @pl.when(i == 0)
    def _():
        gather(pos_ref, 0)

    @pl.when(i + 1 < pl.num_programs(0))
    def _():
        gather(nxt_ref, 1 - slot)

    base = pl.multiple_of(slot * (tm * TOKEN_TILE_ROWS), tm * TOKEN_TILE_ROWS)
    slot_rows = pl.ds(base, tm * TOKEN_TILE_ROWS)
    pltpu.make_async_copy(ys_ref.at[pl.ds(0, tm * TOKEN_TILE_ROWS)], buf_ref.at[slot_rows], sem.at[slot]).wait()
    ffn = _load_token_tiles(buf_ref, tm, base)
    y_ref[...] = _layer_norm(alpha * x_ref[...] + ffn, g_ref[...], b_ref[...])


def _moe_combine_ln(pos, x, ys_buf, g, b, alpha, tm=MOE_TOK_TILE):
    n, d = x.shape
    nt = n // tm
    row = lambda i: (i, 0)
    fixed = lambda i: (0, 0)
    return pl.pallas_call(
        functools.partial(_moe_combine_ln_kernel, alpha=alpha, tm=tm),
        grid=(nt,),
        in_specs=[pl.BlockSpec((1, 1, tm), lambda i: (i, 0, 0), memory_space=pltpu.SMEM),
                  pl.BlockSpec((1, 1, tm), lambda i: (jnp.minimum(i + 1, nt - 1), 0, 0), memory_space=pltpu.SMEM),
                  pl.BlockSpec((tm, d), row), pl.BlockSpec(memory_space=pl.ANY),
                  pl.BlockSpec((1, d), fixed), pl.BlockSpec((1, d), fixed)],
        out_specs=pl.BlockSpec((tm, d), row),
        out_shape=jax.ShapeDtypeStruct((n, d), F32),
        scratch_shapes=[pltpu.VMEM((2 * tm * TOKEN_TILE_ROWS, LANES), F32), pltpu.SemaphoreType.DMA((2,))],
        compiler_params=_params(("arbitrary",), 32),
        name="moe_combine_ln",
    )(pos, pos, x, ys_buf, g, b)


def _bucket_experts():
    lo, hi = [], []
    for a in range(EXPERTS_PER_GROUP):
        for c in range(a + 1, EXPERTS_PER_GROUP):
            lo.append(a)
            hi.append(c)
    grp = np.repeat(np.arange(N_GROUPS), MOE_PAIRS).astype(np.int32)
    return grp, np.tile(lo, N_GROUPS).astype(np.int32), np.tile(hi, N_GROUPS).astype(np.int32)


def _moe_layer(x, routed, bufs, w_gate, w_up, w_down, layer, g, b, alpha):
    tr = MOE_ROW_TILE
    xs_buf, xm_buf, ys_buf = bufs
    xt, meta, onehot, cnt = routed
    counts = cnt[0, :MOE_BUCKETS].astype(jnp.int32)
    padded = ((counts + (tr - 1)) // tr * tr).reshape(N_GROUPS, MOE_PAIRS)
    step_rows = MOE_TILES_PER_STEP * tr
    slack = -jnp.sum(padded, axis=1) % step_rows
    padded = padded.at[:, MOE_PAIRS - 1].add(slack).reshape(MOE_BUCKETS)
    ends = jnp.cumsum(padded)
    base = jnp.zeros((1, LANES), F32).at[0, :MOE_BUCKETS].set((ends - padded).astype(F32))
    n_used = ends[-1] // tr
    n_tiles = xm_buf.shape[0] // tr
    tile = jnp.minimum(jnp.arange(n_tiles, dtype=jnp.int32), n_used - 1)
    tile_bkt = jnp.sum((ends[None, :] <= (tile * tr)[:, None]).astype(jnp.int32), axis=1)
    tile_bkt = jnp.minimum(tile_bkt, MOE_BUCKETS - 1)
    b_grp, b_lo, b_hi = (jnp.asarray(t)[tile_bkt] for t in _bucket_experts())
    pos = _moe_pos(onehot, base)
    xs_buf, xm_buf = _moe_scatter(pos, xt, meta, xs_buf, xm_buf)
    ys_buf = _moe_experts(xs_buf, xm_buf, ys_buf, b_grp, b_lo, b_hi, n_used.reshape(1).astype(jnp.int32),
                          w_gate, w_up, w_down, layer)
    return _moe_combine_ln(pos, x, ys_buf, g, b, alpha), (xs_buf, xm_buf, ys_buf)


def kernel(x, ln_mix_g, ln_mix_b, ln_ffn_g, ln_ffn_b, fox_w_in, fox_b_f, fox_w_out, hgrn_w_in, hgrn_lb_logits, hgrn_norm_g, hgrn_w_out, conv_w_in, conv_w, conv_w_out, moe_w_group, moe_b_group, moe_w_expert, moe_b_expert, moe_w_gate, moe_w_up, moe_w_down):
    bsz, t_len, d = x.shape
    depth = ln_mix_g.shape[0]
    alpha = float((2 * depth) ** 0.25)
    assert d == FOX_HEADS * FOX_HEAD_DIM == HGRN_HEADS * HGRN_DIM
    assert t_len % 512 == 0 and (bsz * t_len) % 1024 == 0

    lb_prob = jax.nn.softmax(hgrn_lb_logits.astype(F32), axis=0)
    lower_bounds = jnp.cumsum(lb_prob, axis=0) - lb_prob[0]

    h = x.reshape(bsz * t_len, d)
    r_max = bsz * t_len + MOE_BUCKETS * MOE_ROW_TILE + N_GROUPS * MOE_TILES_PER_STEP * MOE_ROW_TILE
    assert r_max % (MOE_TILES_PER_STEP * MOE_ROW_TILE) == 0
    by_group = lambda w: w.astype(BF16).reshape((depth * N_GROUPS, EXPERTS_PER_GROUP) + w.shape[2:])
    moe_w_gate, moe_w_up, moe_w_down = by_group(moe_w_gate), by_group(moe_w_up), by_group(moe_w_down)
    bufs = (jnp.zeros((r_max * TOKEN_TILE_ROWS, LANES), F32), jnp.zeros((r_max, LANES), F32),
            jnp.zeros((r_max * TOKEN_TILE_ROWS, LANES), F32))
    for layer in range(depth):
        kind, j = layer % 3, layer // 3
        g_mix, b_mix = ln_mix_g[layer].reshape(1, d), ln_mix_b[layer].reshape(1, d)
        router_w = _router_weights(moe_w_group[layer], moe_b_group[layer], moe_w_expert[layer], moe_b_expert[layer])
        if kind == 0:
            h, routed = _fox_layer(h, fox_w_in[j], fox_b_f[j], fox_w_out[j], g_mix, b_mix, router_w,
                                   alpha, bsz, t_len)
        elif kind == 1:
            h, routed = _hgrn_layer(h, hgrn_w_in[j], lower_bounds[layer], hgrn_norm_g[j], hgrn_w_out[j],
                                    g_mix, b_mix, router_w, alpha, bsz, t_len)
        else:
            h, routed = _conv_layer(h, conv_w_in[j], conv_w[j], conv_w_out[j], g_mix, b_mix, router_w,
                                    alpha, bsz, t_len)
        h, bufs = _moe_layer(h, routed, bufs, moe_w_gate, moe_w_up, moe_w_down, layer,
                             ln_ffn_g[layer].reshape(1, d), ln_ffn_b[layer].reshape(1, d), alpha)
    return h.reshape(bsz, t_len, d)
```
